```python
import math
import jax, jax.numpy as jnp
from jax import lax
import numpy as np

D_MODEL = 2048
BATCH = 2
SEQ = 4096
DEPTH = 1
DEC_BATCH = 4
DEC_SEQ = 2048
PAST_LEN = 128

RW_HEAD = 64
RW_HEADS = D_MODEL // RW_HEAD
RW_WIDTH = RW_HEADS * RW_HEAD
DECAY_LORA = 96
ICL_LORA = 96
GATE_LORA = 256
GN_EPS = 64e-5
DA_QK = 64
DA_V = 2 * DA_QK
DA_HEADS = D_MODEL // DA_V
DA_WIDTH = DA_HEADS * DA_V
Q_BLOCK = 128
N_EXPERTS = 16
CAPACITY_FACTOR = 2
EXPERT_FF = int(256 * math.ceil(8 * D_MODEL / 3 / 256))
EPS = 1e-6

RW_SHIFT_COLS = 3 * RW_WIDTH + 2 * DECAY_LORA + 2 * ICL_LORA + GATE_LORA
RW_SPLIT_IDX = tuple(int(i) for i in np.cumsum([RW_WIDTH] * 3 + [DECAY_LORA] * 2 + [ICL_LORA] * 2))
DA_SPLIT_IDX = tuple(int(i) for i in np.cumsum([DA_WIDTH] * 3 + [D_MODEL]))
N_IN = RW_SHIFT_COLS + 3 * DA_WIDTH + 2 * D_MODEL

kernel_name = "hybrid_rwkv7_diffattn_ec_moe_encoder"


def rmsnorm(x, g, eps=EPS):
    xf = x.astype(jnp.float32)
    y = xf * lax.rsqrt(jnp.mean(xf * xf, axis=-1, keepdims=True) + eps)
    return (y * g.astype(jnp.float32)).astype(x.dtype)


def centred_shift(z):
    prev = jnp.pad(z[:, :-1], ((0, 0), (1, 0), (0, 0)))
    nxt = jnp.pad(z[:, 1:], ((0, 0), (0, 1), (0, 0)))
    return 0.5 * (prev + nxt)


def rwkv7_scan(r, w, k, v, a, b):
    G, H, N = r.shape[1], r.shape[2], r.shape[3]

    def step(S, inp):
        r_t, w_t, k_t, v_t, a_t, b_t = inp
        sa = jnp.einsum('ghvk,ghk->ghv', S, a_t)
        S = S * w_t[:, :, None, :] + sa[..., None] * b_t[:, :, None, :] + v_t[..., None] * k_t[:, :, None, :]
        y = jnp.einsum('ghvk,ghk->ghv', S, r_t)
        return S, y

    S0 = jnp.zeros((G, H, N, N), jnp.float32)
    _, ys = lax.scan(step, S0, (r, w, k, v, a, b))
    return ys


def rwkv7_branch(r, k, v, wd, ad, gd, w0, w2, a0, a2, g2, k_k, k_a, r_k, lnx_w, lnx_b):
    B, T, _ = r.shape
    f32 = jnp.float32
    heads = lambda t: t.reshape(t.shape[:-1] + (RW_HEADS, RW_HEAD))
    w_log = -jax.nn.softplus(-(w0[:, None, None, :] + jnp.einsum('zbtr,zrc->zbtc', jnp.tanh(wd), w2))) - 0.5
    decay = jnp.exp(-jnp.exp(w_log.astype(f32)))
    icl = jax.nn.sigmoid((a0[:, None, None, :] + jnp.einsum('zbtr,zrc->zbtc', ad, a2)).astype(f32))
    g = jax.nn.sigmoid(gd) @ g2
    kk = heads(k * k_k).astype(f32)
    kk = kk / jnp.maximum(jnp.sqrt(jnp.sum(kk * kk, axis=-1, keepdims=True)), 1e-12)
    kd = k[None].astype(f32) * (1.0 + (icl - 1.0) * k_a.astype(f32))
    rh, vh = heads(r).astype(f32), heads(v).astype(f32)
    kdh, iclh, decayh = heads(kd), heads(icl), heads(decay)

    def both(fwd, bwd):
        return jnp.moveaxis(jnp.concatenate([fwd, jnp.flip(bwd, axis=1)], axis=0), 1, 0)

    ys = rwkv7_scan(both(rh, rh), both(decayh[0], decayh[1]), both(kdh[0], kdh[1]), both(vh, vh),
                    -both(kk, kk), both(kk * iclh[0], kk * iclh[1]))
    ys = jnp.moveaxis(ys, 0, 1)
    y = ys[:B] + jnp.flip(ys[B:], axis=1)
    mean = jnp.mean(y, axis=-1, keepdims=True)
    var = jnp.mean(jnp.square(y - mean), axis=-1, keepdims=True)
    yn = (y - mean) * lax.rsqrt(var + GN_EPS) * heads(lnx_w).astype(f32) + heads(lnx_b).astype(f32)
    bonus = jnp.sum(jnp.sum(rh[None] * kdh * r_k.astype(f32), axis=-1, keepdims=True), axis=0) * vh
    return (yn + bonus).reshape(B, T, RW_WIDTH).astype(r.dtype) * g


def diff_attention(q, k, v, lam_q1, lam_k1, lam_q2, lam_k2, sub_g, lambda_init):
    B, T, _ = q.shape
    f32 = jnp.float32
    q = q.reshape(B, T, DA_HEADS, 2, DA_QK) * (DA_QK ** -0.5)
    k = k.reshape(B, T, DA_HEADS, 2, DA_QK)
    v = v.reshape(B, T, DA_HEADS, DA_V)
    lam = (jnp.exp(jnp.sum(lam_q1 * lam_k1).astype(f32)) - jnp.exp(jnp.sum(lam_q2 * lam_k2).astype(f32))
           + lambda_init)
    slopes = 2.0 ** (-8.0 * jnp.arange(1, DA_HEADS + 1, dtype=f32) / DA_HEADS)
    n_blk = T // Q_BLOCK
    qb = jnp.moveaxis(q.reshape(B, n_blk, Q_BLOCK, DA_HEADS, 2, DA_QK), 1, 0)
    pos_k = jnp.arange(T)

    def block(args):
        q_blk, i = args
        pos_q = i * Q_BLOCK + jnp.arange(Q_BLOCK)
        dist = jnp.abs(pos_q[:, None] - pos_k[None, :]).astype(f32)
        bias = -slopes[:, None, None] * dist
        s = jnp.einsum('bqhcd,bkhcd->bhcqk', q_blk, k).astype(f32) + bias[None, :, None]
        p = jax.nn.softmax(s, axis=-1)
        p_diff = p[:, :, 0] - lam * p[:, :, 1]
        return jnp.einsum('bhqk,bkhe->bqhe', p_diff.astype(v.dtype), v)

    o = lax.map(block, (qb, jnp.arange(n_blk)))
    o = jnp.moveaxis(o, 0, 1).reshape(B, T, DA_HEADS, DA_V)
    o = rmsnorm(o, sub_g) * (1.0 - lambda_init)
    return o.reshape(B, T, DA_WIDTH)


def expert_choice_ffn(h, w_router, w_gate, w_up, w_down):
    B, T, D = h.shape
    n_tok = B * T
    cap = max(1, CAPACITY_FACTOR * n_tok // N_EXPERTS)
    tokens = h.reshape(n_tok, D)
    affinity = jax.nn.softmax((tokens @ w_router).astype(jnp.float32), axis=-1)
    gate_vals, idx = lax.top_k(affinity.T, cap)
    xe = tokens[idx]
    hid = jax.nn.silu(jnp.einsum('ecd,edf->ecf', xe, w_gate)) * jnp.einsum('ecd,edf->ecf', xe, w_up)
    ye = jnp.einsum('ecf,efd->ecd', hid, w_down) * gate_vals[..., None].astype(h.dtype)
    out = jnp.zeros((n_tok, D), ye.dtype).at[idx.reshape(-1)].add(ye.reshape(-1, D))
    return out.reshape(B, T, D)


def encoder_layer(x, c, lambda_init, w_ada, b_ada, norm1_g, w_in, mu_shift, w0, w2, a0, a2, g2,
                  k_k, k_a, r_k, lnx_w, lnx_b, lam_q1, lam_k1, lam_q2, lam_k2, sub_g, w_out,
                  norm2_g, w_router, w_gate, w_up, w_down):
    ada = jax.nn.silu(c) @ w_ada + b_ada
    shift1, scale1, gate1, shift2, scale2, gate2 = [t[:, None, :] for t in jnp.split(ada, 6, axis=-1)]
    h = rmsnorm(x, norm1_g) * (1.0 + scale1) + shift1
    z = h @ w_in
    z_rw = z[..., :RW_SHIFT_COLS]
    z_rw = z_rw + mu_shift * (centred_shift(z_rw) - z_rw)
    r, k_rw, v_rw, wd_f, wd_b, ad_f, ad_b, gd = jnp.split(z_rw, RW_SPLIT_IDX, axis=-1)
    q_da, k_da, v_da, gate_a, gate_b = jnp.split(z[..., RW_SHIFT_COLS:], DA_SPLIT_IDX, axis=-1)
    y_a = rwkv7_branch(r, k_rw, v_rw, jnp.stack([wd_f, wd_b]), jnp.stack([ad_f, ad_b]), gd,
                       w0, w2, a0, a2, g2, k_k, k_a, r_k, lnx_w, lnx_b)
    y_b = diff_attention(q_da, k_da, v_da, lam_q1, lam_k1, lam_q2, lam_k2, sub_g, lambda_init)
    merged = jax.nn.sigmoid(gate_a) * y_a + jax.nn.sigmoid(gate_b) * y_b
    x = x + gate1 * (merged @ w_out)
    h2 = rmsnorm(x, norm2_g) * (1.0 + scale2) + shift2
    x = x + gate2 * expert_choice_ffn(h2, w_router, w_gate, w_up, w_down)
    return x


def setup_inputs(seed: int = 0) -> dict:
    key = jax.random.key(seed)
    ks = iter(jax.random.split(key, 40))
    nrm = lambda shape, scale: jax.random.normal(next(ks), shape, jnp.float32) * scale
    uni = lambda shape, lo, hi: jax.random.uniform(next(ks), shape, jnp.float32, lo, hi)
    L, D, E, F = DEPTH, D_MODEL, N_EXPERTS, EXPERT_FF
    return {
        "x_prompt": nrm((BATCH, SEQ, D), 1.0),
        "x_sample": nrm((DEC_BATCH, DEC_SEQ, D), 1.0),
        "c_prompt": nrm((BATCH, D), 1.0),
        "c_sample": nrm((DEC_BATCH, D), 1.0),
        "w_ada": nrm((L, D, 6 * D), 0.5 * D ** -0.5),
        "b_ada": nrm((L, 6 * D), 0.02),
        "norm1_g": 1.0 + nrm((L, D), 0.05),
        "w_in": nrm((L, D, N_IN), D ** -0.5),
        "mu_shift": uni((L, RW_SHIFT_COLS), 0.0, 1.0),
        "w0": uni((L, 2, RW_WIDTH), -6.0, 0.0),
        "w2": nrm((L, 2, DECAY_LORA, RW_WIDTH), 0.1 * DECAY_LORA ** -0.5),
        "a0": nrm((L, 2, RW_WIDTH), 0.5),
        "a2": nrm((L, 2, ICL_LORA, RW_WIDTH), 0.1 * ICL_LORA ** -0.5),
        "g2": nrm((L, GATE_LORA, RW_WIDTH), GATE_LORA ** -0.5),
        "k_k": 0.85 + nrm((L, RW_WIDTH), 0.05),
        "k_a": 1.0 + nrm((L, RW_WIDTH), 0.05),
        "r_k": nrm((L, RW_HEADS, RW_HEAD), 0.1),
        "lnx_w": 1.0 + nrm((L, RW_WIDTH), 0.05),
        "lnx_b": nrm((L, RW_WIDTH), 0.02),
        "lam_q1": nrm((L, DA_QK), 0.1),
        "lam_k1": nrm((L, DA_QK), 0.1),
        "lam_q2": nrm((L, DA_QK), 0.1),
        "lam_k2": nrm((L, DA_QK), 0.1),
        "sub_g": 1.0 + nrm((L, DA_V), 0.05),
        "w_out": nrm((L, D, D), D ** -0.5),
        "norm2_g": 1.0 + nrm((L, D), 0.05),
        "w_router": nrm((L, D, E), D ** -0.5),
        "w_gate": nrm((L, E, D, F), D ** -0.5),
        "w_up": nrm((L, E, D, F), D ** -0.5),
        "w_down": nrm((L, E, F, D), F ** -0.5),
        "final_g": 1.0 + nrm((D,), 0.05),
    }


def reference(x_prompt, x_sample, c_prompt, c_sample, w_ada, b_ada, norm1_g, w_in, mu_shift,
              w0, w2, a0, a2, g2, k_k, k_a, r_k, lnx_w, lnx_b, lam_q1, lam_k1, lam_q2, lam_k2,
              sub_g, w_out, norm2_g, w_router, w_gate, w_up, w_down, final_g):
    def run(x, c):
        for l in range(DEPTH):
            lambda_init = 0.8 - 0.6 * math.exp(-0.3 * l)
            x = encoder_layer(x, c, lambda_init, w_ada[l], b_ada[l], norm1_g[l], w_in[l], mu_shift[l],
                              w0[l], w2[l], a0[l], a2[l], g2[l], k_k[l], k_a[l], r_k[l], lnx_w[l], lnx_b[l],
                              lam_q1[l], lam_k1[l], lam_q2[l], lam_k2[l], sub_g[l], w_out[l],
                              norm2_g[l], w_router[l], w_gate[l], w_up[l], w_down[l])
        return rmsnorm(x, final_g)

    y_prompt = run(x_prompt, c_prompt)
    y_sample = run(x_sample, c_sample)
    return (y_prompt, y_sample)
```

```python
import functools
import math

import jax
import jax.numpy as jnp
from jax import lax
from jax.experimental import pallas as pl
from jax.experimental.pallas import tpu as pltpu

F32 = jnp.float32
BF16 = jnp.bfloat16
I32 = jnp.int32

RW_HEAD = 64
DA_QK = 64
DA_V = 2 * DA_QK
RW_CHUNK = 64
CAPACITY_FACTOR = 2
EPS = 1e-6
GN_EPS = RW_HEAD * 1e-5
LANES = 128
SUBLANES = 8
VMEM_LIMIT = 56 * 1024 * 1024


def _cparams(*sem):
    return pltpu.CompilerParams(dimension_semantics=sem, vmem_limit_bytes=VMEM_LIMIT)


def _pick(n, pref, align):
    if n <= pref:
        return n
    t = (pref // align) * align
    while t >= align:
        if n % t == 0:
            return t
        t -= align
    return n


_NN = (((1,), (0,)), ((), ()))
_NT = (((1,), (1,)), ((), ()))
_TN = (((0,), (0,)), ((), ()))


def _dg(a, b, dims):
    return lax.dot_general(a, b, dims, preferred_element_type=F32)


def _dot1(a, b, dims=_NN):
    return _dg(a.astype(BF16), b.astype(BF16), dims)


def _split(x):
    hi = x.astype(BF16)
    lo = (x - hi.astype(F32)).astype(BF16)
    return hi, lo


def _dot3(a, b, dims=_NN):
    ah, al = _split(a)
    bh, bl = _split(b)
    return _dg(ah, bh, dims) + (_dg(ah, bl, dims) + _dg(al, bh, dims))


def _dot_exact_rhs(a, b_bf16, dims=_NN):
    ah, al = _split(a)
    return _dg(ah, b_bf16, dims) + _dg(al, b_bf16, dims)


def _sigmoid(x):
    return 1.0 / (1.0 + jnp.exp(-x))


def _silu(x):
    return x * _sigmoid(x)


def _ada_kernel(c_ref, w_ref, b_ref, o_ref):
    o_ref[...] = _dot1(_silu(c_ref[...]), w_ref[...]) + b_ref[...]


def _ada(c, w, b):
    m, d = c.shape
    n = w.shape[1]
    tn = _pick(n, 1024, LANES)
    return pl.pallas_call(
        _ada_kernel,
        grid=(n // tn,),
        in_specs=[pl.BlockSpec((m, d), lambda j: (0, 0)),
                  pl.BlockSpec((d, tn), lambda j: (0, j)),
                  pl.BlockSpec((1, tn), lambda j: (0, j))],
        out_specs=pl.BlockSpec((m, tn), lambda j: (0, j)),
        out_shape=jax.ShapeDtypeStruct((m, n), F32),
        compiler_params=_cparams("arbitrary"),
        name="ada",
    )(c, w, b.reshape(1, n))


def _norm_mod_kernel(x_ref, g_ref, sc_ref, sh_ref, o_ref):
    x = x_ref[0]
    y = x * lax.rsqrt(jnp.mean(x * x, axis=-1, keepdims=True) + EPS) * g_ref[...]
    o_ref[0] = (y * (1.0 + sc_ref[0]) + sh_ref[0]).astype(o_ref.dtype)


def _norm_mod(x, g, scale, shift):
    bsz, t, d = x.shape
    tt = _pick(t, 512, SUBLANES)
    row = pl.BlockSpec((1, 1, d), lambda b, i: (b, 0, 0))
    return pl.pallas_call(
        _norm_mod_kernel,
        grid=(bsz, t // tt),
        in_specs=[pl.BlockSpec((1, tt, d), lambda b, i: (b, i, 0)),
                  pl.BlockSpec((1, d), lambda b, i: (0, 0)), row, row],
        out_specs=pl.BlockSpec((1, tt, d), lambda b, i: (b, i, 0)),
        out_shape=jax.ShapeDtypeStruct((bsz, t, d), BF16),
        compiler_params=_cparams("arbitrary", "arbitrary"),
        name="norm_mod",
    )(x, g.reshape(1, d), scale, shift)


def _mm_kernel(a_ref, b_ref, o_ref):
    o_ref[...] = _dot1(a_ref[...], b_ref[...]).astype(o_ref.dtype)


def _matmul(a, b, out_dtype=F32, tm_pref=1024, tn_pref=1024):
    m, k = a.shape
    n = b.shape[1]
    tm = _pick(m, tm_pref, SUBLANES)
    tn = _pick(n, tn_pref, LANES)
    return pl.pallas_call(
        _mm_kernel,
        grid=(m // tm, n // tn),
        in_specs=[pl.BlockSpec((tm, k), lambda i, j: (i, 0)),
                  pl.BlockSpec((k, tn), lambda i, j: (0, j))],
        out_specs=pl.BlockSpec((tm, tn), lambda i, j: (i, j)),
        out_shape=jax.ShapeDtypeStruct((m, n), out_dtype),
        compiler_params=_cparams("arbitrary", "arbitrary"),
        name="matmul",
    )(a, b)


def _mm_res_kernel(a_ref, b_ref, x_ref, g_ref, o_ref):
    o_ref[0] = x_ref[0] + g_ref[0] * _dot1(a_ref[0], b_ref[...])


def _matmul_residual(a, b, x, gate):
    bsz, t, k = a.shape
    n = b.shape[1]
    tm = _pick(t, 1024, SUBLANES)
    tn = _pick(n, 1024, LANES)
    return pl.pallas_call(
        _mm_res_kernel,
        grid=(bsz, t // tm, n // tn),
        in_specs=[pl.BlockSpec((1, tm, k), lambda bb, i, j: (bb, i, 0)),
                  pl.BlockSpec((k, tn), lambda bb, i, j: (0, j)),
                  pl.BlockSpec((1, tm, tn), lambda bb, i, j: (bb, i, j)),
                  pl.BlockSpec((1, 1, tn), lambda bb, i, j: (bb, 0, j))],
        out_specs=pl.BlockSpec((1, tm, tn), lambda bb, i, j: (bb, i, j)),
        out_shape=jax.ShapeDtypeStruct((bsz, t, n), F32),
        compiler_params=_cparams("arbitrary", "arbitrary", "arbitrary"),
        name="matmul_residual",
    )(a, b, x, gate)


def _shift_tile(z, prev8, next8, mu, i, n_t):
    tt = z.shape[0]
    pr = jnp.where(i > 0, prev8[SUBLANES - 1:SUBLANES, :], 0.0)
    nx = jnp.where(i < n_t - 1, next8[0:1, :], 0.0)
    ridx = lax.broadcasted_iota(I32, z.shape, 0)
    zp = jnp.where(ridx == 0, pr, pltpu.roll(z, 1, 0))
    zn = jnp.where(ridx == tt - 1, nx, pltpu.roll(z, tt - 1, 0))
    return z + mu * (0.5 * (zp + zn) - z)


def _seg_sum(x, e_ref, et_ref):
    s = _dot_exact_rhs(x, e_ref[...])
    return _dot_exact_rhs(s, et_ref[...])


def _rw_prep_kernel(zr_ref, zrp_ref, zrn_ref, zl_ref, zlp_ref, zln_ref, mur_ref, mul_ref,
                    w0_ref, w2_ref, a0_ref, a2_ref, g2_ref, kk_ref, ka_ref, rk_ref,
                    e_ref, et_ref, lf_ref, lb_ref,
                    r_o, v_o, g_o, bonus_o,
                    kdf_o, bf_o, anf_o, cumf_o, kdb_o, bb_o, anb_o, cumb_o, *, dl, il):
    i = pl.program_id(1)
    n_t = pl.num_programs(1)
    w = r_o.shape[-1]
    zs = _shift_tile(zr_ref[0], zrp_ref[0], zrn_ref[0], mur_ref[...], i, n_t)
    zl = _shift_tile(zl_ref[0], zlp_ref[0], zln_ref[0], mul_ref[...], i, n_t)
    r = zs[:, 0:w]
    k = zs[:, w:2 * w]
    v = zs[:, 2 * w:3 * w]
    gd = zl[:, 2 * dl + 2 * il:]
    kkr = k * kk_ref[...]
    kk = kkr / jnp.maximum(jnp.sqrt(_seg_sum(kkr * kkr, e_ref, et_ref)), 1e-12)
    r_o[0] = r
    v_o[0] = v
    g_o[0] = _dot1(_sigmoid(gd), g2_ref[...])
    kd_sum = None
    outs = ((kdf_o, bf_o, anf_o, cumf_o, lf_ref), (kdb_o, bb_o, anb_o, cumb_o, lb_ref))
    for z, (kd_o, b_o, an_o, cum_o, l_ref) in enumerate(outs):
        wd = zl[:, z * dl:(z + 1) * dl]
        ad = zl[:, 2 * dl + z * il:2 * dl + (z + 1) * il]
        wl = w0_ref[z:z + 1, :] + _dot1(jnp.tanh(wd), w2_ref[z])
        u = -wl
        softplus = jnp.maximum(u, 0.0) + jnp.log(1.0 + jnp.exp(-jnp.abs(u)))
        lw = -jnp.exp(-softplus - 0.5)
        icl = _sigmoid(a0_ref[z:z + 1, :] + _dot1(ad, a2_ref[z]))
        kd = k * (1.0 + (icl - 1.0) * ka_ref[...])
        kd_o[0] = kd
        b_o[0] = kk * icl
        an_o[0] = -kk * jnp.exp(-lw)
        lh = lw.astype(BF16)
        rem = lw - lh.astype(F32)
        lm = rem.astype(BF16)
        ll = (rem - lm.astype(F32)).astype(BF16)
        lmat = l_ref[...]
        cum_o[0] = _dg(lmat, lh, _NN) + (_dg(lmat, lm, _NN) + _dg(lmat, ll, _NN))
        kd_sum = kd if kd_sum is None else kd_sum + kd
    bonus_o[0] = _seg_sum(r * kd_sum * rk_ref[...], e_ref, et_ref) * v


def _rw_prep(z_rkv, z_lo, mu_r, mu_l, w0, w2, a0, a2, g2, k_k, k_a, r_k):
    bsz, t, w3 = z_rkv.shape
    w = w3 // 3
    lw_cols = z_lo.shape[-1]
    dl, il = w2.shape[1], a2.shape[1]
    nh = w // RW_HEAD
    tt = _pick(t, 128, RW_CHUNK)
    n_t = t // tt
    hb = tt // SUBLANES
    head = jnp.arange(w, dtype=I32) // RW_HEAD
    e = (head[:, None] == jnp.arange(nh, dtype=I32)[None, :]).astype(BF16)
    et = e.T
    ti = jnp.arange(tt, dtype=I32)
    same = (ti[:, None] // RW_CHUNK) == (ti[None, :] // RW_CHUNK)
    lf = (same & (ti[None, :] <= ti[:, None])).astype(BF16)
    lb = (same & (ti[None, :] >= ti[:, None])).astype(BF16)

    def tile(c):
        return pl.BlockSpec((1, tt, c), lambda b, i: (b, i, 0))

    def prev(c):
        return pl.BlockSpec((1, SUBLANES, c), lambda b, i: (b, jnp.maximum(i * hb - 1, 0), 0))

    def nxt(c):
        return pl.BlockSpec((1, SUBLANES, c),
                            lambda b, i: (b, jnp.minimum((i + 1) * hb, t // SUBLANES - 1), 0))

    def full(a):
        nd = a.ndim
        return pl.BlockSpec(a.shape, lambda b, i: (0,) * nd)

    consts = [mu_r.reshape(1, w3), mu_l.reshape(1, lw_cols), w0, w2, a0, a2, g2,
              k_k.reshape(1, w), k_a.reshape(1, w), r_k.reshape(1, w), e, et, lf, lb]
    out = jax.ShapeDtypeStruct((bsz, t, w), F32)
    return pl.pallas_call(
        functools.partial(_rw_prep_kernel, dl=dl, il=il),
        grid=(bsz, n_t),
        in_specs=[tile(w3), prev(w3), nxt(w3), tile(lw_cols), prev(lw_cols), nxt(lw_cols)]
        + [full(a) for a in consts],
        out_specs=[tile(w)] * 12,
        out_shape=[out] * 12,
        compiler_params=_cparams("arbitrary", "arbitrary"),
        name="rw_prep",
    )(z_rkv, z_rkv, z_rkv, z_lo, z_lo, z_lo, *consts)


def _scan_kernel(r_ref, v_ref, kd_ref, b_ref, an_ref, cum_ref, y_ref, h_ref, *, reverse, n_pairs):
    c = RW_CHUNK
    n = RW_HEAD

    @pl.when(pl.program_id(1) == 0)
    def _():
        h_ref[...] = jnp.zeros_like(h_ref)

    row = lax.broadcasted_iota(I32, (c, c), 0)
    col = lax.broadcasted_iota(I32, (c, c), 1)
    if reverse:
        strict, incl, last = col > row, col >= row, 0
    else:
        strict, incl, last = col < row, col <= row, c - 1
    eye = row == col
    eye_f = jnp.where(eye, 1.0, 0.0)

    def pair_body(p, carry):
        sl = pl.ds(pl.multiple_of(p * (2 * n), 2 * n), 2 * n)
        cum = cum_ref[0, :, sl]
        tot = cum[last:last + 1, :]
        e1 = jnp.exp(cum)
        e2 = jnp.exp(-cum)
        et = jnp.exp(tot)
        rt = r_ref[0, :, sl] * e1
        at = an_ref[0, :, sl] * e1
        bt = b_ref[0, :, sl] * e2
        kt = kd_ref[0, :, sl] * e2
        bh = bt * et
        kh = kt * et
        v2 = v_ref[0, :, sl]
        hs = h_ref[:, sl]
        ys, hn = [], []
        for j in range(2):
            s = slice(n * j, n * (j + 1))
            g = _dot3(jnp.concatenate([at[:, s], rt[:, s]], axis=0),
                      jnp.concatenate([bt[:, s], kt[:, s]], axis=0), _NT)
            nab = jnp.where(strict, g[:c, :c], 0.0)
            aak = jnp.where(strict, g[:c, c:], 0.0)
            mb = jnp.where(incl, g[c:, :c], 0.0)
            mk = jnp.where(incl, g[c:, c:], 0.0)
            vj = v2[:, s]
            tm = eye_f + nab
            npow = nab
            for _ in range(5):
                npow = _dot3(npow, npow)
                tm = tm + _dot3(npow, tm)
            x = jnp.concatenate([at[:, s], _dot3(aak, vj)], axis=1)
            zz = _dot3(tm, x)
            rhs2 = jnp.concatenate(
                [zz, jnp.concatenate([jnp.zeros((c, n), F32), vj], axis=1)], axis=0)
            o1 = _dot3(jnp.concatenate([mb, mk], axis=1), rhs2)
            o2 = _dot3(jnp.concatenate([bh[:, s], kh[:, s]], axis=0), rhs2, _TN)
            rp = rt[:, s] + o1[:, :n]
            pm = o2[:, :n] + jnp.where(eye, et[:, s], 0.0)
            o3 = _dot3(jnp.concatenate([rp, pm], axis=0), hs[:, s])
            ys.append(o3[:c] + o1[:, n:])
            hn.append(o3[c:] + o2[:, n:])
        y_ref[0, :, sl] = jnp.concatenate(ys, axis=1)
        h_ref[:, sl] = jnp.concatenate(hn, axis=1)
        return carry

    lax.fori_loop(0, n_pairs, pair_body, 0)


def _rw_scan(r, v, kd, b, an, cum, reverse):
    bsz, t, w = r.shape
    nc = t // RW_CHUNK
    if reverse:
        imap = lambda bb, ci: (bb, nc - 1 - ci, 0)
    else:
        imap = lambda bb, ci: (bb, ci, 0)
    spec = pl.BlockSpec((1, RW_CHUNK, w), imap)
    return pl.pallas_call(
        functools.partial(_scan_kernel, reverse=reverse, n_pairs=w // (2 * RW_HEAD)),
        grid=(bsz, nc),
        in_specs=[spec] * 6,
        out_specs=spec,
        out_shape=jax.ShapeDtypeStruct((bsz, t, w), F32),
        scratch_shapes=[pltpu.VMEM((RW_HEAD, w), F32)],
        compiler_params=_cparams("arbitrary", "arbitrary"),
        name="rw_scan_bwd" if reverse else "rw_scan_fwd",
    )(r, v, kd, b, an, cum)


def _merge_kernel(yf_ref, yb_ref, bonus_ref, g_ref, att_ref, ga_ref, gb_ref, lw_ref, lb_ref,
                  e_ref, et_ref, o_ref):
    y = yf_ref[0] + yb_ref[0]
    mean = _seg_sum(y, e_ref, et_ref) * (1.0 / RW_HEAD)
    d = y - mean
    var = _seg_sum(d * d, e_ref, et_ref) * (1.0 / RW_HEAD)
    yn = d * lax.rsqrt(var + GN_EPS) * lw_ref[...] + lb_ref[...]
    ya = (yn + bonus_ref[0]) * g_ref[0]
    o_ref[0] = (_sigmoid(ga_ref[0]) * ya + _sigmoid(gb_ref[0]) * att_ref[0]).astype(o_ref.dtype)


def _merge(yf, yb, bonus, g, att, z_gates, lnx_w, lnx_b):
    bsz, t, w = yf.shape
    nh = w // RW_HEAD
    tt = _pick(t, 256, SUBLANES)
    head = jnp.arange(w, dtype=I32) // RW_HEAD
    e = (head[:, None] == jnp.arange(nh, dtype=I32)[None, :]).astype(BF16)
    et = e.T
    tile = pl.BlockSpec((1, tt, w), lambda b, i: (b, i, 0))
    gate_b = pl.BlockSpec((1, tt, w), lambda b, i: (b, i, 1))
    vec = pl.BlockSpec((1, w), lambda b, i: (0, 0))
    return pl.pallas_call(
        _merge_kernel,
        grid=(bsz, t // tt),
        in_specs=[tile, tile, tile, tile, tile, tile, gate_b, vec, vec,
                  pl.BlockSpec(e.shape, lambda b, i: (0, 0)),
                  pl.BlockSpec(et.shape, lambda b, i: (0, 0))],
        out_specs=tile,
        out_shape=jax.ShapeDtypeStruct((bsz, t, w), BF16),
        compiler_params=_cparams("arbitrary", "arbitrary"),
        name="merge",
    )(yf, yb, bonus, g, att, z_gates, z_gates, lnx_w.reshape(1, w), lnx_b.reshape(1, w), e, et)


def _attn_kernel(slope_ref, lam_ref, q_ref, k_ref, v_ref, sg_ref, o_ref, m_ref, l_ref, acc_ref,
                 *, out_scale):
    h = pl.program_id(1)
    qi = pl.program_id(2)
    kj = pl.program_id(3)
    tq = q_ref.shape[1]
    tk = k_ref.shape[1]

    @pl.when(kj == 0)
    def _():
        m_ref[...] = jnp.full_like(m_ref, -jnp.inf)
        l_ref[...] = jnp.zeros_like(l_ref)
        acc_ref[...] = jnp.zeros_like(acc_ref)

    q = (q_ref[0] * (DA_QK ** -0.5)).astype(BF16)
    k = k_ref[0].astype(BF16)
    v = v_ref[0].astype(BF16)
    rowpos = qi * tq + lax.broadcasted_iota(I32, (tq, tk), 0)
    colpos = kj * tk + lax.broadcasted_iota(I32, (tq, tk), 1)
    bias = -slope_ref[h] * jnp.abs(rowpos - colpos).astype(F32)
    for c in range(2):
        cs = slice(c * DA_QK, (c + 1) * DA_QK)
        s = _dg(q[:, cs], k[:, cs], _NT) + bias
        m_prev = m_ref[c]
        m_new = jnp.maximum(m_prev, jnp.max(s, axis=1, keepdims=True))
        alpha = jnp.exp(m_prev - m_new)
        p = jnp.exp(s - m_new)
        l_ref[c] = alpha * l_ref[c] + jnp.sum(p, axis=1, keepdims=True)
        acc_ref[c] = alpha * acc_ref[c] + _dg(p.astype(BF16), v, _NN)
        m_ref[c] = m_new

    @pl.when(kj == pl.num_programs(3) - 1)
    def _():
        o = acc_ref[0] / l_ref[0] - lam_ref[0] * (acc_ref[1] / l_ref[1])
        o = o * lax.rsqrt(jnp.mean(o * o, axis=-1, keepdims=True) + EPS) * sg_ref[...]
        o_ref[0] = o * out_scale


def _diff_attention(z_qkv, lam, sub_g, lambda_init):
    bsz, t, w3 = z_qkv.shape
    nh = w3 // 3 // DA_V
    tq = _pick(t, 512, LANES)
    tk = _pick(t, 512, LANES)
    slopes = 2.0 ** (-8.0 * jnp.arange(1, nh + 1, dtype=F32) / nh)
    smem = pl.BlockSpec(memory_space=pltpu.SMEM)
    return pl.pallas_call(
        functools.partial(_attn_kernel, out_scale=1.0 - lambda_init),
        grid=(bsz, nh, t // tq, t // tk),
        in_specs=[smem, smem,
                  pl.BlockSpec((1, tq, DA_V), lambda b, h, i, j: (b, i, h)),
                  pl.BlockSpec((1, tk, DA_V), lambda b, h, i, j: (b, j, nh + h)),
                  pl.BlockSpec((1, tk, DA_V), lambda b, h, i, j: (b, j, 2 * nh + h)),
                  pl.BlockSpec((1, DA_V), lambda b, h, i, j: (0, 0))],
        out_specs=pl.BlockSpec((1, tq, DA_V), lambda b, h, i, j: (b, i, h)),
        out_shape=jax.ShapeDtypeStruct((bsz, t, nh * DA_V), F32),
        scratch_shapes=[pltpu.VMEM((2, tq, 1), F32), pltpu.VMEM((2, tq, 1), F32),
                        pltpu.VMEM((2, tq, DA_V), F32)],
        compiler_params=_cparams("arbitrary", "arbitrary", "arbitrary", "arbitrary"),
        name="diff_attention",
    )(slopes, lam, z_qkv, z_qkv, z_qkv, sub_g.reshape(1, DA_V))


def _router_kernel(x_ref, g_ref, sc_ref, sh_ref, wr_ref, h_ref, aff_ref):
    x = x_ref[0]
    y = x * lax.rsqrt(jnp.mean(x * x, axis=-1, keepdims=True) + EPS) * g_ref[...]
    h = y * (1.0 + sc_ref[0]) + sh_ref[0]
    h_ref[0] = h.astype(h_ref.dtype)
    logits = _dot3(wr_ref[...], h, _NT)
    m = jnp.max(logits, axis=0, keepdims=True)
    ex = jnp.exp(logits - m)
    aff = ex / jnp.sum(ex, axis=0, keepdims=True)
    for j in range(aff_ref.shape[1]):
        aff_ref[:, j, :] = aff[:, j * LANES:(j + 1) * LANES]


def _router(x, g, scale, shift, w_router_t):
    bsz, t, d = x.shape
    ne = w_router_t.shape[0]
    tt = _pick(t, SUBLANES * LANES, SUBLANES * LANES)
    nb = tt // LANES
    n_t = t // tt
    row = pl.BlockSpec((1, 1, d), lambda b, i: (b, 0, 0))
    return pl.pallas_call(
        _router_kernel,
        grid=(bsz, n_t),
        in_specs=[pl.BlockSpec((1, tt, d), lambda b, i: (b, i, 0)),
                  pl.BlockSpec((1, d), lambda b, i: (0, 0)), row, row,
                  pl.BlockSpec((ne, d), lambda b, i: (0, 0))],
        out_specs=[pl.BlockSpec((1, tt, d), lambda b, i: (b, i, 0)),
                   pl.BlockSpec((ne, nb, LANES), lambda b, i: (0, b * n_t + i, 0))],
        out_shape=[jax.ShapeDtypeStruct((bsz, t, d), BF16),
                   jax.ShapeDtypeStruct((ne, bsz * t // LANES, LANES), F32)],
        compiler_params=_cparams("arbitrary", "arbitrary"),
        name="router",
    )(x, g.reshape(1, d), scale, shift, w_router_t)


def _sum_all(x):
    return jnp.sum(jnp.sum(x, axis=1, keepdims=True), axis=0, keepdims=True)


def _select_kernel(aff_ref, pos_ref, *, cap):
    nb = aff_ref.shape[1]
    li = lax.broadcasted_iota(I32, (LANES, LANES), 0)
    lj = lax.broadcasted_iota(I32, (LANES, LANES), 1)
    upper = jnp.where(li <= lj, 1.0, 0.0).astype(BF16)
    ri = lax.broadcasted_iota(I32, (nb, nb), 0)
    rj = lax.broadcasted_iota(I32, (nb, nb), 1)
    lower = jnp.where(rj < ri, 1.0, 0.0).astype(BF16)

    def excl_prefix(mask):
        x = jnp.where(mask, 1.0, 0.0)
        inc = _dg(x.astype(BF16), upper, _NN)
        tot = jnp.broadcast_to(inc[:, LANES - 1:LANES], (nb, LANES))
        offs = _dg(lower, tot.astype(BF16), _NN)
        return offs + inc - x

    def expert_body(e, carry):
        bits = pltpu.bitcast(aff_ref[e], I32)

        def bit_body(i, thr):
            cand = thr | jnp.left_shift(jnp.int32(1), 30 - i)
            cnt = _sum_all(jnp.where(bits >= cand, 1.0, 0.0))
            return jnp.where(cnt >= cap, cand, thr)

        thr = lax.fori_loop(0, 31, bit_body, jnp.zeros((1, 1), I32))
        gt = bits > thr
        eq = bits == thr
        need = cap - _sum_all(jnp.where(gt, 1.0, 0.0))
        sel = gt | (eq & (excl_prefix(eq) < need))
        pos_ref[e] = jnp.where(sel, excl_prefix(sel), -1.0).astype(I32)
        return carry

    lax.fori_loop(0, aff_ref.shape[0], expert_body, 0)


def _select(aff, cap):
    ne, nb, _ = aff.shape
    return pl.pallas_call(
        functools.partial(_select_kernel, cap=cap),
        grid=(1,),
        in_specs=[pl.BlockSpec(aff.shape, lambda i: (0, 0, 0))],
        out_specs=pl.BlockSpec(aff.shape, lambda i: (0, 0, 0)),
        out_shape=jax.ShapeDtypeStruct(aff.shape, I32),
        compiler_params=_cparams("arbitrary"),
        name="select",
    )(aff)


def _one_hot(pos_row, cap):
    slots = lax.broadcasted_iota(I32, (cap, LANES), 0)
    return slots == pos_row


def _gather_kernel(pos_ref, aff_ref, h_ref, xe_ref, gate_ref, acc_ref, gacc_ref):
    kt = pl.program_id(1)
    cap = acc_ref.shape[0]

    @pl.when(kt == 0)
    def _():
        acc_ref[...] = jnp.zeros_like(acc_ref)
        gacc_ref[...] = jnp.zeros_like(gacc_ref)

    for gi in range(pos_ref.shape[1]):
        sel = _one_hot(pos_ref[0, gi:gi + 1, :], cap)
        acc_ref[...] += _dg(jnp.where(sel, 1.0, 0.0).astype(BF16),
                            h_ref[gi * LANES:(gi + 1) * LANES, :], _NN)
        gacc_ref[...] += jnp.sum(jnp.where(sel, aff_ref[0, gi:gi + 1, :], 0.0), axis=1, keepdims=True)

    @pl.when(kt == pl.num_programs(1) - 1)
    def _():
        xe_ref[0] = acc_ref[...].astype(xe_ref.dtype)
        gate_ref[0] = gacc_ref[...]


def _gather(pos, aff, h2, cap):
    ne, nb, _ = pos.shape
    n_tok, d = h2.shape
    gb = _pick(nb, 8, SUBLANES)
    return pl.pallas_call(
        _gather_kernel,
        grid=(ne, nb // gb),
        in_specs=[pl.BlockSpec((1, gb, LANES), lambda e, kt: (e, kt, 0)),
                  pl.BlockSpec((1, gb, LANES), lambda e, kt: (e, kt, 0)),
                  pl.BlockSpec((gb * LANES, d), lambda e, kt: (kt, 0))],
        out_specs=[pl.BlockSpec((1, cap, d), lambda e, kt: (e, 0, 0)),
                   pl.BlockSpec((1, cap, 1), lambda e, kt: (e, 0, 0))],
        out_shape=[jax.ShapeDtypeStruct((ne, cap, d), BF16),
                   jax.ShapeDtypeStruct((ne, cap, 1), F32)],
        scratch_shapes=[pltpu.VMEM((cap, d), F32), pltpu.VMEM((cap, 1), F32)],
        compiler_params=_cparams("arbitrary", "arbitrary"),
        name="moe_gather",
    )(pos, aff, h2)


def _ffn_kernel(xe_ref, wg_ref, wu_ref, wd_ref, gate_ref, ye_ref, acc_ref):
    f = pl.program_id(1)

    @pl.when(f == 0)
    def _():
        acc_ref[...] = jnp.zeros_like(acc_ref)

    xe = xe_ref[0]
    hid = _silu(_dot1(xe, wg_ref[0])) * _dot1(xe, wu_ref[0])
    acc_ref[...] += _dot1(hid, wd_ref[0])

    @pl.when(f == pl.num_programs(1) - 1)
    def _():
        ye_ref[0] = (acc_ref[...] * gate_ref[0]).astype(ye_ref.dtype)


def _expert_ffn(xe, gate, w_gate, w_up, w_down):
    ne, cap, d = xe.shape
    ff = w_gate.shape[-1]
    tf = _pick(ff, 256, LANES)
    return pl.pallas_call(
        _ffn_kernel,
        grid=(ne, ff // tf),
        in_specs=[pl.BlockSpec((1, cap, d), lambda e, f: (e, 0, 0)),
                  pl.BlockSpec((1, d, tf), lambda e, f: (e, 0, f)),
                  pl.BlockSpec((1, d, tf), lambda e, f: (e, 0, f)),
                  pl.BlockSpec((1, tf, d), lambda e, f: (e, f, 0)),
                  pl.BlockSpec((1, cap, 1), lambda e, f: (e, 0, 0))],
        out_specs=pl.BlockSpec((1, cap, d), lambda e, f: (e, 0, 0)),
        out_shape=jax.ShapeDtypeStruct((ne, cap, d), BF16),
        scratch_shapes=[pltpu.VMEM((cap, d), F32)],
        compiler_params=_cparams("arbitrary", "arbitrary"),
        name="expert_ffn",
    )(xe, w_gate, w_up, w_down, gate)


def _scatter_kernel(pos_ref, ye_ref, x_ref, g_ref, fg_ref, o_ref, acc_ref, *, final_norm):
    e = pl.program_id(2)
    cap = ye_ref.shape[1]

    @pl.when(e == 0)
    def _():
        acc_ref[...] = jnp.zeros_like(acc_ref)

    ye = ye_ref[0]
    gb = acc_ref.shape[0] // LANES
    row0 = (pl.program_id(0) * pl.num_programs(1) + pl.program_id(1)) * gb
    for gi in range(gb):
        sel = jnp.where(_one_hot(pos_ref[0, pl.ds(row0 + gi, 1), :], cap), 1.0, 0.0).astype(BF16)
        acc_ref[gi * LANES:(gi + 1) * LANES, :] += _dg(sel, ye, _TN)

    @pl.when(e == pl.num_programs(2) - 1)
    def _():
        x = x_ref[0] + g_ref[0] * acc_ref[...]
        if final_norm:
            x = x * lax.rsqrt(jnp.mean(x * x, axis=-1, keepdims=True) + EPS) * fg_ref[...]
        o_ref[0] = x


def _scatter_residual(pos, ye, x, gate2, final_g, final_norm):
    ne, nb, _ = pos.shape
    cap, d = ye.shape[1], ye.shape[2]
    bsz, t, _ = x.shape
    tt = _pick(t, 512, LANES)
    n_t = t // tt
    return pl.pallas_call(
        functools.partial(_scatter_kernel, final_norm=final_norm),
        grid=(bsz, n_t, ne),
        in_specs=[pl.BlockSpec((1, nb, LANES), lambda b, i, e: (e, 0, 0)),
                  pl.BlockSpec((1, cap, d), lambda b, i, e: (e, 0, 0)),
                  pl.BlockSpec((1, tt, d), lambda b, i, e: (b, i, 0)),
                  pl.BlockSpec((1, 1, d), lambda b, i, e: (b, 0, 0)),
                  pl.BlockSpec((1, d), lambda b, i, e: (0, 0))],
        out_specs=pl.BlockSpec((1, tt, d), lambda b, i, e: (b, i, 0)),
        out_shape=jax.ShapeDtypeStruct((bsz, t, d), F32),
        scratch_shapes=[pltpu.VMEM((tt, d), F32)],
        compiler_params=_cparams("arbitrary", "arbitrary", "arbitrary"),
        name="moe_scatter",
    )(pos, ye, x, gate2, final_g.reshape(1, d))


def _encoder_layer(x, ada, lw, lambda_init, final_g, final_norm):
    bsz, t, d = x.shape
    shift1, scale1, gate1, shift2, scale2, gate2 = [a[:, None, :] for a in jnp.split(ada, 6, axis=-1)]

    h = _norm_mod(x, lw["norm1_g"], scale1, shift1).reshape(bsz * t, d)
    z_rkv = _matmul(h, lw["w_rkv"]).reshape(bsz, t, -1)
    z_lo = _matmul(h, lw["w_lo"]).reshape(bsz, t, -1)
    z_qkv = _matmul(h, lw["w_qkv"]).reshape(bsz, t, -1)
    z_gates = _matmul(h, lw["w_gates"]).reshape(bsz, t, -1)

    (r, v, g, bonus, kd_f, b_f, an_f, cum_f, kd_b, b_b, an_b, cum_b) = _rw_prep(
        z_rkv, z_lo, lw["mu_r"], lw["mu_l"], lw["w0"], lw["w2"], lw["a0"], lw["a2"], lw["g2"],
        lw["k_k"], lw["k_a"], lw["r_k"])
    y_f = _rw_scan(r, v, kd_f, b_f, an_f, cum_f, reverse=False)
    y_b = _rw_scan(r, v, kd_b, b_b, an_b, cum_b, reverse=True)

    att = _diff_attention(z_qkv, lw["lam"], lw["sub_g"], lambda_init)
    merged = _merge(y_f, y_b, bonus, g, att, z_gates, lw["lnx_w"], lw["lnx_b"])
    x1 = _matmul_residual(merged, lw["w_out"], x, gate1)

    n_tok = bsz * t
    ne = lw["w_router_t"].shape[0]
    cap = max(1, CAPACITY_FACTOR * n_tok // ne)
    h2, aff = _router(x1, lw["norm2_g"], scale2, shift2, lw["w_router_t"])
    pos = _select(aff, cap)
    xe, gate = _gather(pos, aff, h2.reshape(n_tok, d), cap)
    ye = _expert_ffn(xe, gate, lw["w_gate"], lw["w_up"], lw["w_down"])
    return _scatter_residual(pos, ye, x1, gate2, final_g, final_norm)


def kernel(x_prompt, x_sample, c_prompt, c_sample, w_ada, b_ada, norm1_g, w_in, mu_shift, w0, w2, a0, a2, g2, k_k, k_a, r_k, lnx_w, lnx_b, lam_q1, lam_k1, lam_q2, lam_k2, sub_g, w_out, norm2_g, w_router, w_gate, w_up, w_down, final_g):
    depth = w_ada.shape[0]
    d = x_prompt.shape[-1]
    w3 = 3 * d
    n_lo = 2 * w2.shape[2] + 2 * a2.shape[2] + g2.shape[1]
    xs = [x_prompt, x_sample]
    n_rows = [x_prompt.shape[0], x_sample.shape[0]]
    c_all = jnp.concatenate([c_prompt, c_sample], axis=0)
    pad = (-c_all.shape[0]) % SUBLANES
    c_all = jnp.pad(c_all, ((0, pad), (0, 0)))
    for l in range(depth):
        lambda_init = 0.8 - 0.6 * math.exp(-0.3 * l)
        w_in_l = w_in[l].astype(BF16)
        lam = (jnp.exp(jnp.sum(lam_q1[l] * lam_k1[l])) - jnp.exp(jnp.sum(lam_q2[l] * lam_k2[l]))
               + lambda_init).astype(F32).reshape(1)
        lw = dict(
            norm1_g=norm1_g[l], norm2_g=norm2_g[l],
            w_rkv=w_in_l[:, :w3], w_lo=w_in_l[:, w3:w3 + n_lo],
            w_qkv=w_in_l[:, w3 + n_lo:w3 + n_lo + 3 * d], w_gates=w_in_l[:, w3 + n_lo + 3 * d:],
            mu_r=mu_shift[l, :w3], mu_l=mu_shift[l, w3:w3 + n_lo],
            w0=w0[l], w2=w2[l], a0=a0[l], a2=a2[l], g2=g2[l], k_k=k_k[l], k_a=k_a[l], r_k=r_k[l],
            lnx_w=lnx_w[l], lnx_b=lnx_b[l], lam=lam, sub_g=sub_g[l],
            w_out=w_out[l].astype(BF16), w_router_t=w_router[l].T,
            w_gate=w_gate[l], w_up=w_up[l], w_down=w_down[l])
        ada = _ada(c_all, w_ada[l], b_ada[l])
        off = 0
        for gi in range(2):
            xs[gi] = _encoder_layer(xs[gi], ada[off:off + n_rows[gi]], lw, lambda_init, final_g,
                                    final_norm=(l == depth - 1))
            off += n_rows[gi]
    return tuple(xs)
```

```python
import functools
import math

import jax
import jax.numpy as jnp
from jax import lax
from jax.experimental import pallas as pl
from jax.experimental.pallas import tpu as pltpu

F32 = jnp.float32
BF16 = jnp.bfloat16
I32 = jnp.int32

RW_HEAD = 64
DA_QK = 64
DA_V = 2 * DA_QK
RW_CHUNK = 64
CAPACITY_FACTOR = 2
EPS = 1e-6
GN_EPS = RW_HEAD * 1e-5
LANES = 128
SUBLANES = 8
VMEM_LIMIT = 56 * 1024 * 1024


def _cparams(*sem):
    return pltpu.CompilerParams(dimension_semantics=sem, vmem_limit_bytes=VMEM_LIMIT)


def _pick(n, pref, align):
    if n <= pref:
        return n
    t = (pref // align) * align
    while t >= align:
        if n % t == 0:
            return t
        t -= align
    return n


_NN = (((1,), (0,)), ((), ()))
_NT = (((1,), (1,)), ((), ()))
_TN = (((0,), (0,)), ((), ()))


def _dg(a, b, dims):
    return lax.dot_general(a, b, dims, preferred_element_type=F32)


def _dot1(a, b, dims=_NN):
    return _dg(a.astype(BF16), b.astype(BF16), dims)


def _split(x):
    hi = x.astype(BF16)
    lo = (x - hi.astype(F32)).astype(BF16)
    return hi, lo


def _dot3(a, b, dims=_NN):
    ah, al = _split(a)
    bh, bl = _split(b)
    return _dg(ah, bh, dims) + (_dg(ah, bl, dims) + _dg(al, bh, dims))


def _dot_exact_rhs(a, b_bf16, dims=_NN):
    ah, al = _split(a)
    return _dg(ah, b_bf16, dims) + _dg(al, b_bf16, dims)


def _sigmoid(x):
    return 1.0 / (1.0 + jnp.exp(-x))


def _silu(x):
    return x * _sigmoid(x)


def _ada_kernel(c_ref, w_ref, b_ref, o_ref):
    o_ref[...] = _dot1(_silu(c_ref[...]), w_ref[...]) + b_ref[...]


def _ada(c, w, b):
    m, d = c.shape
    n = w.shape[1]
    tn = _pick(n, 1024, LANES)
    return pl.pallas_call(
        _ada_kernel,
        grid=(n // tn,),
        in_specs=[pl.BlockSpec((m, d), lambda j: (0, 0)),
                  pl.BlockSpec((d, tn), lambda j: (0, j)),
                  pl.BlockSpec((1, tn), lambda j: (0, j))],
        out_specs=pl.BlockSpec((m, tn), lambda j: (0, j)),
        out_shape=jax.ShapeDtypeStruct((m, n), F32),
        compiler_params=_cparams("arbitrary"),
        name="ada",
    )(c, w, b.reshape(1, n))


def _norm_mod_kernel(x_ref, g_ref, sc_ref, sh_ref, o_ref):
    x = x_ref[0]
    y = x * lax.rsqrt(jnp.mean(x * x, axis=-1, keepdims=True) + EPS) * g_ref[...]
    o_ref[0] = (y * (1.0 + sc_ref[0]) + sh_ref[0]).astype(o_ref.dtype)


def _norm_mod(x, g, scale, shift):
    bsz, t, d = x.shape
    tt = _pick(t, 512, SUBLANES)
    row = pl.BlockSpec((1, 1, d), lambda b, i: (b, 0, 0))
    return pl.pallas_call(
        _norm_mod_kernel,
        grid=(bsz, t // tt),
        in_specs=[pl.BlockSpec((1, tt, d), lambda b, i: (b, i, 0)),
                  pl.BlockSpec((1, d), lambda b, i: (0, 0)), row, row],
        out_specs=pl.BlockSpec((1, tt, d), lambda b, i: (b, i, 0)),
        out_shape=jax.ShapeDtypeStruct((bsz, t, d), BF16),
        compiler_params=_cparams("arbitrary", "arbitrary"),
        name="norm_mod",
    )(x, g.reshape(1, d), scale, shift)


def _mm_kernel(a_ref, b_ref, o_ref):
    o_ref[...] = _dot1(a_ref[...], b_ref[...]).astype(o_ref.dtype)


def _matmul(a, b, out_dtype=F32, tm_pref=1024, tn_pref=1024):
    m, k = a.shape
    n = b.shape[1]
    tm = _pick(m, tm_pref, SUBLANES)
    tn = _pick(n, tn_pref, LANES)
    return pl.pallas_call(
        _mm_kernel,
        grid=(m // tm, n // tn),
        in_specs=[pl.BlockSpec((tm, k), lambda i, j: (i, 0)),
                  pl.BlockSpec((k, tn), lambda i, j: (0, j))],
        out_specs=pl.BlockSpec((tm, tn), lambda i, j: (i, j)),
        out_shape=jax.ShapeDtypeStruct((m, n), out_dtype),
        compiler_params=_cparams("arbitrary", "arbitrary"),
        name="matmul",
    )(a, b)


def _mm_res_kernel(a_ref, b_ref, x_ref, g_ref, o_ref):
    o_ref[0] = x_ref[0] + g_ref[0] * _dot1(a_ref[0], b_ref[...])


def _matmul_residual(a, b, x, gate):
    bsz, t, k = a.shape
    n = b.shape[1]
    tm = _pick(t, 1024, SUBLANES)
    tn = _pick(n, 1024, LANES)
    return pl.pallas_call(
        _mm_res_kernel,
        grid=(bsz, t // tm, n // tn),
        in_specs=[pl.BlockSpec((1, tm, k), lambda bb, i, j: (bb, i, 0)),
                  pl.BlockSpec((k, tn), lambda bb, i, j: (0, j)),
                  pl.BlockSpec((1, tm, tn), lambda bb, i, j: (bb, i, j)),
                  pl.BlockSpec((1, 1, tn), lambda bb, i, j: (bb, 0, j))],
        out_specs=pl.BlockSpec((1, tm, tn), lambda bb, i, j: (bb, i, j)),
        out_shape=jax.ShapeDtypeStruct((bsz, t, n), F32),
        compiler_params=_cparams("arbitrary", "arbitrary", "arbitrary"),
        name="matmul_residual",
    )(a, b, x, gate)


def _shift_tile(z, prev8, next8, mu, i, n_t):
    tt = z.shape[0]
    pr = jnp.where(i > 0, prev8[SUBLANES - 1:SUBLANES, :], 0.0)
    nx = jnp.where(i < n_t - 1, next8[0:1, :], 0.0)
    ridx = lax.broadcasted_iota(I32, z.shape, 0)
    zp = jnp.where(ridx == 0, pr, pltpu.roll(z, 1, 0))
    zn = jnp.where(ridx == tt - 1, nx, pltpu.roll(z, tt - 1, 0))
    return z + mu * (0.5 * (zp + zn) - z)


def _seg_sum(x, e_ref, et_ref):
    s = _dot_exact_rhs(x, e_ref[...])
    return _dot_exact_rhs(s, et_ref[...])


def _rw_prep_kernel(zr_ref, zrp_ref, zrn_ref, zl_ref, zlp_ref, zln_ref, mur_ref, mul_ref,
                    w0_ref, w2_ref, a0_ref, a2_ref, g2_ref, kk_ref, ka_ref, rk_ref,
                    e_ref, et_ref, lf_ref, lb_ref,
                    r_o, v_o, g_o, bonus_o,
                    kdf_o, bf_o, anf_o, cumf_o, kdb_o, bb_o, anb_o, cumb_o, *, dl, il):
    i = pl.program_id(1)
    n_t = pl.num_programs(1)
    w = r_o.shape[-1]
    zs = _shift_tile(zr_ref[0], zrp_ref[0], zrn_ref[0], mur_ref[...], i, n_t)
    zl = _shift_tile(zl_ref[0], zlp_ref[0], zln_ref[0], mul_ref[...], i, n_t)
    r = zs[:, 0:w]
    k = zs[:, w:2 * w]
    v = zs[:, 2 * w:3 * w]
    gd = zl[:, 2 * dl + 2 * il:]
    kkr = k * kk_ref[...]
    kk = kkr / jnp.maximum(jnp.sqrt(_seg_sum(kkr * kkr, e_ref, et_ref)), 1e-12)
    r_o[0] = r
    v_o[0] = v
    g_o[0] = _dot1(_sigmoid(gd), g2_ref[...])
    kd_sum = None
    outs = ((kdf_o, bf_o, anf_o, cumf_o, lf_ref), (kdb_o, bb_o, anb_o, cumb_o, lb_ref))
    for z, (kd_o, b_o, an_o, cum_o, l_ref) in enumerate(outs):
        wd = zl[:, z * dl:(z + 1) * dl]
        ad = zl[:, 2 * dl + z * il:2 * dl + (z + 1) * il]
        wl = w0_ref[z:z + 1, :] + _dot1(jnp.tanh(wd), w2_ref[z])
        u = -wl
        softplus = jnp.maximum(u, 0.0) + jnp.log(1.0 + jnp.exp(-jnp.abs(u)))
        lw = -jnp.exp(-softplus - 0.5)
        icl = _sigmoid(a0_ref[z:z + 1, :] + _dot1(ad, a2_ref[z]))
        kd = k * (1.0 + (icl - 1.0) * ka_ref[...])
        kd_o[0] = kd
        b_o[0] = kk * icl
        an_o[0] = -kk * jnp.exp(-lw)
        lh = lw.astype(BF16)
        rem = lw - lh.astype(F32)
        lm = rem.astype(BF16)
        ll = (rem - lm.astype(F32)).astype(BF16)
        lmat = l_ref[...]
        cum_o[0] = _dg(lmat, lh, _NN) + (_dg(lmat, lm, _NN) + _dg(lmat, ll, _NN))
        kd_sum = kd if kd_sum is None else kd_sum + kd
    bonus_o[0] = _seg_sum(r * kd_sum * rk_ref[...], e_ref, et_ref) * v


def _rw_prep(z_rkv, z_lo, mu_r, mu_l, w0, w2, a0, a2, g2, k_k, k_a, r_k):
    bsz, t, w3 = z_rkv.shape
    w = w3 // 3
    lw_cols = z_lo.shape[-1]
    dl, il = w2.shape[1], a2.shape[1]
    nh = w // RW_HEAD
    tt = _pick(t, 128, RW_CHUNK)
    n_t = t // tt
    hb = tt // SUBLANES
    head = jnp.arange(w, dtype=I32) // RW_HEAD
    e = (head[:, None] == jnp.arange(nh, dtype=I32)[None, :]).astype(BF16)
    et = e.T
    ti = jnp.arange(tt, dtype=I32)
    same = (ti[:, None] // RW_CHUNK) == (ti[None, :] // RW_CHUNK)
    lf = (same & (ti[None, :] <= ti[:, None])).astype(BF16)
    lb = (same & (ti[None, :] >= ti[:, None])).astype(BF16)

    def tile(c):
        return pl.BlockSpec((1, tt, c), lambda b, i: (b, i, 0))

    def prev(c):
        return pl.BlockSpec((1, SUBLANES, c), lambda b, i: (b, jnp.maximum(i * hb - 1, 0), 0))

    def nxt(c):
        return pl.BlockSpec((1, SUBLANES, c),
                            lambda b, i: (b, jnp.minimum((i + 1) * hb, t // SUBLANES - 1), 0))

    def full(a):
        nd = a.ndim
        return pl.BlockSpec(a.shape, lambda b, i: (0,) * nd)

    consts = [mu_r.reshape(1, w3), mu_l.reshape(1, lw_cols), w0, w2, a0, a2, g2,
              k_k.reshape(1, w), k_a.reshape(1, w), r_k.reshape(1, w), e, et, lf, lb]
    out = jax.ShapeDtypeStruct((bsz, t, w), F32)
    return pl.pallas_call(
        functools.partial(_rw_prep_kernel, dl=dl, il=il),
        grid=(bsz, n_t),
        in_specs=[tile(w3), prev(w3), nxt(w3), tile(lw_cols), prev(lw_cols), nxt(lw_cols)]
        + [full(a) for a in consts],
        out_specs=[tile(w)] * 12,
        out_shape=[out] * 12,
        compiler_params=_cparams("arbitrary", "arbitrary"),
        name="rw_prep",
    )(z_rkv, z_rkv, z_rkv, z_lo, z_lo, z_lo, *consts)


def _scan_kernel(r_ref, v_ref, kd_ref, b_ref, an_ref, cum_ref, y_ref, h_ref, *, reverse, n_groups,
                 group, prec):
    c = RW_CHUNK
    n = RW_HEAD

    @pl.when(pl.program_id(1) == 0)
    def _():
        h_ref[...] = jnp.zeros_like(h_ref)

    row = lax.broadcasted_iota(I32, (c, c), 0)
    col = lax.broadcasted_iota(I32, (c, c), 1)
    if reverse:
        strict, incl, last = col > row, col >= row, 0
    else:
        strict, incl, last = col < row, col <= row, c - 1
    eye = row == col
    eye_f = jnp.where(eye, 1.0, 0.0)

    dots = {1: _dot1, 3: _dot3}
    d_g, d_inv, d_av, d_z, d_o, d_st = (dots[p] for p in prec)
    hrange = range(group)

    def group_body(gi, carry):
        width = group * n
        sl = pl.ds(pl.multiple_of(gi * width, width), width)
        cum = cum_ref[0, :, sl]
        tot = cum[last:last + 1, :]
        e1 = jnp.exp(cum)
        e2 = jnp.exp(-cum)
        et = jnp.exp(tot)
        rt = r_ref[0, :, sl] * e1
        at = an_ref[0, :, sl] * e1
        bt = b_ref[0, :, sl] * e2
        kt = kd_ref[0, :, sl] * e2
        bh = bt * et
        kh = kt * et
        v2 = v_ref[0, :, sl]
        hs = h_ref[:, sl]
        hsl = [slice(n * j, n * (j + 1)) for j in hrange]
        g = [d_g(jnp.concatenate([at[:, s], rt[:, s]], axis=0),
                 jnp.concatenate([bt[:, s], kt[:, s]], axis=0), _NT) for s in hsl]
        nab = [jnp.where(strict, x[:c, :c], 0.0) for x in g]
        aak = [jnp.where(strict, x[:c, c:], 0.0) for x in g]
        mbk = [jnp.concatenate([jnp.where(incl, x[c:, :c], 0.0), jnp.where(incl, x[c:, c:], 0.0)],
                               axis=1) for x in g]
        vs = [v2[:, s] for s in hsl]
        av = [d_av(a, vv) for a, vv in zip(aak, vs)]
        tm = [eye_f + x for x in nab]
        npow = nab
        for _ in range(5):
            npow = [d_inv(x, x) for x in npow]
            tm = [t_ + d_inv(x, t_) for x, t_ in zip(npow, tm)]
        zz = [d_z(t_, jnp.concatenate([at[:, s], a], axis=1)) for t_, s, a in zip(tm, hsl, av)]
        rhs2 = [jnp.concatenate([z_, jnp.concatenate([jnp.zeros((c, n), F32), vv], axis=1)], axis=0)
                for z_, vv in zip(zz, vs)]
        o1 = [d_o(m, x) for m, x in zip(mbk, rhs2)]
        o2 = [d_o(jnp.concatenate([bh[:, s], kh[:, s]], axis=0), x, _TN)
              for s, x in zip(hsl, rhs2)]
        o3 = [d_st(jnp.concatenate([rt[:, s] + a[:, :n],
                                    b_[:, :n] + jnp.where(eye, et[:, s], 0.0)], axis=0), hs[:, s])
              for s, a, b_ in zip(hsl, o1, o2)]
        y_ref[0, :, sl] = jnp.concatenate([x[:c] + a[:, n:] for x, a in zip(o3, o1)], axis=1)
        h_ref[:, sl] = jnp.concatenate([x[c:] + b_[:, n:] for x, b_ in zip(o3, o2)], axis=1)
        return carry

    lax.fori_loop(0, n_groups, group_body, 0)


def _rw_scan(r, v, kd, b, an, cum, reverse, prec=(1, 1, 1, 1, 1, 1), group=8):
    bsz, t, w = r.shape
    nc = t // RW_CHUNK
    group = min(group, w // RW_HEAD)
    if reverse:
        imap = lambda bb, ci: (bb, nc - 1 - ci, 0)
    else:
        imap = lambda bb, ci: (bb, ci, 0)
    spec = pl.BlockSpec((1, RW_CHUNK, w), imap)
    return pl.pallas_call(
        functools.partial(_scan_kernel, reverse=reverse, n_groups=w // (group * RW_HEAD),
                          group=group, prec=prec),
        grid=(bsz, nc),
        in_specs=[spec] * 6,
        out_specs=spec,
        out_shape=jax.ShapeDtypeStruct((bsz, t, w), F32),
        scratch_shapes=[pltpu.VMEM((RW_HEAD, w), F32)],
        compiler_params=_cparams("arbitrary", "arbitrary"),
        name="rw_scan_bwd" if reverse else "rw_scan_fwd",
    )(r, v, kd, b, an, cum)


def _merge_kernel(yf_ref, yb_ref, bonus_ref, g_ref, att_ref, ga_ref, gb_ref, lw_ref, lb_ref,
                  e_ref, et_ref, o_ref):
    y = yf_ref[0] + yb_ref[0]
    mean = _seg_sum(y, e_ref, et_ref) * (1.0 / RW_HEAD)
    d = y - mean
    var = _seg_sum(d * d, e_ref, et_ref) * (1.0 / RW_HEAD)
    yn = d * lax.rsqrt(var + GN_EPS) * lw_ref[...] + lb_ref[...]
    ya = (yn + bonus_ref[0]) * g_ref[0]
    o_ref[0] = (_sigmoid(ga_ref[0]) * ya + _sigmoid(gb_ref[0]) * att_ref[0]).astype(o_ref.dtype)


def _merge(yf, yb, bonus, g, att, z_gates, lnx_w, lnx_b):
    bsz, t, w = yf.shape
    nh = w // RW_HEAD
    tt = _pick(t, 256, SUBLANES)
    head = jnp.arange(w, dtype=I32) // RW_HEAD
    e = (head[:, None] == jnp.arange(nh, dtype=I32)[None, :]).astype(BF16)
    et = e.T
    tile = pl.BlockSpec((1, tt, w), lambda b, i: (b, i, 0))
    gate_b = pl.BlockSpec((1, tt, w), lambda b, i: (b, i, 1))
    vec = pl.BlockSpec((1, w), lambda b, i: (0, 0))
    return pl.pallas_call(
        _merge_kernel,
        grid=(bsz, t // tt),
        in_specs=[tile, tile, tile, tile, tile, tile, gate_b, vec, vec,
                  pl.BlockSpec(e.shape, lambda b, i: (0, 0)),
                  pl.BlockSpec(et.shape, lambda b, i: (0, 0))],
        out_specs=tile,
        out_shape=jax.ShapeDtypeStruct((bsz, t, w), BF16),
        compiler_params=_cparams("arbitrary", "arbitrary"),
        name="merge",
    )(yf, yb, bonus, g, att, z_gates, z_gates, lnx_w.reshape(1, w), lnx_b.reshape(1, w), e, et)


def _attn_kernel(slope_ref, lam_ref, q_ref, k_ref, v_ref, sg_ref, o_ref, m_ref, l_ref, acc_ref,
                 *, out_scale):
    h = pl.program_id(1)
    qi = pl.program_id(2)
    kj = pl.program_id(3)
    tq = q_ref.shape[1]
    tk = k_ref.shape[1]

    @pl.when(kj == 0)
    def _():
        m_ref[...] = jnp.full_like(m_ref, -jnp.inf)
        l_ref[...] = jnp.zeros_like(l_ref)
        acc_ref[...] = jnp.zeros_like(acc_ref)

    q = (q_ref[0] * (DA_QK ** -0.5)).astype(BF16)
    k = k_ref[0].astype(BF16)
    v = v_ref[0].astype(BF16)
    rowpos = qi * tq + lax.broadcasted_iota(I32, (tq, tk), 0)
    colpos = kj * tk + lax.broadcasted_iota(I32, (tq, tk), 1)
    bias = -slope_ref[h] * jnp.abs(rowpos - colpos).astype(F32)
    for c in range(2):
        cs = slice(c * DA_QK, (c + 1) * DA_QK)
        s = _dg(q[:, cs], k[:, cs], _NT) + bias
        m_prev = m_ref[c]
        m_new = jnp.maximum(m_prev, jnp.max(s, axis=1, keepdims=True))
        alpha = jnp.exp(m_prev - m_new)
        p = jnp.exp(s - m_new)
        l_ref[c] = alpha * l_ref[c] + jnp.sum(p, axis=1, keepdims=True)
        acc_ref[c] = alpha * acc_ref[c] + _dg(p.astype(BF16), v, _NN)
        m_ref[c] = m_new

    @pl.when(kj == pl.num_programs(3) - 1)
    def _():
        o = acc_ref[0] / l_ref[0] - lam_ref[0] * (acc_ref[1] / l_ref[1])
        o = o * lax.rsqrt(jnp.mean(o * o, axis=-1, keepdims=True) + EPS) * sg_ref[...]
        o_ref[0] = o * out_scale


def _diff_attention(z_qkv, lam, sub_g, lambda_init):
    bsz, t, w3 = z_qkv.shape
    nh = w3 // 3 // DA_V
    tq = _pick(t, 512, LANES)
    tk = _pick(t, 512, LANES)
    slopes = 2.0 ** (-8.0 * jnp.arange(1, nh + 1, dtype=F32) / nh)
    smem = pl.BlockSpec(memory_space=pltpu.SMEM)
    return pl.pallas_call(
        functools.partial(_attn_kernel, out_scale=1.0 - lambda_init),
        grid=(bsz, nh, t // tq, t // tk),
        in_specs=[smem, smem,
                  pl.BlockSpec((1, tq, DA_V), lambda b, h, i, j: (b, i, h)),
                  pl.BlockSpec((1, tk, DA_V), lambda b, h, i, j: (b, j, nh + h)),
                  pl.BlockSpec((1, tk, DA_V), lambda b, h, i, j: (b, j, 2 * nh + h)),
                  pl.BlockSpec((1, DA_V), lambda b, h, i, j: (0, 0))],
        out_specs=pl.BlockSpec((1, tq, DA_V), lambda b, h, i, j: (b, i, h)),
        out_shape=jax.ShapeDtypeStruct((bsz, t, nh * DA_V), F32),
        scratch_shapes=[pltpu.VMEM((2, tq, 1), F32), pltpu.VMEM((2, tq, 1), F32),
                        pltpu.VMEM((2, tq, DA_V), F32)],
        compiler_params=_cparams("arbitrary", "arbitrary", "arbitrary", "arbitrary"),
        name="diff_attention",
    )(slopes, lam, z_qkv, z_qkv, z_qkv, sub_g.reshape(1, DA_V))


def _router_kernel(x_ref, g_ref, sc_ref, sh_ref, wr_ref, h_ref, aff_ref):
    x = x_ref[0]
    y = x * lax.rsqrt(jnp.mean(x * x, axis=-1, keepdims=True) + EPS) * g_ref[...]
    h = y * (1.0 + sc_ref[0]) + sh_ref[0]
    h_ref[0] = h.astype(h_ref.dtype)
    logits = _dot3(wr_ref[...], h, _NT)
    m = jnp.max(logits, axis=0, keepdims=True)
    ex = jnp.exp(logits - m)
    aff = ex / jnp.sum(ex, axis=0, keepdims=True)
    for j in range(aff_ref.shape[1]):
        aff_ref[:, j, :] = aff[:, j * LANES:(j + 1) * LANES]


def _router(x, g, scale, shift, w_router_t):
    bsz, t, d = x.shape
    ne = w_router_t.shape[0]
    tt = _pick(t, SUBLANES * LANES, SUBLANES * LANES)
    nb = tt // LANES
    n_t = t // tt
    row = pl.BlockSpec((1, 1, d), lambda b, i: (b, 0, 0))
    return pl.pallas_call(
        _router_kernel,
        grid=(bsz, n_t),
        in_specs=[pl.BlockSpec((1, tt, d), lambda b, i: (b, i, 0)),
                  pl.BlockSpec((1, d), lambda b, i: (0, 0)), row, row,
                  pl.BlockSpec((ne, d), lambda b, i: (0, 0))],
        out_specs=[pl.BlockSpec((1, tt, d), lambda b, i: (b, i, 0)),
                   pl.BlockSpec((ne, nb, LANES), lambda b, i: (0, b * n_t + i, 0))],
        out_shape=[jax.ShapeDtypeStruct((bsz, t, d), BF16),
                   jax.ShapeDtypeStruct((ne, bsz * t // LANES, LANES), F32)],
        compiler_params=_cparams("arbitrary", "arbitrary"),
        name="router",
    )(x, g.reshape(1, d), scale, shift, w_router_t)


def _sum_all(x):
    return jnp.sum(jnp.sum(x, axis=1, keepdims=True), axis=0, keepdims=True)


def _select_kernel(aff_ref, pos_ref, *, cap):
    nb = aff_ref.shape[1]
    li = lax.broadcasted_iota(I32, (LANES, LANES), 0)
    lj = lax.broadcasted_iota(I32, (LANES, LANES), 1)
    upper = jnp.where(li <= lj, 1.0, 0.0).astype(BF16)
    ri = lax.broadcasted_iota(I32, (nb, nb), 0)
    rj = lax.broadcasted_iota(I32, (nb, nb), 1)
    lower = jnp.where(rj < ri, 1.0, 0.0).astype(BF16)

    def excl_prefix(mask):
        x = jnp.where(mask, 1.0, 0.0)
        inc = _dg(x.astype(BF16), upper, _NN)
        tot = jnp.broadcast_to(inc[:, LANES - 1:LANES], (nb, LANES))
        offs = _dg(lower, tot.astype(BF16), _NN)
        return offs + inc - x

    def expert_body(e, carry):
        bits = pltpu.bitcast(aff_ref[e], I32)

        def bit_body(i, thr):
            cand = thr | jnp.left_shift(jnp.int32(1), 30 - i)
            cnt = _sum_all(jnp.where(bits >= cand, 1.0, 0.0))
            return jnp.where(cnt >= cap, cand, thr)

        thr = lax.fori_loop(0, 31, bit_body, jnp.zeros((1, 1), I32))
        gt = bits > thr
        eq = bits == thr
        need = cap - _sum_all(jnp.where(gt, 1.0, 0.0))
        sel = gt | (eq & (excl_prefix(eq) < need))
        pos_ref[e] = jnp.where(sel, excl_prefix(sel), -1.0).astype(I32)
        return carry

    lax.fori_loop(0, aff_ref.shape[0], expert_body, 0)


def _select(aff, cap):
    ne, nb, _ = aff.shape
    return pl.pallas_call(
        functools.partial(_select_kernel, cap=cap),
        grid=(1,),
        in_specs=[pl.BlockSpec(aff.shape, lambda i: (0, 0, 0))],
        out_specs=pl.BlockSpec(aff.shape, lambda i: (0, 0, 0)),
        out_shape=jax.ShapeDtypeStruct(aff.shape, I32),
        compiler_params=_cparams("arbitrary"),
        name="select",
    )(aff)


def _one_hot(pos_row, cap):
    slots = lax.broadcasted_iota(I32, (cap, LANES), 0)
    return slots == pos_row


def _gather_kernel(pos_ref, aff_ref, h_ref, xe_ref, gate_ref, acc_ref, gacc_ref):
    kt = pl.program_id(1)
    cap = acc_ref.shape[0]

    @pl.when(kt == 0)
    def _():
        acc_ref[...] = jnp.zeros_like(acc_ref)
        gacc_ref[...] = jnp.zeros_like(gacc_ref)

    for gi in range(pos_ref.shape[1]):
        sel = _one_hot(pos_ref[0, gi:gi + 1, :], cap)
        acc_ref[...] += _dg(jnp.where(sel, 1.0, 0.0).astype(BF16),
                            h_ref[gi * LANES:(gi + 1) * LANES, :], _NN)
        gacc_ref[...] += jnp.sum(jnp.where(sel, aff_ref[0, gi:gi + 1, :], 0.0), axis=1, keepdims=True)

    @pl.when(kt == pl.num_programs(1) - 1)
    def _():
        xe_ref[0] = acc_ref[...].astype(xe_ref.dtype)
        gate_ref[0] = gacc_ref[...]


def _gather(pos, aff, h2, cap):
    ne, nb, _ = pos.shape
    n_tok, d = h2.shape
    gb = _pick(nb, 8, SUBLANES)
    return pl.pallas_call(
        _gather_kernel,
        grid=(ne, nb // gb),
        in_specs=[pl.BlockSpec((1, gb, LANES), lambda e, kt: (e, kt, 0)),
                  pl.BlockSpec((1, gb, LANES), lambda e, kt: (e, kt, 0)),
                  pl.BlockSpec((gb * LANES, d), lambda e, kt: (kt, 0))],
        out_specs=[pl.BlockSpec((1, cap, d), lambda e, kt: (e, 0, 0)),
                   pl.BlockSpec((1, cap, 1), lambda e, kt: (e, 0, 0))],
        out_shape=[jax.ShapeDtypeStruct((ne, cap, d), BF16),
                   jax.ShapeDtypeStruct((ne, cap, 1), F32)],
        scratch_shapes=[pltpu.VMEM((cap, d), F32), pltpu.VMEM((cap, 1), F32)],
        compiler_params=_cparams("arbitrary", "arbitrary"),
        name="moe_gather",
    )(pos, aff, h2)


def _ffn_kernel(xe_ref, wg_ref, wu_ref, wd_ref, gate_ref, ye_ref, acc_ref):
    f = pl.program_id(1)

    @pl.when(f == 0)
    def _():
        acc_ref[...] = jnp.zeros_like(acc_ref)

    xe = xe_ref[0]
    hid = _silu(_dot1(xe, wg_ref[0])) * _dot1(xe, wu_ref[0])
    acc_ref[...] += _dot1(hid, wd_ref[0])

    @pl.when(f == pl.num_programs(1) - 1)
    def _():
        ye_ref[0] = (acc_ref[...] * gate_ref[0]).astype(ye_ref.dtype)


def _expert_ffn(xe, gate, w_gate, w_up, w_down):
    ne, cap, d = xe.shape
    ff = w_gate.shape[-1]
    tf = _pick(ff, 256, LANES)
    return pl.pallas_call(
        _ffn_kernel,
        grid=(ne, ff // tf),
        in_specs=[pl.BlockSpec((1, cap, d), lambda e, f: (e, 0, 0)),
                  pl.BlockSpec((1, d, tf), lambda e, f: (e, 0, f)),
                  pl.BlockSpec((1, d, tf), lambda e, f: (e, 0, f)),
                  pl.BlockSpec((1, tf, d), lambda e, f: (e, f, 0)),
                  pl.BlockSpec((1, cap, 1), lambda e, f: (e, 0, 0))],
        out_specs=pl.BlockSpec((1, cap, d), lambda e, f: (e, 0, 0)),
        out_shape=jax.ShapeDtypeStruct((ne, cap, d), BF16),
        scratch_shapes=[pltpu.VMEM((cap, d), F32)],
        compiler_params=_cparams("arbitrary", "arbitrary"),
        name="expert_ffn",
    )(xe, w_gate, w_up, w_down, gate)


def _scatter_kernel(pos_ref, ye_ref, x_ref, g_ref, fg_ref, o_ref, acc_ref, *, final_norm):
    e = pl.program_id(2)
    cap = ye_ref.shape[1]

    @pl.when(e == 0)
    def _():
        acc_ref[...] = jnp.zeros_like(acc_ref)

    ye = ye_ref[0]
    gb = acc_ref.shape[0] // LANES
    row0 = (pl.program_id(0) * pl.num_programs(1) + pl.program_id(1)) * gb
    for gi in range(gb):
        sel = jnp.where(_one_hot(pos_ref[0, pl.ds(row0 + gi, 1), :], cap), 1.0, 0.0).astype(BF16)
        acc_ref[gi * LANES:(gi + 1) * LANES, :] += _dg(sel, ye, _TN)

    @pl.when(e == pl.num_programs(2) - 1)
    def _():
        x = x_ref[0] + g_ref[0] * acc_ref[...]
        if final_norm:
            x = x * lax.rsqrt(jnp.mean(x * x, axis=-1, keepdims=True) + EPS) * fg_ref[...]
        o_ref[0] = x


def _scatter_residual(pos, ye, x, gate2, final_g, final_norm):
    ne, nb, _ = pos.shape
    cap, d = ye.shape[1], ye.shape[2]
    bsz, t, _ = x.shape
    tt = _pick(t, 512, LANES)
    n_t = t // tt
    return pl.pallas_call(
        functools.partial(_scatter_kernel, final_norm=final_norm),
        grid=(bsz, n_t, ne),
        in_specs=[pl.BlockSpec((1, nb, LANES), lambda b, i, e: (e, 0, 0)),
                  pl.BlockSpec((1, cap, d), lambda b, i, e: (e, 0, 0)),
                  pl.BlockSpec((1, tt, d), lambda b, i, e: (b, i, 0)),
                  pl.BlockSpec((1, 1, d), lambda b, i, e: (b, 0, 0)),
                  pl.BlockSpec((1, d), lambda b, i, e: (0, 0))],
        out_specs=pl.BlockSpec((1, tt, d), lambda b, i, e: (b, i, 0)),
        out_shape=jax.ShapeDtypeStruct((bsz, t, d), F32),
        scratch_shapes=[pltpu.VMEM((tt, d), F32)],
        compiler_params=_cparams("arbitrary", "arbitrary", "arbitrary"),
        name="moe_scatter",
    )(pos, ye, x, gate2, final_g.reshape(1, d))


def _encoder_layer(x, ada, lw, lambda_init, final_g, final_norm):
    bsz, t, d = x.shape
    shift1, scale1, gate1, shift2, scale2, gate2 = [a[:, None, :] for a in jnp.split(ada, 6, axis=-1)]

    h = _norm_mod(x, lw["norm1_g"], scale1, shift1).reshape(bsz * t, d)
    z_rkv = _matmul(h, lw["w_rkv"]).reshape(bsz, t, -1)
    z_lo = _matmul(h, lw["w_lo"]).reshape(bsz, t, -1)
    z_qkv = _matmul(h, lw["w_qkv"]).reshape(bsz, t, -1)
    z_gates = _matmul(h, lw["w_gates"]).reshape(bsz, t, -1)

    (r, v, g, bonus, kd_f, b_f, an_f, cum_f, kd_b, b_b, an_b, cum_b) = _rw_prep(
        z_rkv, z_lo, lw["mu_r"], lw["mu_l"], lw["w0"], lw["w2"], lw["a0"], lw["a2"], lw["g2"],
        lw["k_k"], lw["k_a"], lw["r_k"])
    y_f = _rw_scan(r, v, kd_f, b_f, an_f, cum_f, reverse=False)
    y_b = _rw_scan(r, v, kd_b, b_b, an_b, cum_b, reverse=True)

    att = _diff_attention(z_qkv, lw["lam"], lw["sub_g"], lambda_init)
    merged = _merge(y_f, y_b, bonus, g, att, z_gates, lw["lnx_w"], lw["lnx_b"])
    x1 = _matmul_residual(merged, lw["w_out"], x, gate1)

    n_tok = bsz * t
    ne = lw["w_router_t"].shape[0]
    cap = max(1, CAPACITY_FACTOR * n_tok // ne)
    h2, aff = _router(x1, lw["norm2_g"], scale2, shift2, lw["w_router_t"])
    pos = _select(aff, cap)
    xe, gate = _gather(pos, aff, h2.reshape(n_tok, d), cap)
    ye = _expert_ffn(xe, gate, lw["w_gate"], lw["w_up"], lw["w_down"])
    return _scatter_residual(pos, ye, x1, gate2, final_g, final_norm)


def kernel(x_prompt, x_sample, c_prompt, c_sample, w_ada, b_ada, norm1_g, w_in, mu_shift, w0, w2, a0, a2, g2, k_k, k_a, r_k, lnx_w, lnx_b, lam_q1, lam_k1, lam_q2, lam_k2, sub_g, w_out, norm2_g, w_router, w_gate, w_up, w_down, final_g):
    depth = w_ada.shape[0]
    d = x_prompt.shape[-1]
    w3 = 3 * d
    n_lo = 2 * w2.shape[2] + 2 * a2.shape[2] + g2.shape[1]
    xs = [x_prompt, x_sample]
    n_rows = [x_prompt.shape[0], x_sample.shape[0]]
    c_all = jnp.concatenate([c_prompt, c_sample], axis=0)
    pad = (-c_all.shape[0]) % SUBLANES
    c_all = jnp.pad(c_all, ((0, pad), (0, 0)))
    for l in range(depth):
        lambda_init = 0.8 - 0.6 * math.exp(-0.3 * l)
        w_in_l = w_in[l].astype(BF16)
        lam = (jnp.exp(jnp.sum(lam_q1[l] * lam_k1[l])) - jnp.exp(jnp.sum(lam_q2[l] * lam_k2[l]))
               + lambda_init).astype(F32).reshape(1)
        lw = dict(
            norm1_g=norm1_g[l], norm2_g=norm2_g[l],
            w_rkv=w_in_l[:, :w3], w_lo=w_in_l[:, w3:w3 + n_lo],
            w_qkv=w_in_l[:, w3 + n_lo:w3 + n_lo + 3 * d], w_gates=w_in_l[:, w3 + n_lo + 3 * d:],
            mu_r=mu_shift[l, :w3], mu_l=mu_shift[l, w3:w3 + n_lo],
            w0=w0[l], w2=w2[l], a0=a0[l], a2=a2[l], g2=g2[l], k_k=k_k[l], k_a=k_a[l], r_k=r_k[l],
            lnx_w=lnx_w[l], lnx_b=lnx_b[l], lam=lam, sub_g=sub_g[l],
            w_out=w_out[l].astype(BF16), w_router_t=w_router[l].T,
            w_gate=w_gate[l], w_up=w_up[l], w_down=w_down[l])
        ada = _ada(c_all, w_ada[l], b_ada[l])
        off = 0
        for gi in range(2):
            xs[gi] = _encoder_layer(xs[gi], ada[off:off + n_rows[gi]], lw, lambda_init, final_g,
                                    final_norm=(l == depth - 1))
            off += n_rows[gi]
    return tuple(xs)
```

```python
import functools
import math

import jax
import jax.numpy as jnp
from jax import lax
from jax.experimental import pallas as pl
from jax.experimental.pallas import tpu as pltpu

F32 = jnp.float32
BF16 = jnp.bfloat16
I32 = jnp.int32

RW_HEAD = 64
DA_QK = 64
DA_V = 2 * DA_QK
RW_CHUNK = 64
CAPACITY_FACTOR = 2
EPS = 1e-6
GN_EPS = RW_HEAD * 1e-5
LANES = 128
SUBLANES = 8
VMEM_LIMIT = 56 * 1024 * 1024


def _cparams(*sem):
    return pltpu.CompilerParams(dimension_semantics=sem, vmem_limit_bytes=VMEM_LIMIT)


def _pick(n, pref, align):
    if n <= pref:
        return n
    t = (pref // align) * align
    while t >= align:
        if n % t == 0:
            return t
        t -= align
    return n


_NN = (((1,), (0,)), ((), ()))
_NT = (((1,), (1,)), ((), ()))
_TN = (((0,), (0,)), ((), ()))


def _dg(a, b, dims):
    return lax.dot_general(a, b, dims, preferred_element_type=F32)


def _dot1(a, b, dims=_NN):
    return _dg(a.astype(BF16), b.astype(BF16), dims)


def _split(x):
    hi = x.astype(BF16)
    lo = (x - hi.astype(F32)).astype(BF16)
    return hi, lo


def _dot3(a, b, dims=_NN):
    ah, al = _split(a)
    bh, bl = _split(b)
    return _dg(ah, bh, dims) + (_dg(ah, bl, dims) + _dg(al, bh, dims))


def _dot_exact_rhs(a, b_bf16, dims=_NN):
    ah, al = _split(a)
    return _dg(ah, b_bf16, dims) + _dg(al, b_bf16, dims)


def _sigmoid(x):
    return 1.0 / (1.0 + jnp.exp(-x))


def _silu(x):
    return x * _sigmoid(x)


def _ada_kernel(c_ref, w_ref, b_ref, o_ref):
    o_ref[...] = _dot1(_silu(c_ref[...]), w_ref[...]) + b_ref[...]


def _ada(c, w, b):
    m, d = c.shape
    n = w.shape[1]
    tn = _pick(n, 1024, LANES)
    return pl.pallas_call(
        _ada_kernel,
        grid=(n // tn,),
        in_specs=[pl.BlockSpec((m, d), lambda j: (0, 0)),
                  pl.BlockSpec((d, tn), lambda j: (0, j)),
                  pl.BlockSpec((1, tn), lambda j: (0, j))],
        out_specs=pl.BlockSpec((m, tn), lambda j: (0, j)),
        out_shape=jax.ShapeDtypeStruct((m, n), F32),
        compiler_params=_cparams("arbitrary"),
        name="ada",
    )(c, w, b.reshape(1, n))


def _norm_mod_kernel(x_ref, g_ref, sc_ref, sh_ref, o_ref):
    x = x_ref[0]
    y = x * lax.rsqrt(jnp.mean(x * x, axis=-1, keepdims=True) + EPS) * g_ref[...]
    o_ref[0] = (y * (1.0 + sc_ref[0]) + sh_ref[0]).astype(o_ref.dtype)


def _norm_mod(x, g, scale, shift):
    bsz, t, d = x.shape
    tt = _pick(t, 512, SUBLANES)
    row = pl.BlockSpec((1, 1, d), lambda b, i: (b, 0, 0))
    return pl.pallas_call(
        _norm_mod_kernel,
        grid=(bsz, t // tt),
        in_specs=[pl.BlockSpec((1, tt, d), lambda b, i: (b, i, 0)),
                  pl.BlockSpec((1, d), lambda b, i: (0, 0)), row, row],
        out_specs=pl.BlockSpec((1, tt, d), lambda b, i: (b, i, 0)),
        out_shape=jax.ShapeDtypeStruct((bsz, t, d), BF16),
        compiler_params=_cparams("arbitrary", "arbitrary"),
        name="norm_mod",
    )(x, g.reshape(1, d), scale, shift)


def _mm_kernel(a_ref, b_ref, o_ref):
    o_ref[...] = _dot1(a_ref[...], b_ref[...]).astype(o_ref.dtype)


def _matmul(a, b, out_dtype=F32, tm_pref=1024, tn_pref=1024):
    m, k = a.shape
    n = b.shape[1]
    tm = _pick(m, tm_pref, SUBLANES)
    tn = _pick(n, tn_pref, LANES)
    return pl.pallas_call(
        _mm_kernel,
        grid=(m // tm, n // tn),
        in_specs=[pl.BlockSpec((tm, k), lambda i, j: (i, 0)),
                  pl.BlockSpec((k, tn), lambda i, j: (0, j))],
        out_specs=pl.BlockSpec((tm, tn), lambda i, j: (i, j)),
        out_shape=jax.ShapeDtypeStruct((m, n), out_dtype),
        compiler_params=_cparams("arbitrary", "arbitrary"),
        name="matmul",
    )(a, b)


def _mm_res_kernel(a_ref, b_ref, x_ref, g_ref, o_ref):
    o_ref[0] = x_ref[0] + g_ref[0] * _dot1(a_ref[0], b_ref[...])


def _matmul_residual(a, b, x, gate):
    bsz, t, k = a.shape
    n = b.shape[1]
    tm = _pick(t, 1024, SUBLANES)
    tn = _pick(n, 1024, LANES)
    return pl.pallas_call(
        _mm_res_kernel,
        grid=(bsz, t // tm, n // tn),
        in_specs=[pl.BlockSpec((1, tm, k), lambda bb, i, j: (bb, i, 0)),
                  pl.BlockSpec((k, tn), lambda bb, i, j: (0, j)),
                  pl.BlockSpec((1, tm, tn), lambda bb, i, j: (bb, i, j)),
                  pl.BlockSpec((1, 1, tn), lambda bb, i, j: (bb, 0, j))],
        out_specs=pl.BlockSpec((1, tm, tn), lambda bb, i, j: (bb, i, j)),
        out_shape=jax.ShapeDtypeStruct((bsz, t, n), F32),
        compiler_params=_cparams("arbitrary", "arbitrary", "arbitrary"),
        name="matmul_residual",
    )(a, b, x, gate)


def _shift_tile(z, prev8, next8, mu, i, n_t):
    tt = z.shape[0]
    pr = jnp.where(i > 0, prev8[SUBLANES - 1:SUBLANES, :], 0.0)
    nx = jnp.where(i < n_t - 1, next8[0:1, :], 0.0)
    ridx = lax.broadcasted_iota(I32, z.shape, 0)
    zp = jnp.where(ridx == 0, pr, pltpu.roll(z, 1, 0))
    zn = jnp.where(ridx == tt - 1, nx, pltpu.roll(z, tt - 1, 0))
    return z + mu * (0.5 * (zp + zn) - z)


def _seg_sum(x, e_ref, et_ref):
    s = _dot_exact_rhs(x, e_ref[...])
    return _dot_exact_rhs(s, et_ref[...])


def _rw_prep_kernel(zr_ref, zrp_ref, zrn_ref, zl_ref, zlp_ref, zln_ref, mur_ref, mul_ref,
                    w0_ref, w2_ref, a0_ref, a2_ref, g2_ref, kk_ref, ka_ref, rk_ref,
                    e_ref, et_ref, lf_ref, lb_ref,
                    r_o, v_o, g_o, bonus_o,
                    kdf_o, bf_o, anf_o, cumf_o, kdb_o, bb_o, anb_o, cumb_o, *, dl, il):
    i = pl.program_id(1)
    n_t = pl.num_programs(1)
    w = r_o.shape[-1]
    zs = _shift_tile(zr_ref[0], zrp_ref[0], zrn_ref[0], mur_ref[...], i, n_t)
    zl = _shift_tile(zl_ref[0], zlp_ref[0], zln_ref[0], mul_ref[...], i, n_t)
    r = zs[:, 0:w]
    k = zs[:, w:2 * w]
    v = zs[:, 2 * w:3 * w]
    gd = zl[:, 2 * dl + 2 * il:]
    kkr = k * kk_ref[...]
    kk = kkr / jnp.maximum(jnp.sqrt(_seg_sum(kkr * kkr, e_ref, et_ref)), 1e-12)
    r_o[0] = r
    v_o[0] = v
    g_o[0] = _dot1(_sigmoid(gd), g2_ref[...])
    kd_sum = None
    outs = ((kdf_o, bf_o, anf_o, cumf_o, lf_ref), (kdb_o, bb_o, anb_o, cumb_o, lb_ref))
    for z, (kd_o, b_o, an_o, cum_o, l_ref) in enumerate(outs):
        wd = zl[:, z * dl:(z + 1) * dl]
        ad = zl[:, 2 * dl + z * il:2 * dl + (z + 1) * il]
        wl = w0_ref[z:z + 1, :] + _dot1(jnp.tanh(wd), w2_ref[z])
        u = -wl
        softplus = jnp.maximum(u, 0.0) + jnp.log(1.0 + jnp.exp(-jnp.abs(u)))
        lw = -jnp.exp(-softplus - 0.5)
        icl = _sigmoid(a0_ref[z:z + 1, :] + _dot1(ad, a2_ref[z]))
        kd = k * (1.0 + (icl - 1.0) * ka_ref[...])
        kd_o[0] = kd
        b_o[0] = kk * icl
        an_o[0] = -kk * jnp.exp(-lw)
        lh = lw.astype(BF16)
        rem = lw - lh.astype(F32)
        lm = rem.astype(BF16)
        ll = (rem - lm.astype(F32)).astype(BF16)
        lmat = l_ref[...]
        cum_o[0] = _dg(lmat, lh, _NN) + (_dg(lmat, lm, _NN) + _dg(lmat, ll, _NN))
        kd_sum = kd if kd_sum is None else kd_sum + kd
    bonus_o[0] = _seg_sum(r * kd_sum * rk_ref[...], e_ref, et_ref) * v


def _rw_prep(z_rkv, z_lo, mu_r, mu_l, w0, w2, a0, a2, g2, k_k, k_a, r_k):
    bsz, t, w3 = z_rkv.shape
    w = w3 // 3
    lw_cols = z_lo.shape[-1]
    dl, il = w2.shape[1], a2.shape[1]
    nh = w // RW_HEAD
    tt = _pick(t, 128, RW_CHUNK)
    n_t = t // tt
    hb = tt // SUBLANES
    head = jnp.arange(w, dtype=I32) // RW_HEAD
    e = (head[:, None] == jnp.arange(nh, dtype=I32)[None, :]).astype(BF16)
    et = e.T
    ti = jnp.arange(tt, dtype=I32)
    same = (ti[:, None] // RW_CHUNK) == (ti[None, :] // RW_CHUNK)
    lf = (same & (ti[None, :] <= ti[:, None])).astype(BF16)
    lb = (same & (ti[None, :] >= ti[:, None])).astype(BF16)

    def tile(c):
        return pl.BlockSpec((1, tt, c), lambda b, i: (b, i, 0))

    def prev(c):
        return pl.BlockSpec((1, SUBLANES, c), lambda b, i: (b, jnp.maximum(i * hb - 1, 0), 0))

    def nxt(c):
        return pl.BlockSpec((1, SUBLANES, c),
                            lambda b, i: (b, jnp.minimum((i + 1) * hb, t // SUBLANES - 1), 0))

    def full(a):
        nd = a.ndim
        return pl.BlockSpec(a.shape, lambda b, i: (0,) * nd)

    consts = [mu_r.reshape(1, w3), mu_l.reshape(1, lw_cols), w0, w2, a0, a2, g2,
              k_k.reshape(1, w), k_a.reshape(1, w), r_k.reshape(1, w), e, et, lf, lb]
    out = jax.ShapeDtypeStruct((bsz, t, w), F32)
    return pl.pallas_call(
        functools.partial(_rw_prep_kernel, dl=dl, il=il),
        grid=(bsz, n_t),
        in_specs=[tile(w3), prev(w3), nxt(w3), tile(lw_cols), prev(lw_cols), nxt(lw_cols)]
        + [full(a) for a in consts],
        out_specs=[tile(w)] * 12,
        out_shape=[out] * 12,
        compiler_params=_cparams("arbitrary", "arbitrary"),
        name="rw_prep",
    )(z_rkv, z_rkv, z_rkv, z_lo, z_lo, z_lo, *consts)


def _scan_kernel(r_ref, v_ref, kd_ref, b_ref, an_ref, cum_ref, y_ref, h_ref, *, reverse, n_groups,
                 group, prec):
    c = RW_CHUNK
    n = RW_HEAD

    @pl.when(pl.program_id(1) == 0)
    def _():
        h_ref[...] = jnp.zeros_like(h_ref)

    row = lax.broadcasted_iota(I32, (c, c), 0)
    col = lax.broadcasted_iota(I32, (c, c), 1)
    if reverse:
        strict, incl, last = col > row, col >= row, 0
    else:
        strict, incl, last = col < row, col <= row, c - 1
    eye = row == col
    eye_f = jnp.where(eye, 1.0, 0.0)

    dots = {1: _dot1, 3: _dot3}
    d_g, d_inv, d_av, d_z, d_o, d_st = (dots[p] for p in prec)
    hrange = range(group)

    def group_body(gi, carry):
        width = group * n
        sl = pl.ds(pl.multiple_of(gi * width, width), width)
        cum = cum_ref[0, :, sl]
        tot = cum[last:last + 1, :]
        e1 = jnp.exp(cum)
        e2 = jnp.exp(-cum)
        et = jnp.exp(tot)
        rt = r_ref[0, :, sl] * e1
        at = an_ref[0, :, sl] * e1
        bt = b_ref[0, :, sl] * e2
        kt = kd_ref[0, :, sl] * e2
        bh = bt * et
        kh = kt * et
        v2 = v_ref[0, :, sl]
        hs = h_ref[:, sl]
        hsl = [slice(n * j, n * (j + 1)) for j in hrange]
        g = [d_g(jnp.concatenate([at[:, s], rt[:, s]], axis=0),
                 jnp.concatenate([bt[:, s], kt[:, s]], axis=0), _NT) for s in hsl]
        nab = [jnp.where(strict, x[:c, :c], 0.0) for x in g]
        aak = [jnp.where(strict, x[:c, c:], 0.0) for x in g]
        mbk = [jnp.concatenate([jnp.where(incl, x[c:, :c], 0.0), jnp.where(incl, x[c:, c:], 0.0)],
                               axis=1) for x in g]
        vs = [v2[:, s] for s in hsl]
        av = [d_av(a, vv) for a, vv in zip(aak, vs)]
        tm = [eye_f + x for x in nab]
        npow = nab
        for _ in range(5):
            npow = [d_inv(x, x) for x in npow]
            tm = [t_ + d_inv(x, t_) for x, t_ in zip(npow, tm)]
        zz = [d_z(t_, jnp.concatenate([at[:, s], a], axis=1)) for t_, s, a in zip(tm, hsl, av)]
        rhs2 = [jnp.concatenate([z_, jnp.concatenate([jnp.zeros((c, n), F32), vv], axis=1)], axis=0)
                for z_, vv in zip(zz, vs)]
        o1 = [d_o(m, x) for m, x in zip(mbk, rhs2)]
        o2 = [d_o(jnp.concatenate([bh[:, s], kh[:, s]], axis=0), x, _TN)
              for s, x in zip(hsl, rhs2)]
        o3 = [d_st(jnp.concatenate([rt[:, s] + a[:, :n],
                                    b_[:, :n] + jnp.where(eye, et[:, s], 0.0)], axis=0), hs[:, s])
              for s, a, b_ in zip(hsl, o1, o2)]
        y_ref[0, :, sl] = jnp.concatenate([x[:c] + a[:, n:] for x, a in zip(o3, o1)], axis=1)
        h_ref[:, sl] = jnp.concatenate([x[c:] + b_[:, n:] for x, b_ in zip(o3, o2)], axis=1)
        return carry

    lax.fori_loop(0, n_groups, group_body, 0)


def _rw_scan(r, v, kd, b, an, cum, reverse, prec=(1, 1, 1, 1, 1, 1), group=8):
    bsz, t, w = r.shape
    nc = t // RW_CHUNK
    group = min(group, w // RW_HEAD)
    if reverse:
        imap = lambda bb, ci: (bb, nc - 1 - ci, 0)
    else:
        imap = lambda bb, ci: (bb, ci, 0)
    spec = pl.BlockSpec((1, RW_CHUNK, w), imap)
    return pl.pallas_call(
        functools.partial(_scan_kernel, reverse=reverse, n_groups=w // (group * RW_HEAD),
                          group=group, prec=prec),
        grid=(bsz, nc),
        in_specs=[spec] * 6,
        out_specs=spec,
        out_shape=jax.ShapeDtypeStruct((bsz, t, w), F32),
        scratch_shapes=[pltpu.VMEM((RW_HEAD, w), F32)],
        compiler_params=_cparams("arbitrary", "arbitrary"),
        name="rw_scan_bwd" if reverse else "rw_scan_fwd",
    )(r, v, kd, b, an, cum)


def _merge_kernel(yf_ref, yb_ref, bonus_ref, g_ref, att_ref, ga_ref, gb_ref, lw_ref, lb_ref,
                  e_ref, et_ref, o_ref):
    y = yf_ref[0] + yb_ref[0]
    mean = _seg_sum(y, e_ref, et_ref) * (1.0 / RW_HEAD)
    d = y - mean
    var = _seg_sum(d * d, e_ref, et_ref) * (1.0 / RW_HEAD)
    yn = d * lax.rsqrt(var + GN_EPS) * lw_ref[...] + lb_ref[...]
    ya = (yn + bonus_ref[0]) * g_ref[0]
    o_ref[0] = (_sigmoid(ga_ref[0]) * ya + _sigmoid(gb_ref[0]) * att_ref[0]).astype(o_ref.dtype)


def _merge(yf, yb, bonus, g, att, z_gates, lnx_w, lnx_b):
    bsz, t, w = yf.shape
    nh = w // RW_HEAD
    tt = _pick(t, 256, SUBLANES)
    head = jnp.arange(w, dtype=I32) // RW_HEAD
    e = (head[:, None] == jnp.arange(nh, dtype=I32)[None, :]).astype(BF16)
    et = e.T
    tile = pl.BlockSpec((1, tt, w), lambda b, i: (b, i, 0))
    gate_b = pl.BlockSpec((1, tt, w), lambda b, i: (b, i, 1))
    vec = pl.BlockSpec((1, w), lambda b, i: (0, 0))
    return pl.pallas_call(
        _merge_kernel,
        grid=(bsz, t // tt),
        in_specs=[tile, tile, tile, tile, tile, tile, gate_b, vec, vec,
                  pl.BlockSpec(e.shape, lambda b, i: (0, 0)),
                  pl.BlockSpec(et.shape, lambda b, i: (0, 0))],
        out_specs=tile,
        out_shape=jax.ShapeDtypeStruct((bsz, t, w), BF16),
        compiler_params=_cparams("arbitrary", "arbitrary"),
        name="merge",
    )(yf, yb, bonus, g, att, z_gates, z_gates, lnx_w.reshape(1, w), lnx_b.reshape(1, w), e, et)


LOG2E = math.log2(math.e)


def _attn_kernel(slope_ref, lam_ref, q_ref, k_ref, v_ref, sg_ref, d_ref, o_ref, kb_ref, vb_ref,
                 *, out_scale):
    h = pl.program_id(1)
    qi = pl.program_id(2)
    tq = q_ref.shape[1]
    t = k_ref.shape[1]

    @pl.when(qi == 0)
    def _():
        k = k_ref[0]
        kb_ref[0] = k[:, :DA_QK].astype(BF16)
        kb_ref[1] = k[:, DA_QK:].astype(BF16)
        vb_ref[...] = v_ref[0].astype(BF16)

    start = pl.multiple_of((pl.num_programs(2) - 1 - qi) * tq, tq)
    bias = (-LOG2E * slope_ref[h]) * d_ref[:, pl.ds(start, t)]
    q = (q_ref[0] * (DA_QK ** -0.5 * LOG2E)).astype(BF16)
    outs = []
    for c in range(2):
        s = _dg(q[:, c * DA_QK:(c + 1) * DA_QK], kb_ref[c], _NT) + bias
        p = jnp.exp2(s - jnp.max(s, axis=1, keepdims=True))
        l = jnp.sum(p, axis=1, keepdims=True)
        outs.append(_dg(p.astype(BF16), vb_ref[...], _NN) / l)
    o = outs[0] - lam_ref[0] * outs[1]
    o = o * lax.rsqrt(jnp.mean(o * o, axis=-1, keepdims=True) + EPS) * sg_ref[...]
    o_ref[0] = o * out_scale


def _diff_attention(z_qkv, lam, sub_g, lambda_init):
    bsz, t, w3 = z_qkv.shape
    nh = w3 // 3 // DA_V
    tq = _pick(t, 256, LANES)
    slopes = 2.0 ** (-8.0 * jnp.arange(1, nh + 1, dtype=F32) / nh)
    dist = jnp.abs(jnp.arange(tq, dtype=I32)[:, None] + (t - tq)
                   - jnp.arange(2 * t - tq, dtype=I32)[None, :]).astype(F32)
    smem = pl.BlockSpec(memory_space=pltpu.SMEM)
    return pl.pallas_call(
        functools.partial(_attn_kernel, out_scale=1.0 - lambda_init),
        grid=(bsz, nh, t // tq),
        in_specs=[smem, smem,
                  pl.BlockSpec((1, tq, DA_V), lambda b, h, i: (b, i, h)),
                  pl.BlockSpec((1, t, DA_V), lambda b, h, i: (b, 0, nh + h)),
                  pl.BlockSpec((1, t, DA_V), lambda b, h, i: (b, 0, 2 * nh + h)),
                  pl.BlockSpec((1, DA_V), lambda b, h, i: (0, 0)),
                  pl.BlockSpec(dist.shape, lambda b, h, i: (0, 0), pipeline_mode=pl.Buffered(1))],
        out_specs=pl.BlockSpec((1, tq, DA_V), lambda b, h, i: (b, i, h)),
        out_shape=jax.ShapeDtypeStruct((bsz, t, nh * DA_V), F32),
        scratch_shapes=[pltpu.VMEM((2, t, DA_QK), BF16), pltpu.VMEM((t, DA_V), BF16)],
        compiler_params=_cparams("arbitrary", "arbitrary", "arbitrary"),
        name="diff_attention",
    )(slopes, lam, z_qkv, z_qkv, z_qkv, sub_g.reshape(1, DA_V), dist)


def _router_kernel(x_ref, g_ref, sc_ref, sh_ref, wr_ref, h_ref, aff_ref):
    x = x_ref[0]
    y = x * lax.rsqrt(jnp.mean(x * x, axis=-1, keepdims=True) + EPS) * g_ref[...]
    h = y * (1.0 + sc_ref[0]) + sh_ref[0]
    h_ref[0] = h.astype(h_ref.dtype)
    logits = _dot3(wr_ref[...], h, _NT)
    m = jnp.max(logits, axis=0, keepdims=True)
    ex = jnp.exp(logits - m)
    aff = ex / jnp.sum(ex, axis=0, keepdims=True)
    for j in range(aff_ref.shape[1]):
        aff_ref[:, j, :] = aff[:, j * LANES:(j + 1) * LANES]


def _router(x, g, scale, shift, w_router_t):
    bsz, t, d = x.shape
    ne = w_router_t.shape[0]
    tt = _pick(t, SUBLANES * LANES, SUBLANES * LANES)
    nb = tt // LANES
    n_t = t // tt
    row = pl.BlockSpec((1, 1, d), lambda b, i: (b, 0, 0))
    return pl.pallas_call(
        _router_kernel,
        grid=(bsz, n_t),
        in_specs=[pl.BlockSpec((1, tt, d), lambda b, i: (b, i, 0)),
                  pl.BlockSpec((1, d), lambda b, i: (0, 0)), row, row,
                  pl.BlockSpec((ne, d), lambda b, i: (0, 0))],
        out_specs=[pl.BlockSpec((1, tt, d), lambda b, i: (b, i, 0)),
                   pl.BlockSpec((ne, nb, LANES), lambda b, i: (0, b * n_t + i, 0))],
        out_shape=[jax.ShapeDtypeStruct((bsz, t, d), BF16),
                   jax.ShapeDtypeStruct((ne, bsz * t // LANES, LANES), F32)],
        compiler_params=_cparams("arbitrary", "arbitrary"),
        name="router",
    )(x, g.reshape(1, d), scale, shift, w_router_t)


def _sum_all(x):
    return jnp.sum(jnp.sum(x, axis=1, keepdims=True), axis=0, keepdims=True)


def _select_kernel(aff_ref, pos_ref, *, cap):
    nb = aff_ref.shape[1]
    li = lax.broadcasted_iota(I32, (LANES, LANES), 0)
    lj = lax.broadcasted_iota(I32, (LANES, LANES), 1)
    upper = jnp.where(li <= lj, 1.0, 0.0).astype(BF16)
    ri = lax.broadcasted_iota(I32, (nb, nb), 0)
    rj = lax.broadcasted_iota(I32, (nb, nb), 1)
    lower = jnp.where(rj < ri, 1.0, 0.0).astype(BF16)

    def excl_prefix(mask):
        x = jnp.where(mask, 1.0, 0.0)
        inc = _dg(x.astype(BF16), upper, _NN)
        tot = jnp.broadcast_to(inc[:, LANES - 1:LANES], (nb, LANES))
        offs = _dg(lower, tot.astype(BF16), _NN)
        return offs + inc - x

    def expert_body(e, carry):
        bits = pltpu.bitcast(aff_ref[e], I32)

        def bit_body(i, thr):
            cand = thr | jnp.left_shift(jnp.int32(1), 30 - i)
            cnt = _sum_all(jnp.where(bits >= cand, 1.0, 0.0))
            return jnp.where(cnt >= cap, cand, thr)

        thr = lax.fori_loop(0, 31, bit_body, jnp.zeros((1, 1), I32))
        gt = bits > thr
        eq = bits == thr
        need = cap - _sum_all(jnp.where(gt, 1.0, 0.0))
        sel = gt | (eq & (excl_prefix(eq) < need))
        pos_ref[e] = jnp.where(sel, excl_prefix(sel), -1.0).astype(I32)
        return carry

    lax.fori_loop(0, aff_ref.shape[0], expert_body, 0)


def _select(aff, cap):
    ne, nb, _ = aff.shape
    return pl.pallas_call(
        functools.partial(_select_kernel, cap=cap),
        grid=(1,),
        in_specs=[pl.BlockSpec(aff.shape, lambda i: (0, 0, 0))],
        out_specs=pl.BlockSpec(aff.shape, lambda i: (0, 0, 0)),
        out_shape=jax.ShapeDtypeStruct(aff.shape, I32),
        compiler_params=_cparams("arbitrary"),
        name="select",
    )(aff)


def _one_hot(pos_row, cap):
    slots = lax.broadcasted_iota(I32, (cap, LANES), 0)
    return slots == pos_row


def _gather_kernel(pos_ref, aff_ref, h_ref, xe_ref, gate_ref, acc_ref, gacc_ref):
    kt = pl.program_id(1)
    cap = acc_ref.shape[0]

    @pl.when(kt == 0)
    def _():
        acc_ref[...] = jnp.zeros_like(acc_ref)
        gacc_ref[...] = jnp.zeros_like(gacc_ref)

    for gi in range(pos_ref.shape[1]):
        sel = _one_hot(pos_ref[0, gi:gi + 1, :], cap)
        acc_ref[...] += _dg(jnp.where(sel, 1.0, 0.0).astype(BF16),
                            h_ref[gi * LANES:(gi + 1) * LANES, :], _NN)
        gacc_ref[...] += jnp.sum(jnp.where(sel, aff_ref[0, gi:gi + 1, :], 0.0), axis=1, keepdims=True)

    @pl.when(kt == pl.num_programs(1) - 1)
    def _():
        xe_ref[0] = acc_ref[...].astype(xe_ref.dtype)
        gate_ref[0] = gacc_ref[...]


def _gather(pos, aff, h2, cap):
    ne, nb, _ = pos.shape
    n_tok, d = h2.shape
    gb = _pick(nb, 8, SUBLANES)
    return pl.pallas_call(
        _gather_kernel,
        grid=(ne, nb // gb),
        in_specs=[pl.BlockSpec((1, gb, LANES), lambda e, kt: (e, kt, 0)),
                  pl.BlockSpec((1, gb, LANES), lambda e, kt: (e, kt, 0)),
                  pl.BlockSpec((gb * LANES, d), lambda e, kt: (kt, 0))],
        out_specs=[pl.BlockSpec((1, cap, d), lambda e, kt: (e, 0, 0)),
                   pl.BlockSpec((1, cap, 1), lambda e, kt: (e, 0, 0))],
        out_shape=[jax.ShapeDtypeStruct((ne, cap, d), BF16),
                   jax.ShapeDtypeStruct((ne, cap, 1), F32)],
        scratch_shapes=[pltpu.VMEM((cap, d), F32), pltpu.VMEM((cap, 1), F32)],
        compiler_params=_cparams("arbitrary", "arbitrary"),
        name="moe_gather",
    )(pos, aff, h2)


def _ffn_kernel(xe_ref, wg_ref, wu_ref, wd_ref, gate_ref, ye_ref, acc_ref):
    f = pl.program_id(1)

    @pl.when(f == 0)
    def _():
        acc_ref[...] = jnp.zeros_like(acc_ref)

    xe = xe_ref[0]
    hid = _silu(_dot1(xe, wg_ref[0])) * _dot1(xe, wu_ref[0])
    acc_ref[...] += _dot1(hid, wd_ref[0])

    @pl.when(f == pl.num_programs(1) - 1)
    def _():
        ye_ref[0] = (acc_ref[...] * gate_ref[0]).astype(ye_ref.dtype)


def _expert_ffn(xe, gate, w_gate, w_up, w_down):
    ne, cap, d = xe.shape
    ff = w_gate.shape[-1]
    tf = _pick(ff, 256, LANES)
    return pl.pallas_call(
        _ffn_kernel,
        grid=(ne, ff // tf),
        in_specs=[pl.BlockSpec((1, cap, d), lambda e, f: (e, 0, 0)),
                  pl.BlockSpec((1, d, tf), lambda e, f: (e, 0, f)),
                  pl.BlockSpec((1, d, tf), lambda e, f: (e, 0, f)),
                  pl.BlockSpec((1, tf, d), lambda e, f: (e, f, 0)),
                  pl.BlockSpec((1, cap, 1), lambda e, f: (e, 0, 0))],
        out_specs=pl.BlockSpec((1, cap, d), lambda e, f: (e, 0, 0)),
        out_shape=jax.ShapeDtypeStruct((ne, cap, d), BF16),
        scratch_shapes=[pltpu.VMEM((cap, d), F32)],
        compiler_params=_cparams("arbitrary", "arbitrary"),
        name="expert_ffn",
    )(xe, w_gate, w_up, w_down, gate)


def _scatter_kernel(pos_ref, ye_ref, x_ref, g_ref, fg_ref, o_ref, acc_ref, *, final_norm):
    e = pl.program_id(2)
    cap = ye_ref.shape[1]

    @pl.when(e == 0)
    def _():
        acc_ref[...] = jnp.zeros_like(acc_ref)

    ye = ye_ref[0]
    gb = acc_ref.shape[0] // LANES
    row0 = (pl.program_id(0) * pl.num_programs(1) + pl.program_id(1)) * gb
    for gi in range(gb):
        sel = jnp.where(_one_hot(pos_ref[0, pl.ds(row0 + gi, 1), :], cap), 1.0, 0.0).astype(BF16)
        acc_ref[gi * LANES:(gi + 1) * LANES, :] += _dg(sel, ye, _TN)

    @pl.when(e == pl.num_programs(2) - 1)
    def _():
        x = x_ref[0] + g_ref[0] * acc_ref[...]
        if final_norm:
            x = x * lax.rsqrt(jnp.mean(x * x, axis=-1, keepdims=True) + EPS) * fg_ref[...]
        o_ref[0] = x


def _scatter_residual(pos, ye, x, gate2, final_g, final_norm):
    ne, nb, _ = pos.shape
    cap, d = ye.shape[1], ye.shape[2]
    bsz, t, _ = x.shape
    tt = _pick(t, 512, LANES)
    n_t = t // tt
    return pl.pallas_call(
        functools.partial(_scatter_kernel, final_norm=final_norm),
        grid=(bsz, n_t, ne),
        in_specs=[pl.BlockSpec((1, nb, LANES), lambda b, i, e: (e, 0, 0)),
                  pl.BlockSpec((1, cap, d), lambda b, i, e: (e, 0, 0)),
                  pl.BlockSpec((1, tt, d), lambda b, i, e: (b, i, 0)),
                  pl.BlockSpec((1, 1, d), lambda b, i, e: (b, 0, 0)),
                  pl.BlockSpec((1, d), lambda b, i, e: (0, 0))],
        out_specs=pl.BlockSpec((1, tt, d), lambda b, i, e: (b, i, 0)),
        out_shape=jax.ShapeDtypeStruct((bsz, t, d), F32),
        scratch_shapes=[pltpu.VMEM((tt, d), F32)],
        compiler_params=_cparams("arbitrary", "arbitrary", "arbitrary"),
        name="moe_scatter",
    )(pos, ye, x, gate2, final_g.reshape(1, d))


def _encoder_layer(x, ada, lw, lambda_init, final_g, final_norm):
    bsz, t, d = x.shape
    shift1, scale1, gate1, shift2, scale2, gate2 = [a[:, None, :] for a in jnp.split(ada, 6, axis=-1)]

    h = _norm_mod(x, lw["norm1_g"], scale1, shift1).reshape(bsz * t, d)
    z_rkv = _matmul(h, lw["w_rkv"]).reshape(bsz, t, -1)
    z_lo = _matmul(h, lw["w_lo"]).reshape(bsz, t, -1)
    z_qkv = _matmul(h, lw["w_qkv"]).reshape(bsz, t, -1)
    z_gates = _matmul(h, lw["w_gates"]).reshape(bsz, t, -1)

    (r, v, g, bonus, kd_f, b_f, an_f, cum_f, kd_b, b_b, an_b, cum_b) = _rw_prep(
        z_rkv, z_lo, lw["mu_r"], lw["mu_l"], lw["w0"], lw["w2"], lw["a0"], lw["a2"], lw["g2"],
        lw["k_k"], lw["k_a"], lw["r_k"])
    y_f = _rw_scan(r, v, kd_f, b_f, an_f, cum_f, reverse=False)
    y_b = _rw_scan(r, v, kd_b, b_b, an_b, cum_b, reverse=True)

    att = _diff_attention(z_qkv, lw["lam"], lw["sub_g"], lambda_init)
    merged = _merge(y_f, y_b, bonus, g, att, z_gates, lw["lnx_w"], lw["lnx_b"])
    x1 = _matmul_residual(merged, lw["w_out"], x, gate1)

    n_tok = bsz * t
    ne = lw["w_router_t"].shape[0]
    cap = max(1, CAPACITY_FACTOR * n_tok // ne)
    h2, aff = _router(x1, lw["norm2_g"], scale2, shift2, lw["w_router_t"])
    pos = _select(aff, cap)
    xe, gate = _gather(pos, aff, h2.reshape(n_tok, d), cap)
    ye = _expert_ffn(xe, gate, lw["w_gate"], lw["w_up"], lw["w_down"])
    return _scatter_residual(pos, ye, x1, gate2, final_g, final_norm)


def kernel(x_prompt, x_sample, c_prompt, c_sample, w_ada, b_ada, norm1_g, w_in, mu_shift, w0, w2, a0, a2, g2, k_k, k_a, r_k, lnx_w, lnx_b, lam_q1, lam_k1, lam_q2, lam_k2, sub_g, w_out, norm2_g, w_router, w_gate, w_up, w_down, final_g):
    depth = w_ada.shape[0]
    d = x_prompt.shape[-1]
    w3 = 3 * d
    n_lo = 2 * w2.shape[2] + 2 * a2.shape[2] + g2.shape[1]
    xs = [x_prompt, x_sample]
    n_rows = [x_prompt.shape[0], x_sample.shape[0]]
    c_all = jnp.concatenate([c_prompt, c_sample], axis=0)
    pad = (-c_all.shape[0]) % SUBLANES
    c_all = jnp.pad(c_all, ((0, pad), (0, 0)))
    for l in range(depth):
        lambda_init = 0.8 - 0.6 * math.exp(-0.3 * l)
        w_in_l = w_in[l].astype(BF16)
        lam = (jnp.exp(jnp.sum(lam_q1[l] * lam_k1[l])) - jnp.exp(jnp.sum(lam_q2[l] * lam_k2[l]))
               + lambda_init).astype(F32).reshape(1)
        lw = dict(
            norm1_g=norm1_g[l], norm2_g=norm2_g[l],
            w_rkv=w_in_l[:, :w3], w_lo=w_in_l[:, w3:w3 + n_lo],
            w_qkv=w_in_l[:, w3 + n_lo:w3 + n_lo + 3 * d], w_gates=w_in_l[:, w3 + n_lo + 3 * d:],
            mu_r=mu_shift[l, :w3], mu_l=mu_shift[l, w3:w3 + n_lo],
            w0=w0[l], w2=w2[l], a0=a0[l], a2=a2[l], g2=g2[l], k_k=k_k[l], k_a=k_a[l], r_k=r_k[l],
            lnx_w=lnx_w[l], lnx_b=lnx_b[l], lam=lam, sub_g=sub_g[l],
            w_out=w_out[l].astype(BF16), w_router_t=w_router[l].T,
            w_gate=w_gate[l], w_up=w_up[l], w_down=w_down[l])
        ada = _ada(c_all, w_ada[l], b_ada[l])
        off = 0
        for gi in range(2):
            xs[gi] = _encoder_layer(xs[gi], ada[off:off + n_rows[gi]], lw, lambda_init, final_g,
                                    final_norm=(l == depth - 1))
            off += n_rows[gi]
    return tuple(xs)
```

```python
import functools
import math

import jax
import jax.numpy as jnp
from jax import lax
from jax.experimental import pallas as pl
from jax.experimental.pallas import tpu as pltpu

F32 = jnp.float32
BF16 = jnp.bfloat16
I32 = jnp.int32

RW_HEAD = 64
DA_QK = 64
DA_V = 2 * DA_QK
RW_CHUNK = 64
CAPACITY_FACTOR = 2
EPS = 1e-6
GN_EPS = RW_HEAD * 1e-5
LANES = 128
SUBLANES = 8
VMEM_LIMIT = 56 * 1024 * 1024


def _cparams(*sem):
    return pltpu.CompilerParams(dimension_semantics=sem, vmem_limit_bytes=VMEM_LIMIT)


def _pick(n, pref, align):
    if n <= pref:
        return n
    t = (pref // align) * align
    while t >= align:
        if n % t == 0:
            return t
        t -= align
    return n


_NN = (((1,), (0,)), ((), ()))
_NT = (((1,), (1,)), ((), ()))
_TN = (((0,), (0,)), ((), ()))


def _dg(a, b, dims):
    return lax.dot_general(a, b, dims, preferred_element_type=F32)


def _dot1(a, b, dims=_NN):
    return _dg(a.astype(BF16), b.astype(BF16), dims)


def _split(x):
    hi = x.astype(BF16)
    lo = (x - hi.astype(F32)).astype(BF16)
    return hi, lo


def _dot3(a, b, dims=_NN):
    ah, al = _split(a)
    bh, bl = _split(b)
    return _dg(ah, bh, dims) + (_dg(ah, bl, dims) + _dg(al, bh, dims))


def _dot_exact_rhs(a, b_bf16, dims=_NN):
    ah, al = _split(a)
    return _dg(ah, b_bf16, dims) + _dg(al, b_bf16, dims)


def _sigmoid(x):
    return 1.0 / (1.0 + jnp.exp(-x))


def _silu(x):
    return x * _sigmoid(x)


def _ada_kernel(c_ref, w_ref, b_ref, o_ref):
    o_ref[...] = _dot1(_silu(c_ref[...]), w_ref[...]) + b_ref[...]


def _ada(c, w, b):
    m, d = c.shape
    n = w.shape[1]
    tn = _pick(n, 1024, LANES)
    return pl.pallas_call(
        _ada_kernel,
        grid=(n // tn,),
        in_specs=[pl.BlockSpec((m, d), lambda j: (0, 0)),
                  pl.BlockSpec((d, tn), lambda j: (0, j)),
                  pl.BlockSpec((1, tn), lambda j: (0, j))],
        out_specs=pl.BlockSpec((m, tn), lambda j: (0, j)),
        out_shape=jax.ShapeDtypeStruct((m, n), F32),
        compiler_params=_cparams("arbitrary"),
        name="ada",
    )(c, w, b.reshape(1, n))


def _norm_mod_kernel(x_ref, g_ref, sc_ref, sh_ref, o_ref):
    x = x_ref[0]
    y = x * lax.rsqrt(jnp.mean(x * x, axis=-1, keepdims=True) + EPS) * g_ref[...]
    o_ref[0] = (y * (1.0 + sc_ref[0]) + sh_ref[0]).astype(o_ref.dtype)


def _norm_mod(x, g, scale, shift):
    bsz, t, d = x.shape
    tt = _pick(t, 512, SUBLANES)
    row = pl.BlockSpec((1, 1, d), lambda b, i: (b, 0, 0))
    return pl.pallas_call(
        _norm_mod_kernel,
        grid=(bsz, t // tt),
        in_specs=[pl.BlockSpec((1, tt, d), lambda b, i: (b, i, 0)),
                  pl.BlockSpec((1, d), lambda b, i: (0, 0)), row, row],
        out_specs=pl.BlockSpec((1, tt, d), lambda b, i: (b, i, 0)),
        out_shape=jax.ShapeDtypeStruct((bsz, t, d), BF16),
        compiler_params=_cparams("arbitrary", "arbitrary"),
        name="norm_mod",
    )(x, g.reshape(1, d), scale, shift)


def _mm_kernel(a_ref, b_ref, o_ref):
    o_ref[...] = _dot1(a_ref[...], b_ref[...]).astype(o_ref.dtype)


def _matmul(a, b, out_dtype=F32, tm_pref=1024, tn_pref=1024):
    m, k = a.shape
    n = b.shape[1]
    tm = _pick(m, tm_pref, SUBLANES)
    tn = _pick(n, tn_pref, LANES)
    return pl.pallas_call(
        _mm_kernel,
        grid=(m // tm, n // tn),
        in_specs=[pl.BlockSpec((tm, k), lambda i, j: (i, 0)),
                  pl.BlockSpec((k, tn), lambda i, j: (0, j))],
        out_specs=pl.BlockSpec((tm, tn), lambda i, j: (i, j)),
        out_shape=jax.ShapeDtypeStruct((m, n), out_dtype),
        compiler_params=_cparams("arbitrary", "arbitrary"),
        name="matmul",
    )(a, b)


def _mm_res_kernel(a_ref, b_ref, x_ref, g_ref, o_ref):
    o_ref[0] = x_ref[0] + g_ref[0] * _dot1(a_ref[0], b_ref[...])


def _matmul_residual(a, b, x, gate):
    bsz, t, k = a.shape
    n = b.shape[1]
    tm = _pick(t, 1024, SUBLANES)
    tn = _pick(n, 1024, LANES)
    return pl.pallas_call(
        _mm_res_kernel,
        grid=(bsz, t // tm, n // tn),
        in_specs=[pl.BlockSpec((1, tm, k), lambda bb, i, j: (bb, i, 0)),
                  pl.BlockSpec((k, tn), lambda bb, i, j: (0, j)),
                  pl.BlockSpec((1, tm, tn), lambda bb, i, j: (bb, i, j)),
                  pl.BlockSpec((1, 1, tn), lambda bb, i, j: (bb, 0, j))],
        out_specs=pl.BlockSpec((1, tm, tn), lambda bb, i, j: (bb, i, j)),
        out_shape=jax.ShapeDtypeStruct((bsz, t, n), F32),
        compiler_params=_cparams("arbitrary", "arbitrary", "arbitrary"),
        name="matmul_residual",
    )(a, b, x, gate)


def _shift_tile(z, prev8, next8, mu, i, n_t):
    tt = z.shape[0]
    pr = jnp.where(i > 0, prev8[SUBLANES - 1:SUBLANES, :], 0.0)
    nx = jnp.where(i < n_t - 1, next8[0:1, :], 0.0)
    ridx = lax.broadcasted_iota(I32, z.shape, 0)
    zp = jnp.where(ridx == 0, pr, pltpu.roll(z, 1, 0))
    zn = jnp.where(ridx == tt - 1, nx, pltpu.roll(z, tt - 1, 0))
    return z + mu * (0.5 * (zp + zn) - z)


def _seg_sum(x, e_ref, et_ref):
    s = _dot_exact_rhs(x, e_ref[...])
    return _dot_exact_rhs(s, et_ref[...])


def _rw_prep_kernel(zr_ref, zrp_ref, zrn_ref, zl_ref, zlp_ref, zln_ref, mur_ref, mul_ref,
                    w0_ref, w2_ref, a0_ref, a2_ref, g2_ref, kk_ref, ka_ref, rk_ref,
                    e_ref, et_ref, lf_ref, lb_ref,
                    r_o, v_o, g_o, bonus_o,
                    kdf_o, bf_o, anf_o, cumf_o, kdb_o, bb_o, anb_o, cumb_o, *, dl, il):
    i = pl.program_id(1)
    n_t = pl.num_programs(1)
    w = r_o.shape[-1]
    zs = _shift_tile(zr_ref[0], zrp_ref[0], zrn_ref[0], mur_ref[...], i, n_t)
    zl = _shift_tile(zl_ref[0], zlp_ref[0], zln_ref[0], mul_ref[...], i, n_t)
    r = zs[:, 0:w]
    k = zs[:, w:2 * w]
    v = zs[:, 2 * w:3 * w]
    gd = zl[:, 2 * dl + 2 * il:]
    kkr = k * kk_ref[...]
    kk = kkr / jnp.maximum(jnp.sqrt(_seg_sum(kkr * kkr, e_ref, et_ref)), 1e-12)
    r_o[0] = r
    v_o[0] = v
    g_o[0] = _dot1(_sigmoid(gd), g2_ref[...])
    kd_sum = None
    outs = ((kdf_o, bf_o, anf_o, cumf_o, lf_ref), (kdb_o, bb_o, anb_o, cumb_o, lb_ref))
    for z, (kd_o, b_o, an_o, cum_o, l_ref) in enumerate(outs):
        wd = zl[:, z * dl:(z + 1) * dl]
        ad = zl[:, 2 * dl + z * il:2 * dl + (z + 1) * il]
        wl = w0_ref[z:z + 1, :] + _dot1(jnp.tanh(wd), w2_ref[z])
        u = -wl
        softplus = jnp.maximum(u, 0.0) + jnp.log(1.0 + jnp.exp(-jnp.abs(u)))
        lw = -jnp.exp(-softplus - 0.5)
        icl = _sigmoid(a0_ref[z:z + 1, :] + _dot1(ad, a2_ref[z]))
        kd = k * (1.0 + (icl - 1.0) * ka_ref[...])
        kd_o[0] = kd
        b_o[0] = kk * icl
        an_o[0] = -kk * jnp.exp(-lw)
        lh = lw.astype(BF16)
        rem = lw - lh.astype(F32)
        lm = rem.astype(BF16)
        ll = (rem - lm.astype(F32)).astype(BF16)
        lmat = l_ref[...]
        cum_o[0] = _dg(lmat, lh, _NN) + (_dg(lmat, lm, _NN) + _dg(lmat, ll, _NN))
        kd_sum = kd if kd_sum is None else kd_sum + kd
    bonus_o[0] = _seg_sum(r * kd_sum * rk_ref[...], e_ref, et_ref) * v


def _rw_prep(z_rkv, z_lo, mu_r, mu_l, w0, w2, a0, a2, g2, k_k, k_a, r_k):
    bsz, t, w3 = z_rkv.shape
    w = w3 // 3
    lw_cols = z_lo.shape[-1]
    dl, il = w2.shape[1], a2.shape[1]
    nh = w // RW_HEAD
    tt = _pick(t, 128, RW_CHUNK)
    n_t = t // tt
    hb = tt // SUBLANES
    head = jnp.arange(w, dtype=I32) // RW_HEAD
    e = (head[:, None] == jnp.arange(nh, dtype=I32)[None, :]).astype(BF16)
    et = e.T
    ti = jnp.arange(tt, dtype=I32)
    same = (ti[:, None] // RW_CHUNK) == (ti[None, :] // RW_CHUNK)
    lf = (same & (ti[None, :] <= ti[:, None])).astype(BF16)
    lb = (same & (ti[None, :] >= ti[:, None])).astype(BF16)

    def tile(c):
        return pl.BlockSpec((1, tt, c), lambda b, i: (b, i, 0))

    def prev(c):
        return pl.BlockSpec((1, SUBLANES, c), lambda b, i: (b, jnp.maximum(i * hb - 1, 0), 0))

    def nxt(c):
        return pl.BlockSpec((1, SUBLANES, c),
                            lambda b, i: (b, jnp.minimum((i + 1) * hb, t // SUBLANES - 1), 0))

    def full(a):
        nd = a.ndim
        return pl.BlockSpec(a.shape, lambda b, i: (0,) * nd)

    consts = [mu_r.reshape(1, w3), mu_l.reshape(1, lw_cols), w0, w2, a0, a2, g2,
              k_k.reshape(1, w), k_a.reshape(1, w), r_k.reshape(1, w), e, et, lf, lb]
    out = jax.ShapeDtypeStruct((bsz, t, w), F32)
    return pl.pallas_call(
        functools.partial(_rw_prep_kernel, dl=dl, il=il),
        grid=(bsz, n_t),
        in_specs=[tile(w3), prev(w3), nxt(w3), tile(lw_cols), prev(lw_cols), nxt(lw_cols)]
        + [full(a) for a in consts],
        out_specs=[tile(w)] * 12,
        out_shape=[out] * 12,
        compiler_params=_cparams("arbitrary", "arbitrary"),
        name="rw_prep",
    )(z_rkv, z_rkv, z_rkv, z_lo, z_lo, z_lo, *consts)


def _scan_kernel(r_ref, v_ref, kd_ref, b_ref, an_ref, cum_ref, y_ref, h_ref, *, reverse, n_groups,
                 group, prec):
    c = RW_CHUNK
    n = RW_HEAD

    @pl.when(pl.program_id(1) == 0)
    def _():
        h_ref[...] = jnp.zeros_like(h_ref)

    row = lax.broadcasted_iota(I32, (c, c), 0)
    col = lax.broadcasted_iota(I32, (c, c), 1)
    if reverse:
        strict, incl, last = col > row, col >= row, 0
    else:
        strict, incl, last = col < row, col <= row, c - 1
    eye = row == col
    eye_f = jnp.where(eye, 1.0, 0.0)

    dots = {1: _dot1, 3: _dot3}
    d_g, d_inv, d_av, d_z, d_o, d_st = (dots[p] for p in prec)
    hrange = range(group)

    def group_body(gi, carry):
        width = group * n
        sl = pl.ds(pl.multiple_of(gi * width, width), width)
        cum = cum_ref[0, :, sl]
        tot = cum[last:last + 1, :]
        e1 = jnp.exp(cum)
        e2 = jnp.exp(-cum)
        et = jnp.exp(tot)
        rt = r_ref[0, :, sl] * e1
        at = an_ref[0, :, sl] * e1
        bt = b_ref[0, :, sl] * e2
        kt = kd_ref[0, :, sl] * e2
        bh = bt * et
        kh = kt * et
        v2 = v_ref[0, :, sl]
        hs = h_ref[:, sl]
        hsl = [slice(n * j, n * (j + 1)) for j in hrange]
        g = [d_g(jnp.concatenate([at[:, s], rt[:, s]], axis=0),
                 jnp.concatenate([bt[:, s], kt[:, s]], axis=0), _NT) for s in hsl]
        nab = [jnp.where(strict, x[:c, :c], 0.0) for x in g]
        aak = [jnp.where(strict, x[:c, c:], 0.0) for x in g]
        mbk = [jnp.concatenate([jnp.where(incl, x[c:, :c], 0.0), jnp.where(incl, x[c:, c:], 0.0)],
                               axis=1) for x in g]
        vs = [v2[:, s] for s in hsl]
        av = [d_av(a, vv) for a, vv in zip(aak, vs)]
        tm = [eye_f + x for x in nab]
        npow = nab
        for _ in range(5):
            npow = [d_inv(x, x) for x in npow]
            tm = [t_ + d_inv(x, t_) for x, t_ in zip(npow, tm)]
        zz = [d_z(t_, jnp.concatenate([at[:, s], a], axis=1)) for t_, s, a in zip(tm, hsl, av)]
        rhs2 = [jnp.concatenate([z_, jnp.concatenate([jnp.zeros((c, n), F32), vv], axis=1)], axis=0)
                for z_, vv in zip(zz, vs)]
        o1 = [d_o(m, x) for m, x in zip(mbk, rhs2)]
        o2 = [d_o(jnp.concatenate([bh[:, s], kh[:, s]], axis=0), x, _TN)
              for s, x in zip(hsl, rhs2)]
        o3 = [d_st(jnp.concatenate([rt[:, s] + a[:, :n],
                                    b_[:, :n] + jnp.where(eye, et[:, s], 0.0)], axis=0), hs[:, s])
              for s, a, b_ in zip(hsl, o1, o2)]
        y_ref[0, :, sl] = jnp.concatenate([x[:c] + a[:, n:] for x, a in zip(o3, o1)], axis=1)
        h_ref[:, sl] = jnp.concatenate([x[c:] + b_[:, n:] for x, b_ in zip(o3, o2)], axis=1)
        return carry

    lax.fori_loop(0, n_groups, group_body, 0)


def _rw_scan(r, v, kd, b, an, cum, reverse, prec=(1, 1, 1, 1, 1, 1), group=8):
    bsz, t, w = r.shape
    nc = t // RW_CHUNK
    group = min(group, w // RW_HEAD)
    if reverse:
        imap = lambda bb, ci: (bb, nc - 1 - ci, 0)
    else:
        imap = lambda bb, ci: (bb, ci, 0)
    spec = pl.BlockSpec((1, RW_CHUNK, w), imap)
    return pl.pallas_call(
        functools.partial(_scan_kernel, reverse=reverse, n_groups=w // (group * RW_HEAD),
                          group=group, prec=prec),
        grid=(bsz, nc),
        in_specs=[spec] * 6,
        out_specs=spec,
        out_shape=jax.ShapeDtypeStruct((bsz, t, w), F32),
        scratch_shapes=[pltpu.VMEM((RW_HEAD, w), F32)],
        compiler_params=_cparams("arbitrary", "arbitrary"),
        name="rw_scan_bwd" if reverse else "rw_scan_fwd",
    )(r, v, kd, b, an, cum)


def _merge_kernel(yf_ref, yb_ref, bonus_ref, g_ref, att_ref, ga_ref, gb_ref, lw_ref, lb_ref,
                  e_ref, et_ref, o_ref):
    y = yf_ref[0] + yb_ref[0]
    mean = _seg_sum(y, e_ref, et_ref) * (1.0 / RW_HEAD)
    d = y - mean
    var = _seg_sum(d * d, e_ref, et_ref) * (1.0 / RW_HEAD)
    yn = d * lax.rsqrt(var + GN_EPS) * lw_ref[...] + lb_ref[...]
    ya = (yn + bonus_ref[0]) * g_ref[0]
    o_ref[0] = (_sigmoid(ga_ref[0]) * ya + _sigmoid(gb_ref[0]) * att_ref[0]).astype(o_ref.dtype)


def _merge(yf, yb, bonus, g, att, z_gates, lnx_w, lnx_b):
    bsz, t, w = yf.shape
    nh = w // RW_HEAD
    tt = _pick(t, 256, SUBLANES)
    head = jnp.arange(w, dtype=I32) // RW_HEAD
    e = (head[:, None] == jnp.arange(nh, dtype=I32)[None, :]).astype(BF16)
    et = e.T
    tile = pl.BlockSpec((1, tt, w), lambda b, i: (b, i, 0))
    gate_b = pl.BlockSpec((1, tt, w), lambda b, i: (b, i, 1))
    vec = pl.BlockSpec((1, w), lambda b, i: (0, 0))
    return pl.pallas_call(
        _merge_kernel,
        grid=(bsz, t // tt),
        in_specs=[tile, tile, tile, tile, tile, tile, gate_b, vec, vec,
                  pl.BlockSpec(e.shape, lambda b, i: (0, 0)),
                  pl.BlockSpec(et.shape, lambda b, i: (0, 0))],
        out_specs=tile,
        out_shape=jax.ShapeDtypeStruct((bsz, t, w), BF16),
        compiler_params=_cparams("arbitrary", "arbitrary"),
        name="merge",
    )(yf, yb, bonus, g, att, z_gates, z_gates, lnx_w.reshape(1, w), lnx_b.reshape(1, w), e, et)


LOG2E = math.log2(math.e)


def _attn_kernel(slope_ref, lam_ref, q_ref, k_ref, v_ref, sg_ref, d_ref, o_ref, kb_ref, vb_ref,
                 *, out_scale):
    h = pl.program_id(1)
    qi = pl.program_id(2)
    tq = q_ref.shape[1]
    t = k_ref.shape[1]

    @pl.when(qi == 0)
    def _():
        k = k_ref[0]
        kb_ref[0] = k[:, :DA_QK].astype(BF16)
        kb_ref[1] = k[:, DA_QK:].astype(BF16)
        vb_ref[...] = v_ref[0].astype(BF16)

    start = pl.multiple_of((pl.num_programs(2) - 1 - qi) * tq, tq)
    bias = (-LOG2E * slope_ref[h]) * d_ref[:, pl.ds(start, t)]
    q = (q_ref[0] * (DA_QK ** -0.5 * LOG2E)).astype(BF16)
    outs = []
    for c in range(2):
        s = _dg(q[:, c * DA_QK:(c + 1) * DA_QK], kb_ref[c], _NT) + bias
        p = jnp.exp2(s - jnp.max(s, axis=1, keepdims=True))
        l = jnp.sum(p, axis=1, keepdims=True)
        outs.append(_dg(p.astype(BF16), vb_ref[...], _NN) / l)
    o = outs[0] - lam_ref[0] * outs[1]
    o = o * lax.rsqrt(jnp.mean(o * o, axis=-1, keepdims=True) + EPS) * sg_ref[...]
    o_ref[0] = o * out_scale


def _diff_attention(z_qkv, lam, sub_g, lambda_init):
    bsz, t, w3 = z_qkv.shape
    nh = w3 // 3 // DA_V
    tq = _pick(t, 256, LANES)
    slopes = 2.0 ** (-8.0 * jnp.arange(1, nh + 1, dtype=F32) / nh)
    dist = jnp.abs(jnp.arange(tq, dtype=I32)[:, None] + (t - tq)
                   - jnp.arange(2 * t - tq, dtype=I32)[None, :]).astype(F32)
    smem = pl.BlockSpec(memory_space=pltpu.SMEM)
    return pl.pallas_call(
        functools.partial(_attn_kernel, out_scale=1.0 - lambda_init),
        grid=(bsz, nh, t // tq),
        in_specs=[smem, smem,
                  pl.BlockSpec((1, tq, DA_V), lambda b, h, i: (b, i, h)),
                  pl.BlockSpec((1, t, DA_V), lambda b, h, i: (b, 0, nh + h)),
                  pl.BlockSpec((1, t, DA_V), lambda b, h, i: (b, 0, 2 * nh + h)),
                  pl.BlockSpec((1, DA_V), lambda b, h, i: (0, 0)),
                  pl.BlockSpec(dist.shape, lambda b, h, i: (0, 0), pipeline_mode=pl.Buffered(1))],
        out_specs=pl.BlockSpec((1, tq, DA_V), lambda b, h, i: (b, i, h)),
        out_shape=jax.ShapeDtypeStruct((bsz, t, nh * DA_V), F32),
        scratch_shapes=[pltpu.VMEM((2, t, DA_QK), BF16), pltpu.VMEM((t, DA_V), BF16)],
        compiler_params=_cparams("arbitrary", "arbitrary", "arbitrary"),
        name="diff_attention",
    )(slopes, lam, z_qkv, z_qkv, z_qkv, sub_g.reshape(1, DA_V), dist)


def _router_kernel(x_ref, g_ref, sc_ref, sh_ref, wr_ref, h_ref, aff_ref):
    x = x_ref[0]
    y = x * lax.rsqrt(jnp.mean(x * x, axis=-1, keepdims=True) + EPS) * g_ref[...]
    h = y * (1.0 + sc_ref[0]) + sh_ref[0]
    h_ref[0] = h.astype(h_ref.dtype)
    logits = _dot3(wr_ref[...], h, _NT)
    m = jnp.max(logits, axis=0, keepdims=True)
    ex = jnp.exp(logits - m)
    aff = ex / jnp.sum(ex, axis=0, keepdims=True)
    for j in range(aff_ref.shape[1]):
        aff_ref[:, j, :] = aff[:, j * LANES:(j + 1) * LANES]


def _router(x, g, scale, shift, w_router_t):
    bsz, t, d = x.shape
    ne = w_router_t.shape[0]
    tt = _pick(t, SUBLANES * LANES, SUBLANES * LANES)
    nb = tt // LANES
    n_t = t // tt
    row = pl.BlockSpec((1, 1, d), lambda b, i: (b, 0, 0))
    return pl.pallas_call(
        _router_kernel,
        grid=(bsz, n_t),
        in_specs=[pl.BlockSpec((1, tt, d), lambda b, i: (b, i, 0)),
                  pl.BlockSpec((1, d), lambda b, i: (0, 0)), row, row,
                  pl.BlockSpec((ne, d), lambda b, i: (0, 0))],
        out_specs=[pl.BlockSpec((1, tt, d), lambda b, i: (b, i, 0)),
                   pl.BlockSpec((ne, nb, LANES), lambda b, i: (0, b * n_t + i, 0))],
        out_shape=[jax.ShapeDtypeStruct((bsz, t, d), F32),
                   jax.ShapeDtypeStruct((ne, bsz * t // LANES, LANES), F32)],
        compiler_params=_cparams("arbitrary", "arbitrary"),
        name="router",
    )(x, g.reshape(1, d), scale, shift, w_router_t)


def _sum_all(x):
    return jnp.sum(jnp.sum(x, axis=1, keepdims=True), axis=0, keepdims=True)


SLOT_COLS = 8


def _select_kernel(aff_ref, slots_ref, pos_ref, *, cap):
    nb = aff_ref.shape[1]
    li = lax.broadcasted_iota(I32, (LANES, LANES), 0)
    lj = lax.broadcasted_iota(I32, (LANES, LANES), 1)
    upper = jnp.where(li <= lj, 1.0, 0.0).astype(BF16)
    ri = lax.broadcasted_iota(I32, (nb, nb), 0)
    rj = lax.broadcasted_iota(I32, (nb, nb), 1)
    lower = jnp.where(rj < ri, 1.0, 0.0).astype(BF16)

    def excl_prefix(mask):
        x = jnp.where(mask, 1.0, 0.0)
        inc = _dg(x.astype(BF16), upper, _NN)
        tot = jnp.broadcast_to(inc[:, LANES - 1:LANES], (nb, LANES))
        offs = _dg(lower, tot.astype(BF16), _NN)
        return offs + inc - x

    def expert_body(e, carry):
        bits = pltpu.bitcast(aff_ref[e], I32)

        def bit_body(i, thr):
            cand = thr | jnp.left_shift(jnp.int32(1), 30 - i)
            cnt = _sum_all(jnp.where(bits >= cand, 1.0, 0.0))
            return jnp.where(cnt >= cap, cand, thr)

        thr = lax.fori_loop(0, 31, bit_body, jnp.zeros((1, 1), I32))
        gt = bits > thr
        eq = bits == thr
        need = cap - _sum_all(jnp.where(gt, 1.0, 0.0))
        sel = gt | (eq & (excl_prefix(eq) < need))
        pos_ref[...] = jnp.where(sel, excl_prefix(sel), -1.0).astype(I32)
        slots_ref[e] = jnp.zeros((cap, SLOT_COLS), F32)
        slot_ids = lax.broadcasted_iota(I32, (cap, LANES), 0)
        lane_f = lax.broadcasted_iota(I32, (1, LANES), 1).astype(F32)

        def group_body(g, c2):
            a = aff_ref[e, pl.ds(g, 1), :]
            a_hi = a.astype(BF16).astype(F32)
            a_mid = (a - a_hi).astype(BF16).astype(F32)
            a_lo = a - a_hi - a_mid
            base = jnp.full((1, LANES), LANES, F32) * g
            rec = jnp.concatenate([lane_f, base, a_hi, a_mid, a_lo,
                                   jnp.zeros((SLOT_COLS - 5, LANES), F32)], axis=0)
            hot = jnp.where(slot_ids == pos_ref[pl.ds(g, 1), :], 1.0, 0.0)
            slots_ref[e] += _dot1(hot, rec, _NT)
            return c2

        lax.fori_loop(0, nb, group_body, 0)
        return carry

    lax.fori_loop(0, aff_ref.shape[0], expert_body, 0)


def _select(aff, cap):
    ne, nb, _ = aff.shape
    return pl.pallas_call(
        functools.partial(_select_kernel, cap=cap),
        grid=(1,),
        in_specs=[pl.BlockSpec(aff.shape, lambda i: (0, 0, 0))],
        out_specs=pl.BlockSpec((ne, cap, SLOT_COLS), lambda i: (0, 0, 0)),
        out_shape=jax.ShapeDtypeStruct((ne, cap, SLOT_COLS), F32),
        scratch_shapes=[pltpu.VMEM((nb, LANES), I32)],
        compiler_params=_cparams("arbitrary"),
        name="select",
    )(aff)


def _ffn_kernel(idx_ref, h_hbm, wg_ref, wu_ref, wd_ref, slots_ref, moe_in_hbm, moe_hbm,
                xf_ref, xb_ref, acc_ref, rows_ref, sems):
    del moe_in_hbm
    e = pl.program_id(0)
    f = pl.program_id(1)
    n_e = pl.num_programs(0)
    n_f = pl.num_programs(1)
    cap = xf_ref.shape[0]

    def x_copy(ex, p):
        row = idx_ref[ex * cap + p]
        return pltpu.make_async_copy(h_hbm.at[pl.ds(row, 1)], xf_ref.at[pl.ds(p, 1)], sems.at[0])

    def acc_copy(p):
        row = idx_ref[e * cap + p]
        return pltpu.make_async_copy(moe_hbm.at[pl.ds(row, 1)], rows_ref.at[pl.ds(p, 1)], sems.at[1])

    def out_copy(ex, p):
        row = idx_ref[ex * cap + p]
        return pltpu.make_async_copy(rows_ref.at[pl.ds(p, 1)], moe_hbm.at[pl.ds(row, 1)], sems.at[2])

    def for_slots(fn):
        def body(p, c):
            fn(p)
            return c
        lax.fori_loop(0, cap, body, 0, unroll=8)

    @pl.when(f == 0)
    def _():
        @pl.when(e == 0)
        def _():
            for_slots(lambda p: x_copy(e, p).start())

        for_slots(lambda p: x_copy(e, p).wait())
        xb_ref[...] = xf_ref[...].astype(BF16)

        @pl.when(e + 1 < n_e)
        def _():
            for_slots(lambda p: x_copy(e + 1, p).start())

        acc_ref[...] = jnp.zeros_like(acc_ref)

    xe = xb_ref[...]
    hid = _silu(_dot1(xe, wg_ref[0])) * _dot1(xe, wu_ref[0])
    acc_ref[...] += _dot1(hid, wd_ref[0])

    @pl.when(f == n_f // 2)
    def _():
        @pl.when(e > 0)
        def _():
            for_slots(lambda p: out_copy(e - 1, p).wait())

        for_slots(lambda p: acc_copy(p).start())

    @pl.when(f == n_f - 1)
    def _():
        for_slots(lambda p: acc_copy(p).wait())
        rec = slots_ref[0]
        gate = rec[:, 2:3] + rec[:, 3:4] + rec[:, 4:5]
        rows_ref[...] += acc_ref[...] * gate
        for_slots(lambda p: out_copy(e, p).start())

        @pl.when(e == n_e - 1)
        def _():
            for_slots(lambda p: out_copy(e, p).wait())


def _expert_ffn(idx, slots, h2, w_gate, w_up, w_down):
    ne, cap, _ = slots.shape
    n_tok, d = h2.shape
    ff = w_gate.shape[-1]
    tf = _pick(ff, 256, LANES)
    n_f = ff // tf
    assert n_f >= 2, "accumulator rows are fetched mid-way through an expert's steps"
    any_spec = pl.BlockSpec(memory_space=pl.ANY)
    grid_spec = pltpu.PrefetchScalarGridSpec(
        num_scalar_prefetch=1,
        grid=(ne, n_f),
        in_specs=[any_spec,
                  pl.BlockSpec((1, d, tf), lambda e, f, idx_ref: (e, 0, f)),
                  pl.BlockSpec((1, d, tf), lambda e, f, idx_ref: (e, 0, f)),
                  pl.BlockSpec((1, tf, d), lambda e, f, idx_ref: (e, f, 0)),
                  pl.BlockSpec((1, cap, SLOT_COLS), lambda e, f, idx_ref: (e, 0, 0)),
                  any_spec],
        out_specs=any_spec,
        scratch_shapes=[pltpu.VMEM((cap, d), F32), pltpu.VMEM((cap, d), BF16),
                        pltpu.VMEM((cap, d), F32), pltpu.VMEM((cap, d), F32),
                        pltpu.SemaphoreType.DMA((3,))])
    return pl.pallas_call(
        _ffn_kernel,
        grid_spec=grid_spec,
        out_shape=jax.ShapeDtypeStruct((n_tok, d), F32),
        input_output_aliases={6: 0},
        compiler_params=_cparams("arbitrary", "arbitrary"),
        name="expert_ffn",
    )(idx, h2, w_gate, w_up, w_down, slots, jnp.zeros((n_tok, d), F32))


def _residual_out_kernel(x_ref, m_ref, g_ref, fg_ref, o_ref, *, final_norm):
    x = x_ref[0] + g_ref[0] * m_ref[0]
    if final_norm:
        x = x * lax.rsqrt(jnp.mean(x * x, axis=-1, keepdims=True) + EPS) * fg_ref[...]
    o_ref[0] = x


def _residual_out(x, moe, gate2, final_g, final_norm):
    bsz, t, d = x.shape
    tt = _pick(t, 512, SUBLANES)
    tile = pl.BlockSpec((1, tt, d), lambda b, i: (b, i, 0))
    return pl.pallas_call(
        functools.partial(_residual_out_kernel, final_norm=final_norm),
        grid=(bsz, t // tt),
        in_specs=[tile, tile, pl.BlockSpec((1, 1, d), lambda b, i: (b, 0, 0)),
                  pl.BlockSpec((1, d), lambda b, i: (0, 0))],
        out_specs=tile,
        out_shape=jax.ShapeDtypeStruct((bsz, t, d), F32),
        compiler_params=_cparams("arbitrary", "arbitrary"),
        name="residual_out",
    )(x, moe, gate2, final_g.reshape(1, d))


def _encoder_layer(x, ada, lw, lambda_init, final_g, final_norm):
    bsz, t, d = x.shape
    shift1, scale1, gate1, shift2, scale2, gate2 = [a[:, None, :] for a in jnp.split(ada, 6, axis=-1)]

    h = _norm_mod(x, lw["norm1_g"], scale1, shift1).reshape(bsz * t, d)
    z_rkv = _matmul(h, lw["w_rkv"]).reshape(bsz, t, -1)
    z_lo = _matmul(h, lw["w_lo"]).reshape(bsz, t, -1)
    z_qkv = _matmul(h, lw["w_qkv"]).reshape(bsz, t, -1)
    z_gates = _matmul(h, lw["w_gates"]).reshape(bsz, t, -1)

    (r, v, g, bonus, kd_f, b_f, an_f, cum_f, kd_b, b_b, an_b, cum_b) = _rw_prep(
        z_rkv, z_lo, lw["mu_r"], lw["mu_l"], lw["w0"], lw["w2"], lw["a0"], lw["a2"], lw["g2"],
        lw["k_k"], lw["k_a"], lw["r_k"])
    y_f = _rw_scan(r, v, kd_f, b_f, an_f, cum_f, reverse=False)
    y_b = _rw_scan(r, v, kd_b, b_b, an_b, cum_b, reverse=True)

    att = _diff_attention(z_qkv, lw["lam"], lw["sub_g"], lambda_init)
    merged = _merge(y_f, y_b, bonus, g, att, z_gates, lw["lnx_w"], lw["lnx_b"])
    x1 = _matmul_residual(merged, lw["w_out"], x, gate1)

    n_tok = bsz * t
    ne = lw["w_router_t"].shape[0]
    cap = max(1, CAPACITY_FACTOR * n_tok // ne)
    h2, aff = _router(x1, lw["norm2_g"], scale2, shift2, lw["w_router_t"])
    slots = _select(aff, cap)
    idx = (slots[:, :, 0] + slots[:, :, 1]).astype(I32).reshape(ne * cap)
    moe = _expert_ffn(idx, slots, h2.reshape(n_tok, d), lw["w_gate"], lw["w_up"], lw["w_down"])
    return _residual_out(x1, moe.reshape(bsz, t, d), gate2, final_g, final_norm)


def kernel(x_prompt, x_sample, c_prompt, c_sample, w_ada, b_ada, norm1_g, w_in, mu_shift, w0, w2, a0, a2, g2, k_k, k_a, r_k, lnx_w, lnx_b, lam_q1, lam_k1, lam_q2, lam_k2, sub_g, w_out, norm2_g, w_router, w_gate, w_up, w_down, final_g):
    depth = w_ada.shape[0]
    d = x_prompt.shape[-1]
    w3 = 3 * d
    n_lo = 2 * w2.shape[2] + 2 * a2.shape[2] + g2.shape[1]
    xs = [x_prompt, x_sample]
    n_rows = [x_prompt.shape[0], x_sample.shape[0]]
    c_all = jnp.concatenate([c_prompt, c_sample], axis=0)
    pad = (-c_all.shape[0]) % SUBLANES
    c_all = jnp.pad(c_all, ((0, pad), (0, 0)))
    for l in range(depth):
        lambda_init = 0.8 - 0.6 * math.exp(-0.3 * l)
        w_in_l = w_in[l].astype(BF16)
        lam = (jnp.exp(jnp.sum(lam_q1[l] * lam_k1[l])) - jnp.exp(jnp.sum(lam_q2[l] * lam_k2[l]))
               + lambda_init).astype(F32).reshape(1)
        lw = dict(
            norm1_g=norm1_g[l], norm2_g=norm2_g[l],
            w_rkv=w_in_l[:, :w3], w_lo=w_in_l[:, w3:w3 + n_lo],
            w_qkv=w_in_l[:, w3 + n_lo:w3 + n_lo + 3 * d], w_gates=w_in_l[:, w3 + n_lo + 3 * d:],
            mu_r=mu_shift[l, :w3], mu_l=mu_shift[l, w3:w3 + n_lo],
            w0=w0[l], w2=w2[l], a0=a0[l], a2=a2[l], g2=g2[l], k_k=k_k[l], k_a=k_a[l], r_k=r_k[l],
            lnx_w=lnx_w[l], lnx_b=lnx_b[l], lam=lam, sub_g=sub_g[l],
            w_out=w_out[l].astype(BF16), w_router_t=w_router[l].T,
            w_gate=w_gate[l], w_up=w_up[l], w_down=w_down[l])
        ada = _ada(c_all, w_ada[l], b_ada[l])
        off = 0
        for gi in range(2):
            xs[gi] = _encoder_layer(xs[gi], ada[off:off + n_rows[gi]], lw, lambda_init, final_g,
                                    final_norm=(l == depth - 1))
            off += n_rows[gi]
    return tuple(xs)
```

```python
import functools
import math

import jax
import jax.numpy as jnp
from jax import lax
from jax.experimental import pallas as pl
from jax.experimental.pallas import tpu as pltpu

F32 = jnp.float32
BF16 = jnp.bfloat16
I32 = jnp.int32

RW_HEAD = 64
DA_QK = 64
DA_V = 2 * DA_QK
RW_CHUNK = 64
CAPACITY_FACTOR = 2
EPS = 1e-6
GN_EPS = RW_HEAD * 1e-5
LANES = 128
SUBLANES = 8
VMEM_LIMIT = 56 * 1024 * 1024


def _cparams(*sem):
    return pltpu.CompilerParams(dimension_semantics=sem, vmem_limit_bytes=VMEM_LIMIT)


def _pick(n, pref, align):
    if n <= pref:
        return n
    t = (pref // align) * align
    while t >= align:
        if n % t == 0:
            return t
        t -= align
    return n


_NN = (((1,), (0,)), ((), ()))
_NT = (((1,), (1,)), ((), ()))
_TN = (((0,), (0,)), ((), ()))


def _dg(a, b, dims):
    return lax.dot_general(a, b, dims, preferred_element_type=F32)


def _dot1(a, b, dims=_NN):
    return _dg(a.astype(BF16), b.astype(BF16), dims)


def _split(x):
    hi = x.astype(BF16)
    lo = (x - hi.astype(F32)).astype(BF16)
    return hi, lo


def _dot3(a, b, dims=_NN):
    ah, al = _split(a)
    bh, bl = _split(b)
    return _dg(ah, bh, dims) + (_dg(ah, bl, dims) + _dg(al, bh, dims))


def _dot_exact_rhs(a, b_bf16, dims=_NN):
    ah, al = _split(a)
    return _dg(ah, b_bf16, dims) + _dg(al, b_bf16, dims)


def _sigmoid(x):
    return 1.0 / (1.0 + jnp.exp(-x))


def _silu(x):
    return x * _sigmoid(x)


def _ada_kernel(c_ref, w_ref, b_ref, o_ref):
    o_ref[...] = _dot1(_silu(c_ref[...]), w_ref[...]) + b_ref[...]


def _ada(c, w, b):
    m, d = c.shape
    n = w.shape[1]
    tn = _pick(n, 1024, LANES)
    return pl.pallas_call(
        _ada_kernel,
        grid=(n // tn,),
        in_specs=[pl.BlockSpec((m, d), lambda j: (0, 0)),
                  pl.BlockSpec((d, tn), lambda j: (0, j)),
                  pl.BlockSpec((1, tn), lambda j: (0, j))],
        out_specs=pl.BlockSpec((m, tn), lambda j: (0, j)),
        out_shape=jax.ShapeDtypeStruct((m, n), F32),
        compiler_params=_cparams("arbitrary"),
        name="ada",
    )(c, w, b.reshape(1, n))


def _norm_mod_kernel(x_ref, g_ref, sc_ref, sh_ref, o_ref):
    x = x_ref[0]
    y = x * lax.rsqrt(jnp.mean(x * x, axis=-1, keepdims=True) + EPS) * g_ref[...]
    o_ref[0] = (y * (1.0 + sc_ref[0]) + sh_ref[0]).astype(o_ref.dtype)


def _norm_mod(x, g, scale, shift):
    bsz, t, d = x.shape
    tt = _pick(t, 512, SUBLANES)
    row = pl.BlockSpec((1, 1, d), lambda b, i: (b, 0, 0))
    return pl.pallas_call(
        _norm_mod_kernel,
        grid=(bsz, t // tt),
        in_specs=[pl.BlockSpec((1, tt, d), lambda b, i: (b, i, 0)),
                  pl.BlockSpec((1, d), lambda b, i: (0, 0)), row, row],
        out_specs=pl.BlockSpec((1, tt, d), lambda b, i: (b, i, 0)),
        out_shape=jax.ShapeDtypeStruct((bsz, t, d), BF16),
        compiler_params=_cparams("arbitrary", "arbitrary"),
        name="norm_mod",
    )(x, g.reshape(1, d), scale, shift)


def _mm_kernel(a_ref, b_ref, o_ref):
    o_ref[...] = _dot1(a_ref[...], b_ref[...]).astype(o_ref.dtype)


def _matmul(a, b, out_dtype=F32, tm_pref=1024, tn_pref=1024):
    m, k = a.shape
    n = b.shape[1]
    tm = _pick(m, tm_pref, SUBLANES)
    tn = _pick(n, tn_pref, LANES)
    return pl.pallas_call(
        _mm_kernel,
        grid=(m // tm, n // tn),
        in_specs=[pl.BlockSpec((tm, k), lambda i, j: (i, 0)),
                  pl.BlockSpec((k, tn), lambda i, j: (0, j))],
        out_specs=pl.BlockSpec((tm, tn), lambda i, j: (i, j)),
        out_shape=jax.ShapeDtypeStruct((m, n), out_dtype),
        compiler_params=_cparams("arbitrary", "arbitrary"),
        name="matmul",
    )(a, b)


def _mm_res_kernel(a_ref, b_ref, x_ref, g_ref, o_ref):
    o_ref[0] = x_ref[0] + g_ref[0] * _dot1(a_ref[0], b_ref[...])


def _matmul_residual(a, b, x, gate):
    bsz, t, k = a.shape
    n = b.shape[1]
    tm = _pick(t, 1024, SUBLANES)
    tn = _pick(n, 1024, LANES)
    return pl.pallas_call(
        _mm_res_kernel,
        grid=(bsz, t // tm, n // tn),
        in_specs=[pl.BlockSpec((1, tm, k), lambda bb, i, j: (bb, i, 0)),
                  pl.BlockSpec((k, tn), lambda bb, i, j: (0, j)),
                  pl.BlockSpec((1, tm, tn), lambda bb, i, j: (bb, i, j)),
                  pl.BlockSpec((1, 1, tn), lambda bb, i, j: (bb, 0, j))],
        out_specs=pl.BlockSpec((1, tm, tn), lambda bb, i, j: (bb, i, j)),
        out_shape=jax.ShapeDtypeStruct((bsz, t, n), F32),
        compiler_params=_cparams("arbitrary", "arbitrary", "arbitrary"),
        name="matmul_residual",
    )(a, b, x, gate)


def _shift_tile(z, prev8, next8, mu, i, n_t):
    tt = z.shape[0]
    pr = jnp.where(i > 0, prev8[SUBLANES - 1:SUBLANES, :], 0.0)
    nx = jnp.where(i < n_t - 1, next8[0:1, :], 0.0)
    ridx = lax.broadcasted_iota(I32, z.shape, 0)
    zp = jnp.where(ridx == 0, pr, pltpu.roll(z, 1, 0))
    zn = jnp.where(ridx == tt - 1, nx, pltpu.roll(z, tt - 1, 0))
    return z + mu * (0.5 * (zp + zn) - z)


def _seg_sum(x, e_ref, et_ref):
    s = _dot_exact_rhs(x, e_ref[...])
    return _dot_exact_rhs(s, et_ref[...])


def _rw_prep_kernel(zr_ref, zrp_ref, zrn_ref, zl_ref, zlp_ref, zln_ref, mur_ref, mul_ref,
                    w0_ref, w2_ref, a0_ref, a2_ref, g2_ref, kk_ref, ka_ref, rk_ref,
                    e_ref, et_ref, lf_ref, lb_ref,
                    r_o, v_o, g_o, bonus_o,
                    kdf_o, bf_o, anf_o, cumf_o, kdb_o, bb_o, anb_o, cumb_o, *, dl, il):
    i = pl.program_id(1)
    n_t = pl.num_programs(1)
    w = r_o.shape[-1]
    zs = _shift_tile(zr_ref[0], zrp_ref[0], zrn_ref[0], mur_ref[...], i, n_t)
    zl = _shift_tile(zl_ref[0], zlp_ref[0], zln_ref[0], mul_ref[...], i, n_t)
    r = zs[:, 0:w]
    k = zs[:, w:2 * w]
    v = zs[:, 2 * w:3 * w]
    gd = zl[:, 2 * dl + 2 * il:]
    kkr = k * kk_ref[...]
    kk = kkr / jnp.maximum(jnp.sqrt(_seg_sum(kkr * kkr, e_ref, et_ref)), 1e-12)
    r_o[0] = r
    v_o[0] = v
    g_o[0] = _dot1(_sigmoid(gd), g2_ref[...])
    kd_sum = None
    outs = ((kdf_o, bf_o, anf_o, cumf_o, lf_ref), (kdb_o, bb_o, anb_o, cumb_o, lb_ref))
    for z, (kd_o, b_o, an_o, cum_o, l_ref) in enumerate(outs):
        wd = zl[:, z * dl:(z + 1) * dl]
        ad = zl[:, 2 * dl + z * il:2 * dl + (z + 1) * il]
        wl = w0_ref[z:z + 1, :] + _dot1(jnp.tanh(wd), w2_ref[z])
        u = -wl
        softplus = jnp.maximum(u, 0.0) + jnp.log(1.0 + jnp.exp(-jnp.abs(u)))
        lw = -jnp.exp(-softplus - 0.5)
        icl = _sigmoid(a0_ref[z:z + 1, :] + _dot1(ad, a2_ref[z]))
        kd = k * (1.0 + (icl - 1.0) * ka_ref[...])
        kd_o[0] = kd
        b_o[0] = kk * icl
        an_o[0] = -kk * jnp.exp(-lw)
        lh = lw.astype(BF16)
        rem = lw - lh.astype(F32)
        lm = rem.astype(BF16)
        ll = (rem - lm.astype(F32)).astype(BF16)
        lmat = l_ref[...]
        cum_o[0] = _dg(lmat, lh, _NN) + (_dg(lmat, lm, _NN) + _dg(lmat, ll, _NN))
        kd_sum = kd if kd_sum is None else kd_sum + kd
    bonus_o[0] = _seg_sum(r * kd_sum * rk_ref[...], e_ref, et_ref) * v


def _rw_prep(z_rkv, z_lo, mu_r, mu_l, w0, w2, a0, a2, g2, k_k, k_a, r_k):
    bsz, t, w3 = z_rkv.shape
    w = w3 // 3
    lw_cols = z_lo.shape[-1]
    dl, il = w2.shape[1], a2.shape[1]
    nh = w // RW_HEAD
    tt = _pick(t, 128, RW_CHUNK)
    n_t = t // tt
    hb = tt // SUBLANES
    head = jnp.arange(w, dtype=I32) // RW_HEAD
    e = (head[:, None] == jnp.arange(nh, dtype=I32)[None, :]).astype(BF16)
    et = e.T
    ti = jnp.arange(tt, dtype=I32)
    same = (ti[:, None] // RW_CHUNK) == (ti[None, :] // RW_CHUNK)
    lf = (same & (ti[None, :] <= ti[:, None])).astype(BF16)
    lb = (same & (ti[None, :] >= ti[:, None])).astype(BF16)

    def tile(c):
        return pl.BlockSpec((1, tt, c), lambda b, i: (b, i, 0))

    def prev(c):
        return pl.BlockSpec((1, SUBLANES, c), lambda b, i: (b, jnp.maximum(i * hb - 1, 0), 0))

    def nxt(c):
        return pl.BlockSpec((1, SUBLANES, c),
                            lambda b, i: (b, jnp.minimum((i + 1) * hb, t // SUBLANES - 1), 0))

    def full(a):
        nd = a.ndim
        return pl.BlockSpec(a.shape, lambda b, i: (0,) * nd)

    consts = [mu_r.reshape(1, w3), mu_l.reshape(1, lw_cols), w0, w2, a0, a2, g2,
              k_k.reshape(1, w), k_a.reshape(1, w), r_k.reshape(1, w), e, et, lf, lb]
    out = jax.ShapeDtypeStruct((bsz, t, w), F32)
    return pl.pallas_call(
        functools.partial(_rw_prep_kernel, dl=dl, il=il),
        grid=(bsz, n_t),
        in_specs=[tile(w3), prev(w3), nxt(w3), tile(lw_cols), prev(lw_cols), nxt(lw_cols)]
        + [full(a) for a in consts],
        out_specs=[tile(w)] * 12,
        out_shape=[out] * 12,
        compiler_params=_cparams("arbitrary", "arbitrary"),
        name="rw_prep",
    )(z_rkv, z_rkv, z_rkv, z_lo, z_lo, z_lo, *consts)


def _scan_kernel(r_ref, v_ref, kd_ref, b_ref, an_ref, cum_ref, y_ref, h_ref, *, reverse, n_groups,
                 group, prec):
    c = RW_CHUNK
    n = RW_HEAD

    @pl.when(pl.program_id(1) == 0)
    def _():
        h_ref[...] = jnp.zeros_like(h_ref)

    row = lax.broadcasted_iota(I32, (c, c), 0)
    col = lax.broadcasted_iota(I32, (c, c), 1)
    if reverse:
        strict, incl, last = col > row, col >= row, 0
    else:
        strict, incl, last = col < row, col <= row, c - 1
    eye = row == col
    eye_f = jnp.where(eye, 1.0, 0.0)

    dots = {1: _dot1, 3: _dot3}
    d_g, d_inv, d_av, d_z, d_o, d_st = (dots[p] for p in prec)
    hrange = range(group)

    def group_body(gi, carry):
        width = group * n
        sl = pl.ds(pl.multiple_of(gi * width, width), width)
        cum = cum_ref[0, :, sl]
        tot = cum[last:last + 1, :]
        e1 = jnp.exp(cum)
        e2 = jnp.exp(-cum)
        et = jnp.exp(tot)
        rt = r_ref[0, :, sl] * e1
        at = an_ref[0, :, sl] * e1
        bt = b_ref[0, :, sl] * e2
        kt = kd_ref[0, :, sl] * e2
        bh = bt * et
        kh = kt * et
        v2 = v_ref[0, :, sl]
        hs = h_ref[:, sl]
        hsl = [slice(n * j, n * (j + 1)) for j in hrange]
        g = [d_g(jnp.concatenate([at[:, s], rt[:, s]], axis=0),
                 jnp.concatenate([bt[:, s], kt[:, s]], axis=0), _NT) for s in hsl]
        nab = [jnp.where(strict, x[:c, :c], 0.0) for x in g]
        aak = [jnp.where(strict, x[:c, c:], 0.0) for x in g]
        mbk = [jnp.concatenate([jnp.where(incl, x[c:, :c], 0.0), jnp.where(incl, x[c:, c:], 0.0)],
                               axis=1) for x in g]
        vs = [v2[:, s] for s in hsl]
        av = [d_av(a, vv) for a, vv in zip(aak, vs)]
        tm = [eye_f + x for x in nab]
        npow = nab
        for _ in range(5):
            npow = [d_inv(x, x) for x in npow]
            tm = [t_ + d_inv(x, t_) for x, t_ in zip(npow, tm)]
        zz = [d_z(t_, jnp.concatenate([at[:, s], a], axis=1)) for t_, s, a in zip(tm, hsl, av)]
        rhs2 = [jnp.concatenate([z_, jnp.concatenate([jnp.zeros((c, n), F32), vv], axis=1)], axis=0)
                for z_, vv in zip(zz, vs)]
        o1 = [d_o(m, x) for m, x in zip(mbk, rhs2)]
        o2 = [d_o(jnp.concatenate([bh[:, s], kh[:, s]], axis=0), x, _TN)
              for s, x in zip(hsl, rhs2)]
        o3 = [d_st(jnp.concatenate([rt[:, s] + a[:, :n],
                                    b_[:, :n] + jnp.where(eye, et[:, s], 0.0)], axis=0), hs[:, s])
              for s, a, b_ in zip(hsl, o1, o2)]
        y_ref[0, :, sl] = jnp.concatenate([x[:c] + a[:, n:] for x, a in zip(o3, o1)], axis=1)
        h_ref[:, sl] = jnp.concatenate([x[c:] + b_[:, n:] for x, b_ in zip(o3, o2)], axis=1)
        return carry

    lax.fori_loop(0, n_groups, group_body, 0)


def _rw_scan(r, v, kd, b, an, cum, reverse, prec=(1, 1, 1, 1, 1, 1), group=32):
    bsz, t, w = r.shape
    nc = t // RW_CHUNK
    group = min(group, w // RW_HEAD)
    if reverse:
        imap = lambda bb, ci: (bb, nc - 1 - ci, 0)
    else:
        imap = lambda bb, ci: (bb, ci, 0)
    spec = pl.BlockSpec((1, RW_CHUNK, w), imap)
    return pl.pallas_call(
        functools.partial(_scan_kernel, reverse=reverse, n_groups=w // (group * RW_HEAD),
                          group=group, prec=prec),
        grid=(bsz, nc),
        in_specs=[spec] * 6,
        out_specs=spec,
        out_shape=jax.ShapeDtypeStruct((bsz, t, w), F32),
        scratch_shapes=[pltpu.VMEM((RW_HEAD, w), F32)],
        compiler_params=_cparams("arbitrary", "arbitrary"),
        name="rw_scan_bwd" if reverse else "rw_scan_fwd",
    )(r, v, kd, b, an, cum)


def _merge_kernel(yf_ref, yb_ref, bonus_ref, g_ref, att_ref, ga_ref, gb_ref, lw_ref, lb_ref,
                  e_ref, et_ref, o_ref):
    y = yf_ref[0] + yb_ref[0]
    mean = _seg_sum(y, e_ref, et_ref) * (1.0 / RW_HEAD)
    d = y - mean
    var = _seg_sum(d * d, e_ref, et_ref) * (1.0 / RW_HEAD)
    yn = d * lax.rsqrt(var + GN_EPS) * lw_ref[...] + lb_ref[...]
    ya = (yn + bonus_ref[0]) * g_ref[0]
    o_ref[0] = (_sigmoid(ga_ref[0]) * ya + _sigmoid(gb_ref[0]) * att_ref[0]).astype(o_ref.dtype)


def _merge(yf, yb, bonus, g, att, z_gates, lnx_w, lnx_b):
    bsz, t, w = yf.shape
    nh = w // RW_HEAD
    tt = _pick(t, 256, SUBLANES)
    head = jnp.arange(w, dtype=I32) // RW_HEAD
    e = (head[:, None] == jnp.arange(nh, dtype=I32)[None, :]).astype(BF16)
    et = e.T
    tile = pl.BlockSpec((1, tt, w), lambda b, i: (b, i, 0))
    gate_b = pl.BlockSpec((1, tt, w), lambda b, i: (b, i, 1))
    vec = pl.BlockSpec((1, w), lambda b, i: (0, 0))
    return pl.pallas_call(
        _merge_kernel,
        grid=(bsz, t // tt),
        in_specs=[tile, tile, tile, tile, tile, tile, gate_b, vec, vec,
                  pl.BlockSpec(e.shape, lambda b, i: (0, 0)),
                  pl.BlockSpec(et.shape, lambda b, i: (0, 0))],
        out_specs=tile,
        out_shape=jax.ShapeDtypeStruct((bsz, t, w), BF16),
        compiler_params=_cparams("arbitrary", "arbitrary"),
        name="merge",
    )(yf, yb, bonus, g, att, z_gates, z_gates, lnx_w.reshape(1, w), lnx_b.reshape(1, w), e, et)


LOG2E = math.log2(math.e)


MIN_ROW_SUM = 2.0 ** -60


def _attn_kernel(slope_ref, lam_ref, q_ref, k_ref, v_ref, sg_ref, d_ref, o_ref, kb_ref, vb_ref,
                 kn_ref, *, out_scale):
    h = pl.program_id(1)
    qi = pl.program_id(2)
    tq = q_ref.shape[1]
    t = k_ref.shape[1]

    @pl.when(qi == 0)
    def _():
        k = k_ref[0]
        for c in range(2):
            kc = k[:, c * DA_QK:(c + 1) * DA_QK].astype(BF16)
            kb_ref[c] = kc
            kf = kc.astype(F32)
            kn2 = jnp.max(jnp.sum(kf * kf, axis=1, keepdims=True), axis=0, keepdims=True)
            kn_ref[c] = jnp.broadcast_to(kn2, kn_ref.shape[1:])
        vb_ref[...] = v_ref[0].astype(BF16)

    start = pl.multiple_of((pl.num_programs(2) - 1 - qi) * tq, tq)
    bias = (-LOG2E * slope_ref[h]) * d_ref[:, pl.ds(start, t)]
    q = (q_ref[0] * (DA_QK ** -0.5 * LOG2E)).astype(BF16)
    qf = q.astype(F32)

    def softmax_pv(row_shift):
        outs, sums = [], []
        for c in range(2):
            cs = slice(c * DA_QK, (c + 1) * DA_QK)
            s = _dg(q[:, cs], kb_ref[c], _NT) + bias
            p = jnp.exp2(s - row_shift(c, s))
            sums.append(jnp.sum(p, axis=1, keepdims=True))
            outs.append(_dg(p.astype(BF16), vb_ref[...], _NN))
        return outs, sums

    def finish(outs, sums):
        o = outs[0] / sums[0] - lam_ref[0] * (outs[1] / sums[1])
        o = o * lax.rsqrt(jnp.mean(o * o, axis=-1, keepdims=True) + EPS) * sg_ref[...]
        o_ref[0] = o * out_scale

    def norm_bound(c, s):
        qc = qf[:, c * DA_QK:(c + 1) * DA_QK]
        qn2 = jnp.sum(qc * qc, axis=1, keepdims=True)
        return jnp.sqrt(qn2 * kn_ref[c][0:1, 0:1]) * (1.0 + 2.0 ** -7)

    outs, sums = softmax_pv(norm_bound)
    safe = jnp.minimum(jnp.min(sums[0]), jnp.min(sums[1])) >= MIN_ROW_SUM

    @pl.when(safe)
    def _():
        finish(outs, sums)

    @pl.when(jnp.logical_not(safe))
    def _():
        finish(*softmax_pv(lambda c, s: jnp.max(s, axis=1, keepdims=True)))


def _diff_attention(z_qkv, lam, sub_g, lambda_init):
    bsz, t, w3 = z_qkv.shape
    nh = w3 // 3 // DA_V
    tq = _pick(t, 256, LANES)
    slopes = 2.0 ** (-8.0 * jnp.arange(1, nh + 1, dtype=F32) / nh)
    dist = jnp.abs(jnp.arange(tq, dtype=I32)[:, None] + (t - tq)
                   - jnp.arange(2 * t - tq, dtype=I32)[None, :]).astype(F32)
    smem = pl.BlockSpec(memory_space=pltpu.SMEM)
    return pl.pallas_call(
        functools.partial(_attn_kernel, out_scale=1.0 - lambda_init),
        grid=(bsz, nh, t // tq),
        in_specs=[smem, smem,
                  pl.BlockSpec((1, tq, DA_V), lambda b, h, i: (b, i, h)),
                  pl.BlockSpec((1, t, DA_V), lambda b, h, i: (b, 0, nh + h)),
                  pl.BlockSpec((1, t, DA_V), lambda b, h, i: (b, 0, 2 * nh + h)),
                  pl.BlockSpec((1, DA_V), lambda b, h, i: (0, 0)),
                  pl.BlockSpec(dist.shape, lambda b, h, i: (0, 0), pipeline_mode=pl.Buffered(1))],
        out_specs=pl.BlockSpec((1, tq, DA_V), lambda b, h, i: (b, i, h)),
        out_shape=jax.ShapeDtypeStruct((bsz, t, nh * DA_V), F32),
        scratch_shapes=[pltpu.VMEM((2, t, DA_QK), BF16), pltpu.VMEM((t, DA_V), BF16),
                        pltpu.VMEM((2, SUBLANES, LANES), F32)],
        compiler_params=_cparams("arbitrary", "arbitrary", "arbitrary"),
        name="diff_attention",
    )(slopes, lam, z_qkv, z_qkv, z_qkv, sub_g.reshape(1, DA_V), dist)


def _router_kernel(x_ref, g_ref, sc_ref, sh_ref, wr_ref, h_ref, aff_ref):
    x = x_ref[0]
    y = x * lax.rsqrt(jnp.mean(x * x, axis=-1, keepdims=True) + EPS) * g_ref[...]
    h = y * (1.0 + sc_ref[0]) + sh_ref[0]
    h_ref[0] = h.astype(h_ref.dtype)
    logits = _dot3(wr_ref[...], h, _NT)
    m = jnp.max(logits, axis=0, keepdims=True)
    ex = jnp.exp(logits - m)
    aff = ex / jnp.sum(ex, axis=0, keepdims=True)
    for j in range(aff_ref.shape[1]):
        aff_ref[:, j, :] = aff[:, j * LANES:(j + 1) * LANES]


def _router(x, g, scale, shift, w_router_t):
    bsz, t, d = x.shape
    ne = w_router_t.shape[0]
    tt = _pick(t, SUBLANES * LANES, SUBLANES * LANES)
    nb = tt // LANES
    n_t = t // tt
    row = pl.BlockSpec((1, 1, d), lambda b, i: (b, 0, 0))
    return pl.pallas_call(
        _router_kernel,
        grid=(bsz, n_t),
        in_specs=[pl.BlockSpec((1, tt, d), lambda b, i: (b, i, 0)),
                  pl.BlockSpec((1, d), lambda b, i: (0, 0)), row, row,
                  pl.BlockSpec((ne, d), lambda b, i: (0, 0))],
        out_specs=[pl.BlockSpec((1, tt, d), lambda b, i: (b, i, 0)),
                   pl.BlockSpec((ne, nb, LANES), lambda b, i: (0, b * n_t + i, 0))],
        out_shape=[jax.ShapeDtypeStruct((bsz, t, d), F32),
                   jax.ShapeDtypeStruct((ne, bsz * t // LANES, LANES), F32)],
        compiler_params=_cparams("arbitrary", "arbitrary"),
        name="router",
    )(x, g.reshape(1, d), scale, shift, w_router_t)


def _sum_all(x):
    return jnp.sum(jnp.sum(x, axis=1, keepdims=True), axis=0, keepdims=True)


SLOT_ROWS = 8


def _select_kernel(aff_ref, slots_ref, pos_ref, *, cap):
    nb = aff_ref.shape[1]
    li = lax.broadcasted_iota(I32, (LANES, LANES), 0)
    lj = lax.broadcasted_iota(I32, (LANES, LANES), 1)
    upper = jnp.where(li <= lj, 1.0, 0.0).astype(BF16)
    ri = lax.broadcasted_iota(I32, (nb, nb), 0)
    rj = lax.broadcasted_iota(I32, (nb, nb), 1)
    lower = jnp.where(rj < ri, 1.0, 0.0).astype(BF16)

    def excl_prefix(mask):
        x = jnp.where(mask, 1.0, 0.0)
        inc = _dg(x.astype(BF16), upper, _NN)
        tot = jnp.broadcast_to(inc[:, LANES - 1:LANES], (nb, LANES))
        offs = _dg(lower, tot.astype(BF16), _NN)
        return offs + inc - x

    def expert_body(e, carry):
        bits = pltpu.bitcast(aff_ref[e], I32)

        def bit_body(i, thr):
            cand = thr | jnp.left_shift(jnp.int32(1), 30 - i)
            cnt = _sum_all(jnp.where(bits >= cand, 1.0, 0.0))
            return jnp.where(cnt >= cap, cand, thr)

        thr = lax.fori_loop(0, 31, bit_body, jnp.zeros((1, 1), I32))
        gt = bits > thr
        eq = bits == thr
        need = cap - _sum_all(jnp.where(gt, 1.0, 0.0))
        sel = gt | (eq & (excl_prefix(eq) < need))
        pos_ref[...] = jnp.where(sel, excl_prefix(sel), -1.0).astype(I32)
        slot_ids = lax.broadcasted_iota(I32, (cap, LANES), 0)
        lane_f = lax.broadcasted_iota(I32, (1, LANES), 1).astype(F32)

        def group_body(g, rec_acc):
            a = aff_ref[e, pl.ds(g, 1), :]
            a_hi = a.astype(BF16).astype(F32)
            a_mid = (a - a_hi).astype(BF16).astype(F32)
            a_lo = a - a_hi - a_mid
            base = jnp.full((1, LANES), LANES, F32) * g
            rec = jnp.concatenate([lane_f, base, a_hi, a_mid, a_lo,
                                   jnp.zeros((SLOT_ROWS - 5, LANES), F32)], axis=0)
            hot = jnp.where(slot_ids == pos_ref[pl.ds(g, 1), :], 1.0, 0.0)
            return rec_acc + _dot1(rec, hot, _NT)

        rec = lax.fori_loop(0, nb, group_body, jnp.zeros((SLOT_ROWS, cap), F32))
        rid = lax.broadcasted_iota(I32, (SLOT_ROWS, cap), 0)
        token = rec[0:1] + rec[1:2]
        gate = rec[2:3] + rec[3:4] + rec[4:5]
        slots_ref[e] = jnp.where(rid == 0, token, jnp.where(rid == 1, gate, 0.0))
        return carry

    lax.fori_loop(0, aff_ref.shape[0], expert_body, 0)


def _select(aff, cap):
    ne, nb, _ = aff.shape
    return pl.pallas_call(
        functools.partial(_select_kernel, cap=cap),
        grid=(1,),
        in_specs=[pl.BlockSpec(aff.shape, lambda i: (0, 0, 0))],
        out_specs=pl.BlockSpec((ne, SLOT_ROWS, cap), lambda i: (0, 0, 0)),
        out_shape=jax.ShapeDtypeStruct((ne, SLOT_ROWS, cap), F32),
        scratch_shapes=[pltpu.VMEM((nb, LANES), I32)],
        compiler_params=_cparams("arbitrary"),
        name="select",
    )(aff)


def _ffn_kernel(idx_ref, h_hbm, wg_ref, wu_ref, wd_ref, gate_ref, moe_in_hbm, moe_hbm,
                xf_ref, xb_ref, acc_ref, rows_ref, sems):
    del moe_in_hbm
    e = pl.program_id(0)
    f = pl.program_id(1)
    n_e = pl.num_programs(0)
    n_f = pl.num_programs(1)
    cap = xf_ref.shape[0]

    def x_copy(ex, p):
        row = idx_ref[ex * cap + p]
        return pltpu.make_async_copy(h_hbm.at[pl.ds(row, 1)], xf_ref.at[pl.ds(p, 1)], sems.at[0])

    def acc_copy(p):
        row = idx_ref[e * cap + p]
        return pltpu.make_async_copy(moe_hbm.at[pl.ds(row, 1)], rows_ref.at[pl.ds(p, 1)], sems.at[1])

    def out_copy(ex, p):
        row = idx_ref[ex * cap + p]
        return pltpu.make_async_copy(rows_ref.at[pl.ds(p, 1)], moe_hbm.at[pl.ds(row, 1)], sems.at[2])

    def for_slots(fn):
        def body(p, c):
            fn(p)
            return c
        lax.fori_loop(0, cap, body, 0, unroll=8)

    @pl.when(f == 0)
    def _():
        @pl.when(e == 0)
        def _():
            for_slots(lambda p: x_copy(e, p).start())

        for_slots(lambda p: x_copy(e, p).wait())
        xb_ref[...] = xf_ref[...].astype(BF16)

        @pl.when(e + 1 < n_e)
        def _():
            for_slots(lambda p: x_copy(e + 1, p).start())

        acc_ref[...] = jnp.zeros_like(acc_ref)

    xe = xb_ref[...]
    hid = _silu(_dot1(xe, wg_ref[0])) * _dot1(xe, wu_ref[0])
    acc_ref[...] += _dot1(hid, wd_ref[0])

    @pl.when(f == n_f // 2)
    def _():
        @pl.when(e > 0)
        def _():
            for_slots(lambda p: out_copy(e - 1, p).wait())

        for_slots(lambda p: acc_copy(p).start())

    @pl.when(f == n_f - 1)
    def _():
        for_slots(lambda p: acc_copy(p).wait())
        rows_ref[...] += acc_ref[...] * gate_ref[0]
        for_slots(lambda p: out_copy(e, p).start())

        @pl.when(e == n_e - 1)
        def _():
            for_slots(lambda p: out_copy(e, p).wait())


def _expert_ffn(idx, gate, h2, w_gate, w_up, w_down):
    ne, cap, _ = gate.shape
    n_tok, d = h2.shape
    ff = w_gate.shape[-1]
    tf = _pick(ff, 256, LANES)
    n_f = ff // tf
    assert n_f >= 2, "accumulator rows are fetched mid-way through an expert's steps"
    any_spec = pl.BlockSpec(memory_space=pl.ANY)
    grid_spec = pltpu.PrefetchScalarGridSpec(
        num_scalar_prefetch=1,
        grid=(ne, n_f),
        in_specs=[any_spec,
                  pl.BlockSpec((1, d, tf), lambda e, f, idx_ref: (e, 0, f)),
                  pl.BlockSpec((1, d, tf), lambda e, f, idx_ref: (e, 0, f)),
                  pl.BlockSpec((1, tf, d), lambda e, f, idx_ref: (e, f, 0)),
                  pl.BlockSpec((1, cap, 1), lambda e, f, idx_ref: (e, 0, 0)),
                  any_spec],
        out_specs=any_spec,
        scratch_shapes=[pltpu.VMEM((cap, d), F32), pltpu.VMEM((cap, d), BF16),
                        pltpu.VMEM((cap, d), F32), pltpu.VMEM((cap, d), F32),
                        pltpu.SemaphoreType.DMA((3,))])
    return pl.pallas_call(
        _ffn_kernel,
        grid_spec=grid_spec,
        out_shape=jax.ShapeDtypeStruct((n_tok, d), F32),
        input_output_aliases={6: 0},
        compiler_params=_cparams("arbitrary", "arbitrary"),
        name="expert_ffn",
    )(idx, h2, w_gate, w_up, w_down, gate, jnp.zeros((n_tok, d), F32))


def _residual_out_kernel(x_ref, m_ref, g_ref, fg_ref, o_ref, *, final_norm):
    x = x_ref[0] + g_ref[0] * m_ref[0]
    if final_norm:
        x = x * lax.rsqrt(jnp.mean(x * x, axis=-1, keepdims=True) + EPS) * fg_ref[...]
    o_ref[0] = x


def _residual_out(x, moe, gate2, final_g, final_norm):
    bsz, t, d = x.shape
    tt = _pick(t, 512, SUBLANES)
    tile = pl.BlockSpec((1, tt, d), lambda b, i: (b, i, 0))
    return pl.pallas_call(
        functools.partial(_residual_out_kernel, final_norm=final_norm),
        grid=(bsz, t // tt),
        in_specs=[tile, tile, pl.BlockSpec((1, 1, d), lambda b, i: (b, 0, 0)),
                  pl.BlockSpec((1, d), lambda b, i: (0, 0))],
        out_specs=tile,
        out_shape=jax.ShapeDtypeStruct((bsz, t, d), F32),
        compiler_params=_cparams("arbitrary", "arbitrary"),
        name="residual_out",
    )(x, moe, gate2, final_g.reshape(1, d))


def _encoder_layer(x, ada, lw, lambda_init, final_g, final_norm):
    bsz, t, d = x.shape
    shift1, scale1, gate1, shift2, scale2, gate2 = [a[:, None, :] for a in jnp.split(ada, 6, axis=-1)]

    h = _norm_mod(x, lw["norm1_g"], scale1, shift1).reshape(bsz * t, d)
    z_rkv = _matmul(h, lw["w_rkv"]).reshape(bsz, t, -1)
    z_lo = _matmul(h, lw["w_lo"]).reshape(bsz, t, -1)
    z_qkv = _matmul(h, lw["w_qkv"]).reshape(bsz, t, -1)
    z_gates = _matmul(h, lw["w_gates"]).reshape(bsz, t, -1)

    (r, v, g, bonus, kd_f, b_f, an_f, cum_f, kd_b, b_b, an_b, cum_b) = _rw_prep(
        z_rkv, z_lo, lw["mu_r"], lw["mu_l"], lw["w0"], lw["w2"], lw["a0"], lw["a2"], lw["g2"],
        lw["k_k"], lw["k_a"], lw["r_k"])
    y_f = _rw_scan(r, v, kd_f, b_f, an_f, cum_f, reverse=False)
    y_b = _rw_scan(r, v, kd_b, b_b, an_b, cum_b, reverse=True)

    att = _diff_attention(z_qkv, lw["lam"], lw["sub_g"], lambda_init)
    merged = _merge(y_f, y_b, bonus, g, att, z_gates, lw["lnx_w"], lw["lnx_b"])
    x1 = _matmul_residual(merged, lw["w_out"], x, gate1)

    n_tok = bsz * t
    ne = lw["w_router_t"].shape[0]
    cap = max(1, CAPACITY_FACTOR * n_tok // ne)
    h2, aff = _router(x1, lw["norm2_g"], scale2, shift2, lw["w_router_t"])
    slots = _select(aff, cap)
    idx = slots[:, 0, :].astype(I32).reshape(ne * cap)
    gate = slots[:, 1, :].reshape(ne, cap, 1)
    moe = _expert_ffn(idx, gate, h2.reshape(n_tok, d), lw["w_gate"], lw["w_up"], lw["w_down"])
    return _residual_out(x1, moe.reshape(bsz, t, d), gate2, final_g, final_norm)


def kernel(x_prompt, x_sample, c_prompt, c_sample, w_ada, b_ada, norm1_g, w_in, mu_shift, w0, w2, a0, a2, g2, k_k, k_a, r_k, lnx_w, lnx_b, lam_q1, lam_k1, lam_q2, lam_k2, sub_g, w_out, norm2_g, w_router, w_gate, w_up, w_down, final_g):
    depth = w_ada.shape[0]
    d = x_prompt.shape[-1]
    w3 = 3 * d
    n_lo = 2 * w2.shape[2] + 2 * a2.shape[2] + g2.shape[1]
    xs = [x_prompt, x_sample]
    n_rows = [x_prompt.shape[0], x_sample.shape[0]]
    c_all = jnp.concatenate([c_prompt, c_sample], axis=0)
    pad = (-c_all.shape[0]) % SUBLANES
    c_all = jnp.pad(c_all, ((0, pad), (0, 0)))
    for l in range(depth):
        lambda_init = 0.8 - 0.6 * math.exp(-0.3 * l)
        w_in_l = w_in[l].astype(BF16)
        lam = (jnp.exp(jnp.sum(lam_q1[l] * lam_k1[l])) - jnp.exp(jnp.sum(lam_q2[l] * lam_k2[l]))
               + lambda_init).astype(F32).reshape(1)
        lw = dict(
            norm1_g=norm1_g[l], norm2_g=norm2_g[l],
            w_rkv=w_in_l[:, :w3], w_lo=w_in_l[:, w3:w3 + n_lo],
            w_qkv=w_in_l[:, w3 + n_lo:w3 + n_lo + 3 * d], w_gates=w_in_l[:, w3 + n_lo + 3 * d:],
            mu_r=mu_shift[l, :w3], mu_l=mu_shift[l, w3:w3 + n_lo],
            w0=w0[l], w2=w2[l], a0=a0[l], a2=a2[l], g2=g2[l], k_k=k_k[l], k_a=k_a[l], r_k=r_k[l],
            lnx_w=lnx_w[l], lnx_b=lnx_b[l], lam=lam, sub_g=sub_g[l],
            w_out=w_out[l].astype(BF16), w_router_t=w_router[l].T,
            w_gate=w_gate[l], w_up=w_up[l], w_down=w_down[l])
        ada = _ada(c_all, w_ada[l], b_ada[l])
        off = 0
        for gi in range(2):
            xs[gi] = _encoder_layer(xs[gi], ada[off:off + n_rows[gi]], lw, lambda_init, final_g,
                                    final_norm=(l == depth - 1))
            off += n_rows[gi]
    return tuple(xs)
```

```python
import functools
import math

import jax
import jax.numpy as jnp
from jax import lax
from jax.experimental import pallas as pl
from jax.experimental.pallas import tpu as pltpu

F32 = jnp.float32
BF16 = jnp.bfloat16
I32 = jnp.int32

RW_HEAD = 64
DA_QK = 64
DA_V = 2 * DA_QK
RW_CHUNK = 64
CAPACITY_FACTOR = 2
EPS = 1e-6
GN_EPS = RW_HEAD * 1e-5
LANES = 128
SUBLANES = 8
VMEM_LIMIT = 56 * 1024 * 1024


def _cparams(*sem):
    return pltpu.CompilerParams(dimension_semantics=sem, vmem_limit_bytes=VMEM_LIMIT)


def _pick(n, pref, align):
    if n <= pref:
        return n
    t = (pref // align) * align
    while t >= align:
        if n % t == 0:
            return t
        t -= align
    return n


_NN = (((1,), (0,)), ((), ()))
_NT = (((1,), (1,)), ((), ()))
_TN = (((0,), (0,)), ((), ()))


def _dg(a, b, dims):
    return lax.dot_general(a, b, dims, preferred_element_type=F32)


def _dot1(a, b, dims=_NN):
    return _dg(a.astype(BF16), b.astype(BF16), dims)


def _split(x):
    hi = x.astype(BF16)
    lo = (x - hi.astype(F32)).astype(BF16)
    return hi, lo


def _dot3(a, b, dims=_NN):
    ah, al = _split(a)
    bh, bl = _split(b)
    return _dg(ah, bh, dims) + (_dg(ah, bl, dims) + _dg(al, bh, dims))


def _dot_exact_rhs(a, b_bf16, dims=_NN):
    ah, al = _split(a)
    return _dg(ah, b_bf16, dims) + _dg(al, b_bf16, dims)


def _sigmoid(x):
    return 1.0 / (1.0 + jnp.exp(-x))


def _silu(x):
    return x * _sigmoid(x)


def _ada_kernel(c_ref, w_ref, b_ref, o_ref):
    o_ref[...] = _dot1(_silu(c_ref[...]), w_ref[...]) + b_ref[...]


def _ada(c, w, b):
    m, d = c.shape
    n = w.shape[1]
    tn = _pick(n, 1024, LANES)
    return pl.pallas_call(
        _ada_kernel,
        grid=(n // tn,),
        in_specs=[pl.BlockSpec((m, d), lambda j: (0, 0)),
                  pl.BlockSpec((d, tn), lambda j: (0, j)),
                  pl.BlockSpec((1, tn), lambda j: (0, j))],
        out_specs=pl.BlockSpec((m, tn), lambda j: (0, j)),
        out_shape=jax.ShapeDtypeStruct((m, n), F32),
        compiler_params=_cparams("arbitrary"),
        name="ada",
    )(c, w, b.reshape(1, n))


def _norm_mod_kernel(x_ref, g_ref, sc_ref, sh_ref, o_ref):
    x = x_ref[0]
    y = x * lax.rsqrt(jnp.mean(x * x, axis=-1, keepdims=True) + EPS) * g_ref[...]
    o_ref[0] = (y * (1.0 + sc_ref[0]) + sh_ref[0]).astype(o_ref.dtype)


def _norm_mod(x, g, scale, shift):
    bsz, t, d = x.shape
    tt = _pick(t, 512, SUBLANES)
    row = pl.BlockSpec((1, 1, d), lambda b, i: (b, 0, 0))
    return pl.pallas_call(
        _norm_mod_kernel,
        grid=(bsz, t // tt),
        in_specs=[pl.BlockSpec((1, tt, d), lambda b, i: (b, i, 0)),
                  pl.BlockSpec((1, d), lambda b, i: (0, 0)), row, row],
        out_specs=pl.BlockSpec((1, tt, d), lambda b, i: (b, i, 0)),
        out_shape=jax.ShapeDtypeStruct((bsz, t, d), BF16),
        compiler_params=_cparams("arbitrary", "arbitrary"),
        name="norm_mod",
    )(x, g.reshape(1, d), scale, shift)


def _mm_kernel(a_ref, b_ref, o_ref):
    o_ref[...] = _dot1(a_ref[...], b_ref[...]).astype(o_ref.dtype)


def _matmul(a, b, out_dtype=F32, tm_pref=1024, tn_pref=1024):
    m, k = a.shape
    n = b.shape[1]
    tm = _pick(m, tm_pref, SUBLANES)
    tn = _pick(n, tn_pref, LANES)
    return pl.pallas_call(
        _mm_kernel,
        grid=(m // tm, n // tn),
        in_specs=[pl.BlockSpec((tm, k), lambda i, j: (i, 0)),
                  pl.BlockSpec((k, tn), lambda i, j: (0, j))],
        out_specs=pl.BlockSpec((tm, tn), lambda i, j: (i, j)),
        out_shape=jax.ShapeDtypeStruct((m, n), out_dtype),
        compiler_params=_cparams("arbitrary", "arbitrary"),
        name="matmul",
    )(a, b)


def _mm_res_kernel(a_ref, b_ref, x_ref, g_ref, o_ref):
    o_ref[0] = x_ref[0] + g_ref[0] * _dot1(a_ref[0], b_ref[...])


def _matmul_residual(a, b, x, gate):
    bsz, t, k = a.shape
    n = b.shape[1]
    tm = _pick(t, 1024, SUBLANES)
    tn = _pick(n, 1024, LANES)
    return pl.pallas_call(
        _mm_res_kernel,
        grid=(bsz, t // tm, n // tn),
        in_specs=[pl.BlockSpec((1, tm, k), lambda bb, i, j: (bb, i, 0)),
                  pl.BlockSpec((k, tn), lambda bb, i, j: (0, j)),
                  pl.BlockSpec((1, tm, tn), lambda bb, i, j: (bb, i, j)),
                  pl.BlockSpec((1, 1, tn), lambda bb, i, j: (bb, 0, j))],
        out_specs=pl.BlockSpec((1, tm, tn), lambda bb, i, j: (bb, i, j)),
        out_shape=jax.ShapeDtypeStruct((bsz, t, n), F32),
        compiler_params=_cparams("arbitrary", "arbitrary", "arbitrary"),
        name="matmul_residual",
    )(a, b, x, gate)


def _shift_tile(z, prev8, next8, mu, i, n_t):
    tt = z.shape[0]
    pr = jnp.where(i > 0, prev8[SUBLANES - 1:SUBLANES, :], 0.0)
    nx = jnp.where(i < n_t - 1, next8[0:1, :], 0.0)
    ridx = lax.broadcasted_iota(I32, z.shape, 0)
    zp = jnp.where(ridx == 0, pr, pltpu.roll(z, 1, 0))
    zn = jnp.where(ridx == tt - 1, nx, pltpu.roll(z, tt - 1, 0))
    return z + mu * (0.5 * (zp + zn) - z)


def _seg_sum(x, e_ref, et_ref):
    s = _dot_exact_rhs(x, e_ref[...])
    return _dot_exact_rhs(s, et_ref[...])


def _rw_prep_kernel(zr_ref, zrp_ref, zrn_ref, zl_ref, zlp_ref, zln_ref, mur_ref, mul_ref,
                    w0_ref, w2_ref, a0_ref, a2_ref, g2_ref, kk_ref, ka_ref, rk_ref,
                    e_ref, et_ref, lf_ref, lb_ref,
                    r_o, v_o, g_o, bonus_o,
                    kdf_o, bf_o, anf_o, cumf_o, kdb_o, bb_o, anb_o, cumb_o, *, dl, il):
    i = pl.program_id(1)
    n_t = pl.num_programs(1)
    w = r_o.shape[-1]
    zs = _shift_tile(zr_ref[0], zrp_ref[0], zrn_ref[0], mur_ref[...], i, n_t)
    zl = _shift_tile(zl_ref[0], zlp_ref[0], zln_ref[0], mul_ref[...], i, n_t)
    r = zs[:, 0:w]
    k = zs[:, w:2 * w]
    v = zs[:, 2 * w:3 * w]
    gd = zl[:, 2 * dl + 2 * il:]
    kkr = k * kk_ref[...]
    kk = kkr / jnp.maximum(jnp.sqrt(_seg_sum(kkr * kkr, e_ref, et_ref)), 1e-12)
    r_o[0] = r
    v_o[0] = v
    g_o[0] = _dot1(_sigmoid(gd), g2_ref[...])
    kd_sum = None
    outs = ((kdf_o, bf_o, anf_o, cumf_o, lf_ref), (kdb_o, bb_o, anb_o, cumb_o, lb_ref))
    for z, (kd_o, b_o, an_o, cum_o, l_ref) in enumerate(outs):
        wd = zl[:, z * dl:(z + 1) * dl]
        ad = zl[:, 2 * dl + z * il:2 * dl + (z + 1) * il]
        wl = w0_ref[z:z + 1, :] + _dot1(jnp.tanh(wd), w2_ref[z])
        u = -wl
        softplus = jnp.maximum(u, 0.0) + jnp.log(1.0 + jnp.exp(-jnp.abs(u)))
        lw = -jnp.exp(-softplus - 0.5)
        icl = _sigmoid(a0_ref[z:z + 1, :] + _dot1(ad, a2_ref[z]))
        kd = k * (1.0 + (icl - 1.0) * ka_ref[...])
        kd_o[0] = kd
        b_o[0] = kk * icl
        an_o[0] = -kk * jnp.exp(-lw)
        lh = lw.astype(BF16)
        rem = lw - lh.astype(F32)
        lm = rem.astype(BF16)
        ll = (rem - lm.astype(F32)).astype(BF16)
        lmat = l_ref[...]
        cum_o[0] = _dg(lmat, lh, _NN) + (_dg(lmat, lm, _NN) + _dg(lmat, ll, _NN))
        kd_sum = kd if kd_sum is None else kd_sum + kd
    bonus_o[0] = _seg_sum(r * kd_sum * rk_ref[...], e_ref, et_ref) * v


def _rw_prep(z_rkv, z_lo, mu_r, mu_l, w0, w2, a0, a2, g2, k_k, k_a, r_k):
    bsz, t, w3 = z_rkv.shape
    w = w3 // 3
    lw_cols = z_lo.shape[-1]
    dl, il = w2.shape[1], a2.shape[1]
    nh = w // RW_HEAD
    tt = _pick(t, 128, RW_CHUNK)
    n_t = t // tt
    hb = tt // SUBLANES
    head = jnp.arange(w, dtype=I32) // RW_HEAD
    e = (head[:, None] == jnp.arange(nh, dtype=I32)[None, :]).astype(BF16)
    et = e.T
    ti = jnp.arange(tt, dtype=I32)
    same = (ti[:, None] // RW_CHUNK) == (ti[None, :] // RW_CHUNK)
    lf = (same & (ti[None, :] <= ti[:, None])).astype(BF16)
    lb = (same & (ti[None, :] >= ti[:, None])).astype(BF16)

    def tile(c):
        return pl.BlockSpec((1, tt, c), lambda b, i: (b, i, 0))

    def prev(c):
        return pl.BlockSpec((1, SUBLANES, c), lambda b, i: (b, jnp.maximum(i * hb - 1, 0), 0))

    def nxt(c):
        return pl.BlockSpec((1, SUBLANES, c),
                            lambda b, i: (b, jnp.minimum((i + 1) * hb, t // SUBLANES - 1), 0))

    def full(a):
        nd = a.ndim
        return pl.BlockSpec(a.shape, lambda b, i: (0,) * nd)

    consts = [mu_r.reshape(1, w3), mu_l.reshape(1, lw_cols), w0, w2, a0, a2, g2,
              k_k.reshape(1, w), k_a.reshape(1, w), r_k.reshape(1, w), e, et, lf, lb]
    out = jax.ShapeDtypeStruct((bsz, t, w), F32)
    return pl.pallas_call(
        functools.partial(_rw_prep_kernel, dl=dl, il=il),
        grid=(bsz, n_t),
        in_specs=[tile(w3), prev(w3), nxt(w3), tile(lw_cols), prev(lw_cols), nxt(lw_cols)]
        + [full(a) for a in consts],
        out_specs=[tile(w)] * 12,
        out_shape=[out] * 12,
        compiler_params=_cparams("arbitrary", "arbitrary"),
        name="rw_prep",
    )(z_rkv, z_rkv, z_rkv, z_lo, z_lo, z_lo, *consts)


QUAD = 4


def _scan_quad_kernel(r_ref, v_ref, kd_ref, b_ref, an_ref, cum_ref, y_ref, h_ref, *, reverse):
    c = RW_CHUNK
    n = RW_HEAD
    qw = QUAD * n
    nq = h_ref.shape[1] // qw

    @pl.when(pl.program_id(1) == 0)
    def _():
        h_ref[...] = jnp.zeros_like(h_ref)

    row = lax.broadcasted_iota(I32, (c, qw), 0)
    col = lax.broadcasted_iota(I32, (c, qw), 1) % n
    if reverse:
        strict, incl, last = col > row, col >= row, 0
    else:
        strict, incl, last = col < row, col <= row, c - 1
    eye = row == col
    eye_f = jnp.where(eye, 1.0, 0.0)
    bd_mask = (lax.broadcasted_iota(I32, (QUAD * c, qw), 0) // c
               == lax.broadcasted_iota(I32, (QUAD * c, qw), 1) // n)
    first_of_pair = lax.broadcasted_iota(I32, (c, 2 * n), 1) < n

    def bd(x):
        xb = x.astype(BF16)
        return jnp.where(bd_mask, jnp.concatenate([xb] * QUAD, axis=0), jnp.zeros((), BF16))

    def mm(a, b_bf16, dims=_NN):
        return _dg(a.astype(BF16), b_bf16, dims)

    qs = [slice(qw * q, qw * (q + 1)) for q in range(nq)]
    cum = [cum_ref[0, :, s] for s in qs]
    e1 = [jnp.exp(x) for x in cum]
    e2 = [jnp.exp(-x) for x in cum]
    et = [jnp.exp(x[last:last + 1, :]) for x in cum]
    rt = [r_ref[0, :, s] * e for s, e in zip(qs, e1)]
    at = [an_ref[0, :, s] * e for s, e in zip(qs, e1)]
    bt = [b_ref[0, :, s] * e for s, e in zip(qs, e2)]
    kt = [kd_ref[0, :, s] * e for s, e in zip(qs, e2)]
    bh = [x * e for x, e in zip(bt, et)]
    kh = [x * e for x, e in zip(kt, et)]
    vs = [v_ref[0, :, s] for s in qs]
    hs = [h_ref[:, s] for s in qs]
    g = [mm(jnp.concatenate([a, r_], axis=0), jnp.concatenate([bd(b_), bd(k_)], axis=0), _NT)
         for a, r_, b_, k_ in zip(at, rt, bt, kt)]
    nab = [jnp.where(strict, x[:c, :qw], 0.0) for x in g]
    aak = [jnp.where(strict, x[:c, qw:], 0.0) for x in g]
    mb = [jnp.where(incl, x[c:, :qw], 0.0) for x in g]
    mk = [jnp.where(incl, x[c:, qw:], 0.0) for x in g]
    bdv = [bd(x) for x in vs]
    av = [mm(a, x) for a, x in zip(aak, bdv)]
    tm = [eye_f + x for x in nab]
    npow = nab
    for _ in range(5):
        npow = [mm(x, bd(x)) for x in npow]
        tm = [t_ + mm(x, bd(t_)) for x, t_ in zip(npow, tm)]
    za = [mm(t_, bd(a)) for t_, a in zip(tm, at)]
    zw = [mm(t_, bd(a)) for t_, a in zip(tm, av)]
    rp = [r_ + mm(m, bd(z)) for r_, m, z in zip(rt, mb, za)]
    yi = [mm(m, bd(z)) + mm(k_, x) for m, z, k_, x in zip(mb, zw, mk, bdv)]
    pm, qq = [], []
    for q in range(nq):
        pms, qqs = [], []
        for pr in range(QUAD // 2):
            ps = slice(2 * n * pr, 2 * n * (pr + 1))
            lhs_t = jnp.concatenate([bh[q][:, ps], kh[q][:, ps]], axis=0)
            rhs = jnp.concatenate(
                [jnp.concatenate([za[q][:, ps], zw[q][:, ps]], axis=1),
                 jnp.concatenate([jnp.zeros((c, 2 * n), F32), vs[q][:, ps]], axis=1)], axis=0)
            m = _dot1(lhs_t, rhs, _TN)
            pms.append(jnp.where(first_of_pair, m[:n, :2 * n], m[n:, :2 * n]))
            qqs.append(jnp.where(first_of_pair, m[:n, 2 * n:], m[n:, 2 * n:]))
        pm.append(jnp.concatenate(pms, axis=1) + jnp.where(eye, et[q], 0.0))
        qq.append(jnp.concatenate(qqs, axis=1))
    o3 = [mm(jnp.concatenate([r_, p_], axis=0), bd(h_)) for r_, p_, h_ in zip(rp, pm, hs)]
    for q, s in enumerate(qs):
        y_ref[0, :, s] = o3[q][:c] + yi[q]
        h_ref[:, s] = o3[q][c:] + qq[q]


def _rw_scan(r, v, kd, b, an, cum, reverse):
    bsz, t, w = r.shape
    nc = t // RW_CHUNK
    assert w % (QUAD * RW_HEAD) == 0
    if reverse:
        imap = lambda bb, ci: (bb, nc - 1 - ci, 0)
    else:
        imap = lambda bb, ci: (bb, ci, 0)
    spec = pl.BlockSpec((1, RW_CHUNK, w), imap)
    return pl.pallas_call(
        functools.partial(_scan_quad_kernel, reverse=reverse),
        grid=(bsz, nc),
        in_specs=[spec] * 6,
        out_specs=spec,
        out_shape=jax.ShapeDtypeStruct((bsz, t, w), F32),
        scratch_shapes=[pltpu.VMEM((RW_HEAD, w), F32)],
        compiler_params=_cparams("arbitrary", "arbitrary"),
        name="rw_scan_bwd" if reverse else "rw_scan_fwd",
    )(r, v, kd, b, an, cum)


def _merge_kernel(yf_ref, yb_ref, bonus_ref, g_ref, att_ref, ga_ref, gb_ref, lw_ref, lb_ref,
                  e_ref, et_ref, o_ref):
    y = yf_ref[0] + yb_ref[0]
    mean = _seg_sum(y, e_ref, et_ref) * (1.0 / RW_HEAD)
    d = y - mean
    var = _seg_sum(d * d, e_ref, et_ref) * (1.0 / RW_HEAD)
    yn = d * lax.rsqrt(var + GN_EPS) * lw_ref[...] + lb_ref[...]
    ya = (yn + bonus_ref[0]) * g_ref[0]
    o_ref[0] = (_sigmoid(ga_ref[0]) * ya + _sigmoid(gb_ref[0]) * att_ref[0]).astype(o_ref.dtype)


def _merge(yf, yb, bonus, g, att, z_gates, lnx_w, lnx_b):
    bsz, t, w = yf.shape
    nh = w // RW_HEAD
    tt = _pick(t, 256, SUBLANES)
    head = jnp.arange(w, dtype=I32) // RW_HEAD
    e = (head[:, None] == jnp.arange(nh, dtype=I32)[None, :]).astype(BF16)
    et = e.T
    tile = pl.BlockSpec((1, tt, w), lambda b, i: (b, i, 0))
    gate_b = pl.BlockSpec((1, tt, w), lambda b, i: (b, i, 1))
    vec = pl.BlockSpec((1, w), lambda b, i: (0, 0))
    return pl.pallas_call(
        _merge_kernel,
        grid=(bsz, t // tt),
        in_specs=[tile, tile, tile, tile, tile, tile, gate_b, vec, vec,
                  pl.BlockSpec(e.shape, lambda b, i: (0, 0)),
                  pl.BlockSpec(et.shape, lambda b, i: (0, 0))],
        out_specs=tile,
        out_shape=jax.ShapeDtypeStruct((bsz, t, w), BF16),
        compiler_params=_cparams("arbitrary", "arbitrary"),
        name="merge",
    )(yf, yb, bonus, g, att, z_gates, z_gates, lnx_w.reshape(1, w), lnx_b.reshape(1, w), e, et)


LOG2E = math.log2(math.e)


MIN_ROW_SUM = 2.0 ** -60


def _attn_kernel(slope_ref, lam_ref, q_ref, k_ref, v_ref, sg_ref, d_ref, o_ref, kb_ref, vb_ref,
                 kn_ref, *, out_scale):
    h = pl.program_id(1)
    qi = pl.program_id(2)
    tq = q_ref.shape[1]
    t = k_ref.shape[1]

    @pl.when(qi == 0)
    def _():
        k = k_ref[0]
        for c in range(2):
            kc = k[:, c * DA_QK:(c + 1) * DA_QK].astype(BF16)
            kb_ref[c] = kc
            kf = kc.astype(F32)
            kn2 = jnp.max(jnp.sum(kf * kf, axis=1, keepdims=True), axis=0, keepdims=True)
            kn_ref[c] = jnp.broadcast_to(kn2, kn_ref.shape[1:])
        vb_ref[...] = v_ref[0].astype(BF16)

    start = pl.multiple_of((pl.num_programs(2) - 1 - qi) * tq, tq)
    bias = (-LOG2E * slope_ref[h]) * d_ref[:, pl.ds(start, t)]
    q = (q_ref[0] * (DA_QK ** -0.5 * LOG2E)).astype(BF16)
    qf = q.astype(F32)

    def softmax_pv(row_shift):
        outs, sums = [], []
        for c in range(2):
            cs = slice(c * DA_QK, (c + 1) * DA_QK)
            s = _dg(q[:, cs], kb_ref[c], _NT) + bias
            p = jnp.exp2(s - row_shift(c, s))
            sums.append(jnp.sum(p, axis=1, keepdims=True))
            outs.append(_dg(p.astype(BF16), vb_ref[...], _NN))
        return outs, sums

    def finish(outs, sums):
        o = outs[0] / sums[0] - lam_ref[0] * (outs[1] / sums[1])
        o = o * lax.rsqrt(jnp.mean(o * o, axis=-1, keepdims=True) + EPS) * sg_ref[...]
        o_ref[0] = o * out_scale

    def norm_bound(c, s):
        qc = qf[:, c * DA_QK:(c + 1) * DA_QK]
        qn2 = jnp.sum(qc * qc, axis=1, keepdims=True)
        return jnp.sqrt(qn2 * kn_ref[c][0:1, 0:1]) * (1.0 + 2.0 ** -7)

    outs, sums = softmax_pv(norm_bound)
    safe = jnp.minimum(jnp.min(sums[0]), jnp.min(sums[1])) >= MIN_ROW_SUM

    @pl.when(safe)
    def _():
        finish(outs, sums)

    @pl.when(jnp.logical_not(safe))
    def _():
        finish(*softmax_pv(lambda c, s: jnp.max(s, axis=1, keepdims=True)))


def _diff_attention(z_qkv, lam, sub_g, lambda_init):
    bsz, t, w3 = z_qkv.shape
    nh = w3 // 3 // DA_V
    tq = _pick(t, 512, LANES)
    slopes = 2.0 ** (-8.0 * jnp.arange(1, nh + 1, dtype=F32) / nh)
    dist = jnp.abs(jnp.arange(tq, dtype=I32)[:, None] + (t - tq)
                   - jnp.arange(2 * t - tq, dtype=I32)[None, :]).astype(F32)
    smem = pl.BlockSpec(memory_space=pltpu.SMEM)
    return pl.pallas_call(
        functools.partial(_attn_kernel, out_scale=1.0 - lambda_init),
        grid=(bsz, nh, t // tq),
        in_specs=[smem, smem,
                  pl.BlockSpec((1, tq, DA_V), lambda b, h, i: (b, i, h)),
                  pl.BlockSpec((1, t, DA_V), lambda b, h, i: (b, 0, nh + h)),
                  pl.BlockSpec((1, t, DA_V), lambda b, h, i: (b, 0, 2 * nh + h)),
                  pl.BlockSpec((1, DA_V), lambda b, h, i: (0, 0)),
                  pl.BlockSpec(dist.shape, lambda b, h, i: (0, 0), pipeline_mode=pl.Buffered(1))],
        out_specs=pl.BlockSpec((1, tq, DA_V), lambda b, h, i: (b, i, h)),
        out_shape=jax.ShapeDtypeStruct((bsz, t, nh * DA_V), F32),
        scratch_shapes=[pltpu.VMEM((2, t, DA_QK), BF16), pltpu.VMEM((t, DA_V), BF16),
                        pltpu.VMEM((2, SUBLANES, LANES), F32)],
        compiler_params=_cparams("arbitrary", "arbitrary", "arbitrary"),
        name="diff_attention",
    )(slopes, lam, z_qkv, z_qkv, z_qkv, sub_g.reshape(1, DA_V), dist)


def _router_kernel(x_ref, g_ref, sc_ref, sh_ref, wr_ref, h_ref, aff_ref):
    x = x_ref[0]
    y = x * lax.rsqrt(jnp.mean(x * x, axis=-1, keepdims=True) + EPS) * g_ref[...]
    h = y * (1.0 + sc_ref[0]) + sh_ref[0]
    h_ref[0] = h.astype(h_ref.dtype)
    logits = _dot3(wr_ref[...], h, _NT)
    m = jnp.max(logits, axis=0, keepdims=True)
    ex = jnp.exp(logits - m)
    aff = ex / jnp.sum(ex, axis=0, keepdims=True)
    for j in range(aff_ref.shape[1]):
        aff_ref[:, j, :] = aff[:, j * LANES:(j + 1) * LANES]


def _router(x, g, scale, shift, w_router_t):
    bsz, t, d = x.shape
    ne = w_router_t.shape[0]
    tt = _pick(t, SUBLANES * LANES, SUBLANES * LANES)
    nb = tt // LANES
    n_t = t // tt
    row = pl.BlockSpec((1, 1, d), lambda b, i: (b, 0, 0))
    return pl.pallas_call(
        _router_kernel,
        grid=(bsz, n_t),
        in_specs=[pl.BlockSpec((1, tt, d), lambda b, i: (b, i, 0)),
                  pl.BlockSpec((1, d), lambda b, i: (0, 0)), row, row,
                  pl.BlockSpec((ne, d), lambda b, i: (0, 0))],
        out_specs=[pl.BlockSpec((1, tt, d), lambda b, i: (b, i, 0)),
                   pl.BlockSpec((ne, nb, LANES), lambda b, i: (0, b * n_t + i, 0))],
        out_shape=[jax.ShapeDtypeStruct((bsz, t, d), F32),
                   jax.ShapeDtypeStruct((ne, bsz * t // LANES, LANES), F32)],
        compiler_params=_cparams("arbitrary", "arbitrary"),
        name="router",
    )(x, g.reshape(1, d), scale, shift, w_router_t)


def _sum_all(x):
    return jnp.sum(jnp.sum(x, axis=1, keepdims=True), axis=0, keepdims=True)


SLOT_ROWS = 8


def _select_kernel(aff_ref, slots_ref, pos_ref, base_ref, rec_ref, *, cap):
    nb = aff_ref.shape[1]
    li = lax.broadcasted_iota(I32, (LANES, LANES), 0)
    lj = lax.broadcasted_iota(I32, (LANES, LANES), 1)
    upper = jnp.where(li <= lj, 1.0, 0.0).astype(BF16)
    ri = lax.broadcasted_iota(I32, (nb, nb), 0)
    rj = lax.broadcasted_iota(I32, (nb, nb), 1)
    lower = jnp.where(rj < ri, 1.0, 0.0).astype(BF16)

    def excl_prefix(mask):
        x = jnp.where(mask, 1.0, 0.0)
        inc = _dg(x.astype(BF16), upper, _NN)
        tot = jnp.broadcast_to(inc[:, LANES - 1:LANES], (nb, LANES))
        offs = _dg(lower, tot.astype(BF16), _NN)
        return offs + inc - x, offs

    def expert_body(e, carry):
        bits = pltpu.bitcast(aff_ref[e], I32)

        def bit_body(i, thr):
            cand = thr | jnp.left_shift(jnp.int32(1), 30 - i)
            cnt = _sum_all(jnp.where(bits >= cand, 1.0, 0.0))
            return jnp.where(cnt >= cap, cand, thr)

        thr = lax.fori_loop(0, 31, bit_body, jnp.zeros((1, 1), I32))
        gt = bits > thr
        eq = bits == thr
        need = cap - _sum_all(jnp.where(gt, 1.0, 0.0))
        sel = gt | (eq & (excl_prefix(eq)[0] < need))
        pos, row_base = excl_prefix(sel)
        pos_ref[...] = jnp.where(sel, pos, -1.0).astype(I32)
        base_ref[...] = row_base.astype(I32)
        rec_ref[...] = jnp.zeros_like(rec_ref)
        window_ids = lax.broadcasted_iota(I32, (2 * LANES, LANES), 0)
        lane_f = lax.broadcasted_iota(I32, (1, LANES), 1).astype(F32)

        def group_body(g, c2):
            w0 = pl.multiple_of((jnp.max(base_ref[pl.ds(g, 1), :]) // LANES) * LANES, LANES)
            a = aff_ref[e, pl.ds(g, 1), :]
            a_hi = a.astype(BF16).astype(F32)
            a_mid = (a - a_hi).astype(BF16).astype(F32)
            a_lo = a - a_hi - a_mid
            base = jnp.full((1, LANES), LANES, F32) * g
            rec = jnp.concatenate([lane_f, base, a_hi, a_mid, a_lo,
                                   jnp.zeros((SLOT_ROWS - 5, LANES), F32)], axis=0)
            hot = jnp.where(window_ids == pos_ref[pl.ds(g, 1), :] - w0, 1.0, 0.0)
            rec_ref[:, pl.ds(w0, 2 * LANES)] += _dot1(rec, hot, _NT)
            return c2

        lax.fori_loop(0, nb, group_body, 0, unroll=4)
        rec = rec_ref[:, :cap]
        rid = lax.broadcasted_iota(I32, (SLOT_ROWS, cap), 0)
        token = rec[0:1] + rec[1:2]
        gate = rec[2:3] + rec[3:4] + rec[4:5]
        slots_ref[e] = jnp.where(rid == 0, token, jnp.where(rid == 1, gate, 0.0))
        return carry

    lax.fori_loop(0, aff_ref.shape[0], expert_body, 0)


def _select(aff, cap):
    ne, nb, _ = aff.shape
    return pl.pallas_call(
        functools.partial(_select_kernel, cap=cap),
        grid=(1,),
        in_specs=[pl.BlockSpec(aff.shape, lambda i: (0, 0, 0))],
        out_specs=pl.BlockSpec((ne, SLOT_ROWS, cap), lambda i: (0, 0, 0)),
        out_shape=jax.ShapeDtypeStruct((ne, SLOT_ROWS, cap), F32),
        scratch_shapes=[pltpu.VMEM((nb, LANES), I32), pltpu.VMEM((nb, LANES), I32),
                        pltpu.VMEM((SLOT_ROWS, -(-cap // LANES) * LANES + 2 * LANES), F32)],
        compiler_params=_cparams("arbitrary"),
        name="select",
    )(aff)


def _ffn_kernel(idx_ref, h_hbm, wg_ref, wu_ref, wd_ref, gate_ref, moe_in_hbm, moe_hbm,
                xf_ref, xb_ref, acc_ref, rows_ref, sems):
    del moe_in_hbm
    e = pl.program_id(0)
    f = pl.program_id(1)
    n_e = pl.num_programs(0)
    n_f = pl.num_programs(1)
    cap = xf_ref.shape[0]

    def x_copy(ex, p):
        row = idx_ref[ex * cap + p]
        return pltpu.make_async_copy(h_hbm.at[pl.ds(row, 1)], xf_ref.at[pl.ds(p, 1)], sems.at[0])

    def acc_copy(p):
        row = idx_ref[e * cap + p]
        return pltpu.make_async_copy(moe_hbm.at[pl.ds(row, 1)], rows_ref.at[pl.ds(p, 1)], sems.at[1])

    def out_copy(ex, p):
        row = idx_ref[ex * cap + p]
        return pltpu.make_async_copy(rows_ref.at[pl.ds(p, 1)], moe_hbm.at[pl.ds(row, 1)], sems.at[2])

    def for_slots(fn):
        def body(p, c):
            fn(p)
            return c
        lax.fori_loop(0, cap, body, 0, unroll=8)

    @pl.when(f == 0)
    def _():
        @pl.when(e == 0)
        def _():
            for_slots(lambda p: x_copy(e, p).start())

        for_slots(lambda p: x_copy(e, p).wait())
        xb_ref[...] = xf_ref[...].astype(BF16)

        @pl.when(e + 1 < n_e)
        def _():
            for_slots(lambda p: x_copy(e + 1, p).start())

        acc_ref[...] = jnp.zeros_like(acc_ref)

    xe = xb_ref[...]
    hid = _silu(_dot1(xe, wg_ref[0])) * _dot1(xe, wu_ref[0])
    acc_ref[...] += _dot1(hid, wd_ref[0])

    @pl.when(f == n_f // 2)
    def _():
        @pl.when(e > 0)
        def _():
            for_slots(lambda p: out_copy(e - 1, p).wait())

        for_slots(lambda p: acc_copy(p).start())

    @pl.when(f == n_f - 1)
    def _():
        for_slots(lambda p: acc_copy(p).wait())
        rows_ref[...] += acc_ref[...] * gate_ref[0]
        for_slots(lambda p: out_copy(e, p).start())

        @pl.when(e == n_e - 1)
        def _():
            for_slots(lambda p: out_copy(e, p).wait())


def _expert_ffn(idx, gate, h2, w_gate, w_up, w_down):
    ne, cap, _ = gate.shape
    n_tok, d = h2.shape
    ff = w_gate.shape[-1]
    tf = _pick(ff, 256, LANES)
    n_f = ff // tf
    assert n_f >= 2, "accumulator rows are fetched mid-way through an expert's steps"
    any_spec = pl.BlockSpec(memory_space=pl.ANY)
    grid_spec = pltpu.PrefetchScalarGridSpec(
        num_scalar_prefetch=1,
        grid=(ne, n_f),
        in_specs=[any_spec,
                  pl.BlockSpec((1, d, tf), lambda e, f, idx_ref: (e, 0, f)),
                  pl.BlockSpec((1, d, tf), lambda e, f, idx_ref: (e, 0, f)),
                  pl.BlockSpec((1, tf, d), lambda e, f, idx_ref: (e, f, 0)),
                  pl.BlockSpec((1, cap, 1), lambda e, f, idx_ref: (e, 0, 0)),
                  any_spec],
        out_specs=any_spec,
        scratch_shapes=[pltpu.VMEM((cap, d), F32), pltpu.VMEM((cap, d), BF16),
                        pltpu.VMEM((cap, d), F32), pltpu.VMEM((cap, d), F32),
                        pltpu.SemaphoreType.DMA((3,))])
    return pl.pallas_call(
        _ffn_kernel,
        grid_spec=grid_spec,
        out_shape=jax.ShapeDtypeStruct((n_tok, d), F32),
        input_output_aliases={6: 0},
        compiler_params=_cparams("arbitrary", "arbitrary"),
        name="expert_ffn",
    )(idx, h2, w_gate, w_up, w_down, gate, jnp.zeros((n_tok, d), F32))


def _residual_out_kernel(x_ref, m_ref, g_ref, fg_ref, o_ref, *, final_norm):
    x = x_ref[0] + g_ref[0] * m_ref[0]
    if final_norm:
        x = x * lax.rsqrt(jnp.mean(x * x, axis=-1, keepdims=True) + EPS) * fg_ref[...]
    o_ref[0] = x


def _residual_out(x, moe, gate2, final_g, final_norm):
    bsz, t, d = x.shape
    tt = _pick(t, 512, SUBLANES)
    tile = pl.BlockSpec((1, tt, d), lambda b, i: (b, i, 0))
    return pl.pallas_call(
        functools.partial(_residual_out_kernel, final_norm=final_norm),
        grid=(bsz, t // tt),
        in_specs=[tile, tile, pl.BlockSpec((1, 1, d), lambda b, i: (b, 0, 0)),
                  pl.BlockSpec((1, d), lambda b, i: (0, 0))],
        out_specs=tile,
        out_shape=jax.ShapeDtypeStruct((bsz, t, d), F32),
        compiler_params=_cparams("arbitrary", "arbitrary"),
        name="residual_out",
    )(x, moe, gate2, final_g.reshape(1, d))


def _encoder_layer(x, ada, lw, lambda_init, final_g, final_norm):
    bsz, t, d = x.shape
    shift1, scale1, gate1, shift2, scale2, gate2 = [a[:, None, :] for a in jnp.split(ada, 6, axis=-1)]

    h = _norm_mod(x, lw["norm1_g"], scale1, shift1).reshape(bsz * t, d)
    z_rkv = _matmul(h, lw["w_rkv"]).reshape(bsz, t, -1)
    z_lo = _matmul(h, lw["w_lo"]).reshape(bsz, t, -1)
    z_qkv = _matmul(h, lw["w_qkv"]).reshape(bsz, t, -1)
    z_gates = _matmul(h, lw["w_gates"]).reshape(bsz, t, -1)

    (r, v, g, bonus, kd_f, b_f, an_f, cum_f, kd_b, b_b, an_b, cum_b) = _rw_prep(
        z_rkv, z_lo, lw["mu_r"], lw["mu_l"], lw["w0"], lw["w2"], lw["a0"], lw["a2"], lw["g2"],
        lw["k_k"], lw["k_a"], lw["r_k"])
    y_f = _rw_scan(r, v, kd_f, b_f, an_f, cum_f, reverse=False)
    y_b = _rw_scan(r, v, kd_b, b_b, an_b, cum_b, reverse=True)

    att = _diff_attention(z_qkv, lw["lam"], lw["sub_g"], lambda_init)
    merged = _merge(y_f, y_b, bonus, g, att, z_gates, lw["lnx_w"], lw["lnx_b"])
    x1 = _matmul_residual(merged, lw["w_out"], x, gate1)

    n_tok = bsz * t
    ne = lw["w_router_t"].shape[0]
    cap = max(1, CAPACITY_FACTOR * n_tok // ne)
    h2, aff = _router(x1, lw["norm2_g"], scale2, shift2, lw["w_router_t"])
    slots = _select(aff, cap)
    idx = slots[:, 0, :].astype(I32).reshape(ne * cap)
    gate = slots[:, 1, :].reshape(ne, cap, 1)
    moe = _expert_ffn(idx, gate, h2.reshape(n_tok, d), lw["w_gate"], lw["w_up"], lw["w_down"])
    return _residual_out(x1, moe.reshape(bsz, t, d), gate2, final_g, final_norm)


def kernel(x_prompt, x_sample, c_prompt, c_sample, w_ada, b_ada, norm1_g, w_in, mu_shift, w0, w2, a0, a2, g2, k_k, k_a, r_k, lnx_w, lnx_b, lam_q1, lam_k1, lam_q2, lam_k2, sub_g, w_out, norm2_g, w_router, w_gate, w_up, w_down, final_g):
    depth = w_ada.shape[0]
    d = x_prompt.shape[-1]
    w3 = 3 * d
    n_lo = 2 * w2.shape[2] + 2 * a2.shape[2] + g2.shape[1]
    xs = [x_prompt, x_sample]
    n_rows = [x_prompt.shape[0], x_sample.shape[0]]
    c_all = jnp.concatenate([c_prompt, c_sample], axis=0)
    pad = (-c_all.shape[0]) % SUBLANES
    c_all = jnp.pad(c_all, ((0, pad), (0, 0)))
    for l in range(depth):
        lambda_init = 0.8 - 0.6 * math.exp(-0.3 * l)
        w_in_l = w_in[l].astype(BF16)
        lam = (jnp.exp(jnp.sum(lam_q1[l] * lam_k1[l])) - jnp.exp(jnp.sum(lam_q2[l] * lam_k2[l]))
               + lambda_init).astype(F32).reshape(1)
        lw = dict(
            norm1_g=norm1_g[l], norm2_g=norm2_g[l],
            w_rkv=w_in_l[:, :w3], w_lo=w_in_l[:, w3:w3 + n_lo],
            w_qkv=w_in_l[:, w3 + n_lo:w3 + n_lo + 3 * d], w_gates=w_in_l[:, w3 + n_lo + 3 * d:],
            mu_r=mu_shift[l, :w3], mu_l=mu_shift[l, w3:w3 + n_lo],
            w0=w0[l], w2=w2[l], a0=a0[l], a2=a2[l], g2=g2[l], k_k=k_k[l], k_a=k_a[l], r_k=r_k[l],
            lnx_w=lnx_w[l], lnx_b=lnx_b[l], lam=lam, sub_g=sub_g[l],
            w_out=w_out[l].astype(BF16), w_router_t=w_router[l].T,
            w_gate=w_gate[l], w_up=w_up[l], w_down=w_down[l])
        ada = _ada(c_all, w_ada[l], b_ada[l])
        off = 0
        for gi in range(2):
            xs[gi] = _encoder_layer(xs[gi], ada[off:off + n_rows[gi]], lw, lambda_init, final_g,
                                    final_norm=(l == depth - 1))
            off += n_rows[gi]
    return tuple(xs)
```

```python
import functools
import math

import jax
import jax.numpy as jnp
from jax import lax
from jax.experimental import pallas as pl
from jax.experimental.pallas import tpu as pltpu

F32 = jnp.float32
BF16 = jnp.bfloat16
I32 = jnp.int32

RW_HEAD = 64
DA_QK = 64
DA_V = 2 * DA_QK
RW_CHUNK = 64
CAPACITY_FACTOR = 2
EPS = 1e-6
GN_EPS = RW_HEAD * 1e-5
LANES = 128
SUBLANES = 8
VMEM_LIMIT = 56 * 1024 * 1024


def _cparams(*sem):
    return pltpu.CompilerParams(dimension_semantics=sem, vmem_limit_bytes=VMEM_LIMIT)


def _pick(n, pref, align):
    if n <= pref:
        return n
    t = (pref // align) * align
    while t >= align:
        if n % t == 0:
            return t
        t -= align
    return n


_NN = (((1,), (0,)), ((), ()))
_NT = (((1,), (1,)), ((), ()))
_TN = (((0,), (0,)), ((), ()))


def _dg(a, b, dims):
    return lax.dot_general(a, b, dims, preferred_element_type=F32)


def _dot1(a, b, dims=_NN):
    return _dg(a.astype(BF16), b.astype(BF16), dims)


def _split(x):
    hi = x.astype(BF16)
    lo = (x - hi.astype(F32)).astype(BF16)
    return hi, lo


def _dot3(a, b, dims=_NN):
    ah, al = _split(a)
    bh, bl = _split(b)
    return _dg(ah, bh, dims) + (_dg(ah, bl, dims) + _dg(al, bh, dims))


def _dot_exact_rhs(a, b_bf16, dims=_NN):
    ah, al = _split(a)
    return _dg(ah, b_bf16, dims) + _dg(al, b_bf16, dims)


def _sigmoid(x):
    return 1.0 / (1.0 + jnp.exp(-x))


def _silu(x):
    return x * _sigmoid(x)


def _ada_kernel(c_ref, w_ref, b_ref, o_ref):
    o_ref[...] = _dot1(_silu(c_ref[...]), w_ref[...]) + b_ref[...]


def _ada(c, w, b):
    m, d = c.shape
    n = w.shape[1]
    tn = _pick(n, 1024, LANES)
    return pl.pallas_call(
        _ada_kernel,
        grid=(n // tn,),
        in_specs=[pl.BlockSpec((m, d), lambda j: (0, 0)),
                  pl.BlockSpec((d, tn), lambda j: (0, j)),
                  pl.BlockSpec((1, tn), lambda j: (0, j))],
        out_specs=pl.BlockSpec((m, tn), lambda j: (0, j)),
        out_shape=jax.ShapeDtypeStruct((m, n), F32),
        compiler_params=_cparams("arbitrary"),
        name="ada",
    )(c, w, b.reshape(1, n))


def _norm_mod_kernel(x_ref, g_ref, sc_ref, sh_ref, o_ref):
    x = x_ref[0]
    y = x * lax.rsqrt(jnp.mean(x * x, axis=-1, keepdims=True) + EPS) * g_ref[...]
    o_ref[0] = (y * (1.0 + sc_ref[0]) + sh_ref[0]).astype(o_ref.dtype)


def _norm_mod(x, g, scale, shift):
    bsz, t, d = x.shape
    tt = _pick(t, 512, SUBLANES)
    row = pl.BlockSpec((1, 1, d), lambda b, i: (b, 0, 0))
    return pl.pallas_call(
        _norm_mod_kernel,
        grid=(bsz, t // tt),
        in_specs=[pl.BlockSpec((1, tt, d), lambda b, i: (b, i, 0)),
                  pl.BlockSpec((1, d), lambda b, i: (0, 0)), row, row],
        out_specs=pl.BlockSpec((1, tt, d), lambda b, i: (b, i, 0)),
        out_shape=jax.ShapeDtypeStruct((bsz, t, d), BF16),
        compiler_params=_cparams("arbitrary", "arbitrary"),
        name="norm_mod",
    )(x, g.reshape(1, d), scale, shift)


def _mm_kernel(a_ref, b_ref, o_ref):
    o_ref[...] = _dot1(a_ref[...], b_ref[...]).astype(o_ref.dtype)


def _matmul(a, b, out_dtype=F32, tm_pref=1024, tn_pref=1024):
    m, k = a.shape
    n = b.shape[1]
    tm = _pick(m, tm_pref, SUBLANES)
    tn = _pick(n, tn_pref, LANES)
    return pl.pallas_call(
        _mm_kernel,
        grid=(m // tm, n // tn),
        in_specs=[pl.BlockSpec((tm, k), lambda i, j: (i, 0)),
                  pl.BlockSpec((k, tn), lambda i, j: (0, j))],
        out_specs=pl.BlockSpec((tm, tn), lambda i, j: (i, j)),
        out_shape=jax.ShapeDtypeStruct((m, n), out_dtype),
        compiler_params=_cparams("arbitrary", "arbitrary"),
        name="matmul",
    )(a, b)


def _mm_res_kernel(a_ref, b_ref, x_ref, g_ref, o_ref):
    o_ref[0] = x_ref[0] + g_ref[0] * _dot1(a_ref[0], b_ref[...])


def _matmul_residual(a, b, x, gate):
    bsz, t, k = a.shape
    n = b.shape[1]
    tm = _pick(t, 1024, SUBLANES)
    tn = _pick(n, 1024, LANES)
    return pl.pallas_call(
        _mm_res_kernel,
        grid=(bsz, t // tm, n // tn),
        in_specs=[pl.BlockSpec((1, tm, k), lambda bb, i, j: (bb, i, 0)),
                  pl.BlockSpec((k, tn), lambda bb, i, j: (0, j)),
                  pl.BlockSpec((1, tm, tn), lambda bb, i, j: (bb, i, j)),
                  pl.BlockSpec((1, 1, tn), lambda bb, i, j: (bb, 0, j))],
        out_specs=pl.BlockSpec((1, tm, tn), lambda bb, i, j: (bb, i, j)),
        out_shape=jax.ShapeDtypeStruct((bsz, t, n), F32),
        compiler_params=_cparams("arbitrary", "arbitrary", "arbitrary"),
        name="matmul_residual",
    )(a, b, x, gate)


def _shift_tile(z, prev8, next8, mu, i, n_t):
    tt = z.shape[0]
    pr = jnp.where(i > 0, prev8[SUBLANES - 1:SUBLANES, :], 0.0)
    nx = jnp.where(i < n_t - 1, next8[0:1, :], 0.0)
    ridx = lax.broadcasted_iota(I32, z.shape, 0)
    zp = jnp.where(ridx == 0, pr, pltpu.roll(z, 1, 0))
    zn = jnp.where(ridx == tt - 1, nx, pltpu.roll(z, tt - 1, 0))
    return z + mu * (0.5 * (zp + zn) - z)


def _seg_sum(x, e_ref, et_ref):
    s = _dot_exact_rhs(x, e_ref[...])
    return _dot_exact_rhs(s, et_ref[...])


def _rw_prep_kernel(zr_ref, zrp_ref, zrn_ref, zl_ref, zlp_ref, zln_ref, mur_ref, mul_ref,
                    w0_ref, w2_ref, a0_ref, a2_ref, g2_ref, kk_ref, ka_ref, rk_ref,
                    e_ref, et_ref, lf_ref, lb_ref,
                    r_o, v_o, g_o, bonus_o,
                    kdf_o, bf_o, anf_o, cumf_o, kdb_o, bb_o, anb_o, cumb_o, *, dl, il):
    i = pl.program_id(1)
    n_t = pl.num_programs(1)
    w = r_o.shape[-1]
    zs = _shift_tile(zr_ref[0], zrp_ref[0], zrn_ref[0], mur_ref[...], i, n_t)
    zl = _shift_tile(zl_ref[0], zlp_ref[0], zln_ref[0], mul_ref[...], i, n_t)
    r = zs[:, 0:w]
    k = zs[:, w:2 * w]
    v = zs[:, 2 * w:3 * w]
    gd = zl[:, 2 * dl + 2 * il:]
    kkr = k * kk_ref[...]
    kk = kkr * lax.rsqrt(jnp.maximum(_seg_sum(kkr * kkr, e_ref, et_ref), 1e-24))
    r_o[0] = r
    v_o[0] = v
    g_o[0] = _dot1(_sigmoid(gd), g2_ref[...])
    kd_sum = None
    outs = ((kdf_o, bf_o, anf_o, cumf_o, lf_ref), (kdb_o, bb_o, anb_o, cumb_o, lb_ref))
    for z, (kd_o, b_o, an_o, cum_o, l_ref) in enumerate(outs):
        wd = zl[:, z * dl:(z + 1) * dl]
        ad = zl[:, 2 * dl + z * il:2 * dl + (z + 1) * il]
        wl = w0_ref[z:z + 1, :] + _dot1(jnp.tanh(wd), w2_ref[z])
        lw = -math.exp(-0.5) * _sigmoid(wl)
        icl = _sigmoid(a0_ref[z:z + 1, :] + _dot1(ad, a2_ref[z]))
        kd = k * (1.0 + (icl - 1.0) * ka_ref[...])
        kd_o[0] = kd
        b_o[0] = kk * icl
        an_o[0] = -kk * jnp.exp(-lw)
        lh = lw.astype(BF16)
        rem = lw - lh.astype(F32)
        lm = rem.astype(BF16)
        ll = (rem - lm.astype(F32)).astype(BF16)
        lmat = l_ref[...]
        cum_o[0] = _dg(lmat, lh, _NN) + (_dg(lmat, lm, _NN) + _dg(lmat, ll, _NN))
        kd_sum = kd if kd_sum is None else kd_sum + kd
    bonus_o[0] = _seg_sum(r * kd_sum * rk_ref[...], e_ref, et_ref) * v


def _rw_prep(z_rkv, z_lo, mu_r, mu_l, w0, w2, a0, a2, g2, k_k, k_a, r_k):
    bsz, t, w3 = z_rkv.shape
    w = w3 // 3
    lw_cols = z_lo.shape[-1]
    dl, il = w2.shape[1], a2.shape[1]
    nh = w // RW_HEAD
    tt = _pick(t, 128, RW_CHUNK)
    n_t = t // tt
    hb = tt // SUBLANES
    head = jnp.arange(w, dtype=I32) // RW_HEAD
    e = (head[:, None] == jnp.arange(nh, dtype=I32)[None, :]).astype(BF16)
    et = e.T
    ti = jnp.arange(tt, dtype=I32)
    same = (ti[:, None] // RW_CHUNK) == (ti[None, :] // RW_CHUNK)
    lf = (same & (ti[None, :] <= ti[:, None])).astype(BF16)
    lb = (same & (ti[None, :] >= ti[:, None])).astype(BF16)

    def tile(c):
        return pl.BlockSpec((1, tt, c), lambda b, i: (b, i, 0))

    def prev(c):
        return pl.BlockSpec((1, SUBLANES, c), lambda b, i: (b, jnp.maximum(i * hb - 1, 0), 0))

    def nxt(c):
        return pl.BlockSpec((1, SUBLANES, c),
                            lambda b, i: (b, jnp.minimum((i + 1) * hb, t // SUBLANES - 1), 0))

    def full(a):
        nd = a.ndim
        return pl.BlockSpec(a.shape, lambda b, i: (0,) * nd)

    consts = [mu_r.reshape(1, w3), mu_l.reshape(1, lw_cols), w0, w2, a0, a2, g2,
              k_k.reshape(1, w), k_a.reshape(1, w), r_k.reshape(1, w), e, et, lf, lb]
    out = jax.ShapeDtypeStruct((bsz, t, w), F32)
    return pl.pallas_call(
        functools.partial(_rw_prep_kernel, dl=dl, il=il),
        grid=(bsz, n_t),
        in_specs=[tile(w3), prev(w3), nxt(w3), tile(lw_cols), prev(lw_cols), nxt(lw_cols)]
        + [full(a) for a in consts],
        out_specs=[tile(w)] * 12,
        out_shape=[out] * 12,
        compiler_params=_cparams("arbitrary", "arbitrary"),
        name="rw_prep",
    )(z_rkv, z_rkv, z_rkv, z_lo, z_lo, z_lo, *consts)


QUAD = 4


def _scan_quad_kernel(*refs):
    n_in = 6
    ins = (refs[:n_in], refs[n_in:2 * n_in])
    y_refs = refs[2 * n_in:2 * n_in + 2]
    h_refs = refs[2 * n_in + 2:]
    c = RW_CHUNK
    n = RW_HEAD
    qw = QUAD * n
    nq = h_refs[0].shape[1] // qw

    @pl.when(pl.program_id(1) == 0)
    def _():
        for h_ref in h_refs:
            h_ref[...] = jnp.zeros_like(h_ref)

    row = lax.broadcasted_iota(I32, (c, qw), 0)
    col = lax.broadcasted_iota(I32, (c, qw), 1) % n
    strict_d = (col < row, col > row)
    incl_d = (col <= row, col >= row)
    last_d = (c - 1, 0)
    eye = row == col
    eye_f = jnp.where(eye, 1.0, 0.0)
    bd_mask = (lax.broadcasted_iota(I32, (QUAD * c, qw), 0) // c
               == lax.broadcasted_iota(I32, (QUAD * c, qw), 1) // n)
    first_of_pair = lax.broadcasted_iota(I32, (c, 2 * n), 1) < n

    def bd(x):
        xb = x.astype(BF16)
        return jnp.where(bd_mask, jnp.concatenate([xb] * QUAD, axis=0), jnp.zeros((), BF16))

    def mm(a, b_bf16, dims=_NN):
        return _dg(a.astype(BF16), b_bf16, dims)

    chains = [(d, slice(qw * q, qw * (q + 1))) for d in range(2) for q in range(nq)]

    def load(i):
        return [ins[d][i][0, :, s] for d, s in chains]

    r_in, vs, kd_in, b_in, an_in, cum = (load(i) for i in range(n_in))
    e1 = [jnp.exp(x) for x in cum]
    e2 = [jnp.exp(-x) for x in cum]
    et = [jnp.exp(x[last_d[d]:last_d[d] + 1, :]) for x, (d, _) in zip(cum, chains)]
    rt = [x * e for x, e in zip(r_in, e1)]
    at = [x * e for x, e in zip(an_in, e1)]
    bt = [x * e for x, e in zip(b_in, e2)]
    kt = [x * e for x, e in zip(kd_in, e2)]
    bh = [x * e for x, e in zip(bt, et)]
    kh = [x * e for x, e in zip(kt, et)]
    hs = [h_refs[d][:, s] for d, s in chains]
    strict = [strict_d[d] for d, _ in chains]
    incl = [incl_d[d] for d, _ in chains]
    g = [mm(jnp.concatenate([a, r_], axis=0), jnp.concatenate([bd(b_), bd(k_)], axis=0), _NT)
         for a, r_, b_, k_ in zip(at, rt, bt, kt)]
    nab = [jnp.where(m, x[:c, :qw], 0.0) for m, x in zip(strict, g)]
    aak = [jnp.where(m, x[:c, qw:], 0.0) for m, x in zip(strict, g)]
    mb = [jnp.where(m, x[c:, :qw], 0.0) for m, x in zip(incl, g)]
    mk = [jnp.where(m, x[c:, qw:], 0.0) for m, x in zip(incl, g)]
    akv = [mm(jnp.concatenate([a, k_], axis=0), bd(x)) for a, k_, x in zip(aak, mk, vs)]
    av = [x[:c] for x in akv]
    mkv = [x[c:] for x in akv]
    tm = [eye_f + x for x in nab]
    npow = [mm(x, bd(x)) for x in nab]
    for _ in range(4):
        both = [mm(jnp.concatenate([x, t_], axis=0), bd(x)) for x, t_ in zip(npow, tm)]
        npow = [x[:c] for x in both]
        tm = [t_ + x[c:] for t_, x in zip(tm, both)]
    tm = [t_ + mm(t_, bd(x)) for t_, x in zip(tm, npow)]
    za = [mm(t_, bd(a)) for t_, a in zip(tm, at)]
    zw = [mm(t_, bd(a)) for t_, a in zip(tm, av)]
    rp = [r_ + mm(m, bd(z)) for r_, m, z in zip(rt, mb, za)]
    yi = [mm(m, bd(z)) + x for m, z, x in zip(mb, zw, mkv)]
    pm, qq = [], []
    for q in range(len(chains)):
        pms, qqs = [], []
        for pr in range(QUAD // 2):
            ps = slice(2 * n * pr, 2 * n * (pr + 1))
            lhs_t = jnp.concatenate([bh[q][:, ps], kh[q][:, ps]], axis=0)
            rhs = jnp.concatenate(
                [jnp.concatenate([za[q][:, ps], zw[q][:, ps]], axis=1),
                 jnp.concatenate([jnp.zeros((c, 2 * n), F32), vs[q][:, ps]], axis=1)], axis=0)
            m = _dot1(lhs_t, rhs, _TN)
            pms.append(jnp.where(first_of_pair, m[:n, :2 * n], m[n:, :2 * n]))
            qqs.append(jnp.where(first_of_pair, m[:n, 2 * n:], m[n:, 2 * n:]))
        pm.append(jnp.concatenate(pms, axis=1) + jnp.where(eye, et[q], 0.0))
        qq.append(jnp.concatenate(qqs, axis=1))
    o3 = [mm(jnp.concatenate([r_, p_], axis=0), bd(h_)) for r_, p_, h_ in zip(rp, pm, hs)]
    for q, (d, s) in enumerate(chains):
        y_refs[d][0, :, s] = o3[q][:c] + yi[q]
        h_refs[d][:, s] = o3[q][c:] + qq[q]


def _rw_scan(fwd, bwd):
    bsz, t, w = fwd[0].shape
    nc = t // RW_CHUNK
    assert w % (QUAD * RW_HEAD) == 0
    spec_f = pl.BlockSpec((1, RW_CHUNK, w), lambda bb, ci: (bb, ci, 0))
    spec_b = pl.BlockSpec((1, RW_CHUNK, w), lambda bb, ci: (bb, nc - 1 - ci, 0))
    out = jax.ShapeDtypeStruct((bsz, t, w), F32)
    return pl.pallas_call(
        _scan_quad_kernel,
        grid=(bsz, nc),
        in_specs=[spec_f] * 6 + [spec_b] * 6,
        out_specs=[spec_f, spec_b],
        out_shape=[out, out],
        scratch_shapes=[pltpu.VMEM((RW_HEAD, w), F32), pltpu.VMEM((RW_HEAD, w), F32)],
        compiler_params=_cparams("arbitrary", "arbitrary"),
        name="rw_scan",
    )(*fwd, *bwd)


def _merge_kernel(yf_ref, yb_ref, bonus_ref, g_ref, att_ref, ga_ref, gb_ref, lw_ref, lb_ref,
                  e_ref, et_ref, o_ref):
    y = yf_ref[0] + yb_ref[0]
    mean = _seg_sum(y, e_ref, et_ref) * (1.0 / RW_HEAD)
    d = y - mean
    var = _seg_sum(d * d, e_ref, et_ref) * (1.0 / RW_HEAD)
    yn = d * lax.rsqrt(var + GN_EPS) * lw_ref[...] + lb_ref[...]
    ya = (yn + bonus_ref[0]) * g_ref[0]
    o_ref[0] = (_sigmoid(ga_ref[0]) * ya + _sigmoid(gb_ref[0]) * att_ref[0]).astype(o_ref.dtype)


def _merge(yf, yb, bonus, g, att, z_gates, lnx_w, lnx_b):
    bsz, t, w = yf.shape
    nh = w // RW_HEAD
    tt = _pick(t, 256, SUBLANES)
    head = jnp.arange(w, dtype=I32) // RW_HEAD
    e = (head[:, None] == jnp.arange(nh, dtype=I32)[None, :]).astype(BF16)
    et = e.T
    tile = pl.BlockSpec((1, tt, w), lambda b, i: (b, i, 0))
    gate_b = pl.BlockSpec((1, tt, w), lambda b, i: (b, i, 1))
    vec = pl.BlockSpec((1, w), lambda b, i: (0, 0))
    return pl.pallas_call(
        _merge_kernel,
        grid=(bsz, t // tt),
        in_specs=[tile, tile, tile, tile, tile, tile, gate_b, vec, vec,
                  pl.BlockSpec(e.shape, lambda b, i: (0, 0)),
                  pl.BlockSpec(et.shape, lambda b, i: (0, 0))],
        out_specs=tile,
        out_shape=jax.ShapeDtypeStruct((bsz, t, w), BF16),
        compiler_params=_cparams("arbitrary", "arbitrary"),
        name="merge",
    )(yf, yb, bonus, g, att, z_gates, z_gates, lnx_w.reshape(1, w), lnx_b.reshape(1, w), e, et)


LOG2E = math.log2(math.e)


MIN_ROW_SUM = 2.0 ** -60


def _attn_kernel(slope_ref, lam_ref, q_ref, k_ref, v_ref, sg_ref, d_ref, o_ref, kb_ref, vb_ref,
                 kn_ref, *, out_scale):
    h = pl.program_id(1)
    qi = pl.program_id(2)
    tq = q_ref.shape[1]
    t = k_ref.shape[1]

    @pl.when(qi == 0)
    def _():
        k = k_ref[0]
        for c in range(2):
            kc = k[:, c * DA_QK:(c + 1) * DA_QK].astype(BF16)
            kb_ref[c] = kc
            kf = kc.astype(F32)
            kn2 = jnp.max(jnp.sum(kf * kf, axis=1, keepdims=True), axis=0, keepdims=True)
            kn_ref[c] = jnp.broadcast_to(kn2, kn_ref.shape[1:])
        vb_ref[...] = v_ref[0].astype(BF16)

    start = pl.multiple_of((pl.num_programs(2) - 1 - qi) * tq, tq)
    bias = (-LOG2E * slope_ref[h]) * d_ref[:, pl.ds(start, t)]
    q = (q_ref[0] * (DA_QK ** -0.5 * LOG2E)).astype(BF16)
    qf = q.astype(F32)

    def softmax_pv(row_shift):
        outs, sums = [], []
        for c in range(2):
            cs = slice(c * DA_QK, (c + 1) * DA_QK)
            s = _dg(q[:, cs], kb_ref[c], _NT) + bias
            p = jnp.exp2(s - row_shift(c, s))
            sums.append(jnp.sum(p, axis=1, keepdims=True))
            outs.append(_dg(p.astype(BF16), vb_ref[...], _NN))
        return outs, sums

    def finish(outs, sums):
        o = outs[0] / sums[0] - lam_ref[0] * (outs[1] / sums[1])
        o = o * lax.rsqrt(jnp.mean(o * o, axis=-1, keepdims=True) + EPS) * sg_ref[...]
        o_ref[0] = o * out_scale

    def norm_bound(c, s):
        qc = qf[:, c * DA_QK:(c + 1) * DA_QK]
        qn2 = jnp.sum(qc * qc, axis=1, keepdims=True)
        return jnp.sqrt(qn2 * kn_ref[c][0:1, 0:1]) * (1.0 + 2.0 ** -7)

    outs, sums = softmax_pv(norm_bound)
    safe = jnp.minimum(jnp.min(sums[0]), jnp.min(sums[1])) >= MIN_ROW_SUM

    @pl.when(safe)
    def _():
        finish(outs, sums)

    @pl.when(jnp.logical_not(safe))
    def _():
        finish(*softmax_pv(lambda c, s: jnp.max(s, axis=1, keepdims=True)))


def _diff_attention(z_qkv, lam, sub_g, lambda_init):
    bsz, t, w3 = z_qkv.shape
    nh = w3 // 3 // DA_V
    tq = _pick(t, 512, LANES)
    slopes = 2.0 ** (-8.0 * jnp.arange(1, nh + 1, dtype=F32) / nh)
    dist = jnp.abs(jnp.arange(tq, dtype=I32)[:, None] + (t - tq)
                   - jnp.arange(2 * t - tq, dtype=I32)[None, :]).astype(F32)
    smem = pl.BlockSpec(memory_space=pltpu.SMEM)
    return pl.pallas_call(
        functools.partial(_attn_kernel, out_scale=1.0 - lambda_init),
        grid=(bsz, nh, t // tq),
        in_specs=[smem, smem,
                  pl.BlockSpec((1, tq, DA_V), lambda b, h, i: (b, i, h)),
                  pl.BlockSpec((1, t, DA_V), lambda b, h, i: (b, 0, nh + h)),
                  pl.BlockSpec((1, t, DA_V), lambda b, h, i: (b, 0, 2 * nh + h)),
                  pl.BlockSpec((1, DA_V), lambda b, h, i: (0, 0)),
                  pl.BlockSpec(dist.shape, lambda b, h, i: (0, 0), pipeline_mode=pl.Buffered(1))],
        out_specs=pl.BlockSpec((1, tq, DA_V), lambda b, h, i: (b, i, h)),
        out_shape=jax.ShapeDtypeStruct((bsz, t, nh * DA_V), F32),
        scratch_shapes=[pltpu.VMEM((2, t, DA_QK), BF16), pltpu.VMEM((t, DA_V), BF16),
                        pltpu.VMEM((2, SUBLANES, LANES), F32)],
        compiler_params=_cparams("arbitrary", "arbitrary", "arbitrary"),
        name="diff_attention",
    )(slopes, lam, z_qkv, z_qkv, z_qkv, sub_g.reshape(1, DA_V), dist)


def _router_kernel(x_ref, g_ref, sc_ref, sh_ref, wr_ref, h_ref, aff_ref):
    x = x_ref[0]
    y = x * lax.rsqrt(jnp.mean(x * x, axis=-1, keepdims=True) + EPS) * g_ref[...]
    h = y * (1.0 + sc_ref[0]) + sh_ref[0]
    h_ref[0] = h.astype(h_ref.dtype)
    logits = _dot3(wr_ref[...], h, _NT)
    m = jnp.max(logits, axis=0, keepdims=True)
    ex = jnp.exp(logits - m)
    aff = ex / jnp.sum(ex, axis=0, keepdims=True)
    for j in range(aff_ref.shape[1]):
        aff_ref[:, j, :] = aff[:, j * LANES:(j + 1) * LANES]


def _router(x, g, scale, shift, w_router_t):
    bsz, t, d = x.shape
    ne = w_router_t.shape[0]
    tt = _pick(t, SUBLANES * LANES, SUBLANES * LANES)
    nb = tt // LANES
    n_t = t // tt
    row = pl.BlockSpec((1, 1, d), lambda b, i: (b, 0, 0))
    return pl.pallas_call(
        _router_kernel,
        grid=(bsz, n_t),
        in_specs=[pl.BlockSpec((1, tt, d), lambda b, i: (b, i, 0)),
                  pl.BlockSpec((1, d), lambda b, i: (0, 0)), row, row,
                  pl.BlockSpec((ne, d), lambda b, i: (0, 0))],
        out_specs=[pl.BlockSpec((1, tt, d), lambda b, i: (b, i, 0)),
                   pl.BlockSpec((ne, nb, LANES), lambda b, i: (0, b * n_t + i, 0))],
        out_shape=[jax.ShapeDtypeStruct((bsz, t, d), F32),
                   jax.ShapeDtypeStruct((ne, bsz * t // LANES, LANES), F32)],
        compiler_params=_cparams("arbitrary", "arbitrary"),
        name="router",
    )(x, g.reshape(1, d), scale, shift, w_router_t)


def _sum_all(x):
    return jnp.sum(jnp.sum(x, axis=1, keepdims=True), axis=0, keepdims=True)


SLOT_ROWS = 8


def _select_kernel(aff_ref, slots_ref, pos_ref, base_ref, rec_ref, *, cap):
    nb = aff_ref.shape[1]
    li = lax.broadcasted_iota(I32, (LANES, LANES), 0)
    lj = lax.broadcasted_iota(I32, (LANES, LANES), 1)
    upper = jnp.where(li <= lj, 1.0, 0.0).astype(BF16)
    ri = lax.broadcasted_iota(I32, (nb, nb), 0)
    rj = lax.broadcasted_iota(I32, (nb, nb), 1)
    lower = jnp.where(rj < ri, 1.0, 0.0).astype(BF16)

    def excl_prefix(mask):
        x = jnp.where(mask, 1.0, 0.0)
        inc = _dg(x.astype(BF16), upper, _NN)
        tot = jnp.broadcast_to(inc[:, LANES - 1:LANES], (nb, LANES))
        offs = _dg(lower, tot.astype(BF16), _NN)
        return offs + inc - x, offs

    def expert_body(e, carry):
        bits = pltpu.bitcast(aff_ref[e], I32)

        def bit_body(i, thr):
            cand = thr | jnp.left_shift(jnp.int32(1), 30 - i)
            cnt = _sum_all(jnp.where(bits >= cand, 1.0, 0.0))
            return jnp.where(cnt >= cap, cand, thr)

        thr = lax.fori_loop(0, 31, bit_body, jnp.zeros((1, 1), I32))
        gt = bits > thr
        eq = bits == thr
        need = cap - _sum_all(jnp.where(gt, 1.0, 0.0))
        sel = gt | (eq & (excl_prefix(eq)[0] < need))
        pos, row_base = excl_prefix(sel)
        pos_ref[...] = jnp.where(sel, pos, -1.0).astype(I32)
        base_ref[...] = row_base.astype(I32)
        rec_ref[...] = jnp.zeros_like(rec_ref)
        window_ids = lax.broadcasted_iota(I32, (2 * LANES, LANES), 0)
        lane_f = lax.broadcasted_iota(I32, (1, LANES), 1).astype(F32)

        def group_body(g, c2):
            w0 = pl.multiple_of((jnp.max(base_ref[pl.ds(g, 1), :]) // LANES) * LANES, LANES)
            a = aff_ref[e, pl.ds(g, 1), :]
            a_hi = a.astype(BF16).astype(F32)
            a_mid = (a - a_hi).astype(BF16).astype(F32)
            a_lo = a - a_hi - a_mid
            base = jnp.full((1, LANES), LANES, F32) * g
            rec = jnp.concatenate([lane_f, base, a_hi, a_mid, a_lo,
                                   jnp.zeros((SLOT_ROWS - 5, LANES), F32)], axis=0)
            hot = jnp.where(window_ids == pos_ref[pl.ds(g, 1), :] - w0, 1.0, 0.0)
            rec_ref[:, pl.ds(w0, 2 * LANES)] += _dot1(rec, hot, _NT)
            return c2

        lax.fori_loop(0, nb, group_body, 0, unroll=4)
        rec = rec_ref[:, :cap]
        rid = lax.broadcasted_iota(I32, (SLOT_ROWS, cap), 0)
        token = rec[0:1] + rec[1:2]
        gate = rec[2:3] + rec[3:4] + rec[4:5]
        slots_ref[e] = jnp.where(rid == 0, token, jnp.where(rid == 1, gate, 0.0))
        return carry

    lax.fori_loop(0, aff_ref.shape[0], expert_body, 0)


def _select(aff, cap):
    ne, nb, _ = aff.shape
    return pl.pallas_call(
        functools.partial(_select_kernel, cap=cap),
        grid=(1,),
        in_specs=[pl.BlockSpec(aff.shape, lambda i: (0, 0, 0))],
        out_specs=pl.BlockSpec((ne, SLOT_ROWS, cap), lambda i: (0, 0, 0)),
        out_shape=jax.ShapeDtypeStruct((ne, SLOT_ROWS, cap), F32),
        scratch_shapes=[pltpu.VMEM((nb, LANES), I32), pltpu.VMEM((nb, LANES), I32),
                        pltpu.VMEM((SLOT_ROWS, -(-cap // LANES) * LANES + 2 * LANES), F32)],
        compiler_params=_cparams("arbitrary"),
        name="select",
    )(aff)


def _ffn_kernel(idx_ref, h_hbm, wg_ref, wu_ref, wd_ref, gate_ref, moe_in_hbm, moe_hbm,
                xf_ref, xb_ref, acc_ref, rows_ref, sems):
    del moe_in_hbm
    e = pl.program_id(0)
    f = pl.program_id(1)
    n_e = pl.num_programs(0)
    n_f = pl.num_programs(1)
    cap = xf_ref.shape[0]

    def x_copy(ex, p):
        row = idx_ref[ex * cap + p]
        return pltpu.make_async_copy(h_hbm.at[pl.ds(row, 1)], xf_ref.at[pl.ds(p, 1)], sems.at[0])

    def acc_copy(p):
        row = idx_ref[e * cap + p]
        return pltpu.make_async_copy(moe_hbm.at[pl.ds(row, 1)], rows_ref.at[pl.ds(p, 1)], sems.at[1])

    def out_copy(ex, p):
        row = idx_ref[ex * cap + p]
        return pltpu.make_async_copy(rows_ref.at[pl.ds(p, 1)], moe_hbm.at[pl.ds(row, 1)], sems.at[2])

    def for_slots(fn):
        def body(p, c):
            fn(p)
            return c
        lax.fori_loop(0, cap, body, 0, unroll=8)

    @pl.when(f == 0)
    def _():
        @pl.when(e == 0)
        def _():
            for_slots(lambda p: x_copy(e, p).start())

        for_slots(lambda p: x_copy(e, p).wait())
        xb_ref[...] = xf_ref[...].astype(BF16)

        @pl.when(e + 1 < n_e)
        def _():
            for_slots(lambda p: x_copy(e + 1, p).start())

        acc_ref[...] = jnp.zeros_like(acc_ref)

    xe = xb_ref[...]
    hid = _silu(_dot1(xe, wg_ref[0])) * _dot1(xe, wu_ref[0])
    acc_ref[...] += _dot1(hid, wd_ref[0])

    @pl.when(f == n_f // 2)
    def _():
        @pl.when(e > 0)
        def _():
            for_slots(lambda p: out_copy(e - 1, p).wait())

        for_slots(lambda p: acc_copy(p).start())

    @pl.when(f == n_f - 1)
    def _():
        for_slots(lambda p: acc_copy(p).wait())
        rows_ref[...] += acc_ref[...] * gate_ref[0]
        for_slots(lambda p: out_copy(e, p).start())

        @pl.when(e == n_e - 1)
        def _():
            for_slots(lambda p: out_copy(e, p).wait())


def _expert_ffn(idx, gate, h2, w_gate, w_up, w_down):
    ne, cap, _ = gate.shape
    n_tok, d = h2.shape
    ff = w_gate.shape[-1]
    tf = _pick(ff, 256, LANES)
    n_f = ff // tf
    assert n_f >= 2, "accumulator rows are fetched mid-way through an expert's steps"
    any_spec = pl.BlockSpec(memory_space=pl.ANY)
    grid_spec = pltpu.PrefetchScalarGridSpec(
        num_scalar_prefetch=1,
        grid=(ne, n_f),
        in_specs=[any_spec,
                  pl.BlockSpec((1, d, tf), lambda e, f, idx_ref: (e, 0, f)),
                  pl.BlockSpec((1, d, tf), lambda e, f, idx_ref: (e, 0, f)),
                  pl.BlockSpec((1, tf, d), lambda e, f, idx_ref: (e, f, 0)),
                  pl.BlockSpec((1, cap, 1), lambda e, f, idx_ref: (e, 0, 0)),
                  any_spec],
        out_specs=any_spec,
        scratch_shapes=[pltpu.VMEM((cap, d), F32), pltpu.VMEM((cap, d), BF16),
                        pltpu.VMEM((cap, d), F32), pltpu.VMEM((cap, d), F32),
                        pltpu.SemaphoreType.DMA((3,))])
    return pl.pallas_call(
        _ffn_kernel,
        grid_spec=grid_spec,
        out_shape=jax.ShapeDtypeStruct((n_tok, d), F32),
        input_output_aliases={6: 0},
        compiler_params=_cparams("arbitrary", "arbitrary"),
        name="expert_ffn",
    )(idx, h2, w_gate, w_up, w_down, gate, jnp.zeros((n_tok, d), F32))


def _residual_out_kernel(x_ref, m_ref, g_ref, fg_ref, o_ref, *, final_norm):
    x = x_ref[0] + g_ref[0] * m_ref[0]
    if final_norm:
        x = x * lax.rsqrt(jnp.mean(x * x, axis=-1, keepdims=True) + EPS) * fg_ref[...]
    o_ref[0] = x


def _residual_out(x, moe, gate2, final_g, final_norm):
    bsz, t, d = x.shape
    tt = _pick(t, 512, SUBLANES)
    tile = pl.BlockSpec((1, tt, d), lambda b, i: (b, i, 0))
    return pl.pallas_call(
        functools.partial(_residual_out_kernel, final_norm=final_norm),
        grid=(bsz, t // tt),
        in_specs=[tile, tile, pl.BlockSpec((1, 1, d), lambda b, i: (b, 0, 0)),
                  pl.BlockSpec((1, d), lambda b, i: (0, 0))],
        out_specs=tile,
        out_shape=jax.ShapeDtypeStruct((bsz, t, d), F32),
        compiler_params=_cparams("arbitrary", "arbitrary"),
        name="residual_out",
    )(x, moe, gate2, final_g.reshape(1, d))


def _encoder_layer(x, ada, lw, lambda_init, final_g, final_norm):
    bsz, t, d = x.shape
    shift1, scale1, gate1, shift2, scale2, gate2 = [a[:, None, :] for a in jnp.split(ada, 6, axis=-1)]

    h = _norm_mod(x, lw["norm1_g"], scale1, shift1).reshape(bsz * t, d)
    z_rkv = _matmul(h, lw["w_rkv"]).reshape(bsz, t, -1)
    z_lo = _matmul(h, lw["w_lo"]).reshape(bsz, t, -1)
    z_qkv = _matmul(h, lw["w_qkv"]).reshape(bsz, t, -1)
    z_gates = _matmul(h, lw["w_gates"]).reshape(bsz, t, -1)

    (r, v, g, bonus, kd_f, b_f, an_f, cum_f, kd_b, b_b, an_b, cum_b) = _rw_prep(
        z_rkv, z_lo, lw["mu_r"], lw["mu_l"], lw["w0"], lw["w2"], lw["a0"], lw["a2"], lw["g2"],
        lw["k_k"], lw["k_a"], lw["r_k"])
    y_f, y_b = _rw_scan((r, v, kd_f, b_f, an_f, cum_f), (r, v, kd_b, b_b, an_b, cum_b))

    att = _diff_attention(z_qkv, lw["lam"], lw["sub_g"], lambda_init)
    merged = _merge(y_f, y_b, bonus, g, att, z_gates, lw["lnx_w"], lw["lnx_b"])
    x1 = _matmul_residual(merged, lw["w_out"], x, gate1)

    n_tok = bsz * t
    ne = lw["w_router_t"].shape[0]
    cap = max(1, CAPACITY_FACTOR * n_tok // ne)
    h2, aff = _router(x1, lw["norm2_g"], scale2, shift2, lw["w_router_t"])
    slots = _select(aff, cap)
    idx = slots[:, 0, :].astype(I32).reshape(ne * cap)
    gate = slots[:, 1, :].reshape(ne, cap, 1)
    moe = _expert_ffn(idx, gate, h2.reshape(n_tok, d), lw["w_gate"], lw["w_up"], lw["w_down"])
    return _residual_out(x1, moe.reshape(bsz, t, d), gate2, final_g, final_norm)


def kernel(x_prompt, x_sample, c_prompt, c_sample, w_ada, b_ada, norm1_g, w_in, mu_shift, w0, w2, a0, a2, g2, k_k, k_a, r_k, lnx_w, lnx_b, lam_q1, lam_k1, lam_q2, lam_k2, sub_g, w_out, norm2_g, w_router, w_gate, w_up, w_down, final_g):
    depth = w_ada.shape[0]
    d = x_prompt.shape[-1]
    w3 = 3 * d
    n_lo = 2 * w2.shape[2] + 2 * a2.shape[2] + g2.shape[1]
    xs = [x_prompt, x_sample]
    n_rows = [x_prompt.shape[0], x_sample.shape[0]]
    c_all = jnp.concatenate([c_prompt, c_sample], axis=0)
    pad = (-c_all.shape[0]) % SUBLANES
    c_all = jnp.pad(c_all, ((0, pad), (0, 0)))
    for l in range(depth):
        lambda_init = 0.8 - 0.6 * math.exp(-0.3 * l)
        w_in_l = w_in[l].astype(BF16)
        lam = (jnp.exp(jnp.sum(lam_q1[l] * lam_k1[l])) - jnp.exp(jnp.sum(lam_q2[l] * lam_k2[l]))
               + lambda_init).astype(F32).reshape(1)
        lw = dict(
            norm1_g=norm1_g[l], norm2_g=norm2_g[l],
            w_rkv=w_in_l[:, :w3], w_lo=w_in_l[:, w3:w3 + n_lo],
            w_qkv=w_in_l[:, w3 + n_lo:w3 + n_lo + 3 * d], w_gates=w_in_l[:, w3 + n_lo + 3 * d:],
            mu_r=mu_shift[l, :w3], mu_l=mu_shift[l, w3:w3 + n_lo],
            w0=w0[l], w2=w2[l], a0=a0[l], a2=a2[l], g2=g2[l], k_k=k_k[l], k_a=k_a[l], r_k=r_k[l],
            lnx_w=lnx_w[l], lnx_b=lnx_b[l], lam=lam, sub_g=sub_g[l],
            w_out=w_out[l].astype(BF16), w_router_t=w_router[l].T,
            w_gate=w_gate[l], w_up=w_up[l], w_down=w_down[l])
        ada = _ada(c_all, w_ada[l], b_ada[l])
        off = 0
        for gi in range(2):
            xs[gi] = _encoder_layer(xs[gi], ada[off:off + n_rows[gi]], lw, lambda_init, final_g,
                                    final_norm=(l == depth - 1))
            off += n_rows[gi]
    return tuple(xs)
```

```python
import functools
import math

import jax
import jax.numpy as jnp
from jax import lax
from jax.experimental import pallas as pl
from jax.experimental.pallas import tpu as pltpu

F32 = jnp.float32
BF16 = jnp.bfloat16
I32 = jnp.int32

RW_HEAD = 64
DA_QK = 64
DA_V = 2 * DA_QK
RW_CHUNK = 64
CAPACITY_FACTOR = 2
EPS = 1e-6
GN_EPS = RW_HEAD * 1e-5
LANES = 128
SUBLANES = 8
VMEM_LIMIT = 56 * 1024 * 1024


def _cparams(*sem):
    return pltpu.CompilerParams(dimension_semantics=sem, vmem_limit_bytes=VMEM_LIMIT)


def _pick(n, pref, align):
    if n <= pref:
        return n
    t = (pref // align) * align
    while t >= align:
        if n % t == 0:
            return t
        t -= align
    return n


_NN = (((1,), (0,)), ((), ()))
_NT = (((1,), (1,)), ((), ()))
_TN = (((0,), (0,)), ((), ()))


def _dg(a, b, dims):
    return lax.dot_general(a, b, dims, preferred_element_type=F32)


def _dot1(a, b, dims=_NN):
    return _dg(a.astype(BF16), b.astype(BF16), dims)


def _split(x):
    hi = x.astype(BF16)
    lo = (x - hi.astype(F32)).astype(BF16)
    return hi, lo


def _dot3(a, b, dims=_NN):
    ah, al = _split(a)
    bh, bl = _split(b)
    return _dg(ah, bh, dims) + (_dg(ah, bl, dims) + _dg(al, bh, dims))


def _dot_exact_rhs(a, b_bf16, dims=_NN):
    ah, al = _split(a)
    return _dg(ah, b_bf16, dims) + _dg(al, b_bf16, dims)


def _sigmoid(x):
    return 1.0 / (1.0 + jnp.exp(-x))


def _silu(x):
    return x * _sigmoid(x)


def _ada_kernel(c_ref, w_ref, b_ref, o_ref):
    o_ref[...] = _dot1(_silu(c_ref[...]), w_ref[...]) + b_ref[...]


def _ada(c, w, b):
    m, d = c.shape
    n = w.shape[1]
    tn = _pick(n, 1024, LANES)
    return pl.pallas_call(
        _ada_kernel,
        grid=(n // tn,),
        in_specs=[pl.BlockSpec((m, d), lambda j: (0, 0)),
                  pl.BlockSpec((d, tn), lambda j: (0, j)),
                  pl.BlockSpec((1, tn), lambda j: (0, j))],
        out_specs=pl.BlockSpec((m, tn), lambda j: (0, j)),
        out_shape=jax.ShapeDtypeStruct((m, n), F32),
        compiler_params=_cparams("arbitrary"),
        name="ada",
    )(c, w, b.reshape(1, n))


def _norm_mod_kernel(x_ref, g_ref, sc_ref, sh_ref, o_ref):
    x = x_ref[0]
    y = x * lax.rsqrt(jnp.mean(x * x, axis=-1, keepdims=True) + EPS) * g_ref[...]
    o_ref[0] = (y * (1.0 + sc_ref[0]) + sh_ref[0]).astype(o_ref.dtype)


def _norm_mod(x, g, scale, shift):
    bsz, t, d = x.shape
    tt = _pick(t, 512, SUBLANES)
    row = pl.BlockSpec((1, 1, d), lambda b, i: (b, 0, 0))
    return pl.pallas_call(
        _norm_mod_kernel,
        grid=(bsz, t // tt),
        in_specs=[pl.BlockSpec((1, tt, d), lambda b, i: (b, i, 0)),
                  pl.BlockSpec((1, d), lambda b, i: (0, 0)), row, row],
        out_specs=pl.BlockSpec((1, tt, d), lambda b, i: (b, i, 0)),
        out_shape=jax.ShapeDtypeStruct((bsz, t, d), BF16),
        compiler_params=_cparams("arbitrary", "arbitrary"),
        name="norm_mod",
    )(x, g.reshape(1, d), scale, shift)


def _mm_kernel(a_ref, b_ref, o_ref):
    o_ref[...] = _dot1(a_ref[...], b_ref[...]).astype(o_ref.dtype)


def _matmul(a, b, out_dtype=F32, tm_pref=1024, tn_pref=1024):
    m, k = a.shape
    n = b.shape[1]
    tm = _pick(m, tm_pref, SUBLANES)
    tn = _pick(n, tn_pref, LANES)
    return pl.pallas_call(
        _mm_kernel,
        grid=(m // tm, n // tn),
        in_specs=[pl.BlockSpec((tm, k), lambda i, j: (i, 0)),
                  pl.BlockSpec((k, tn), lambda i, j: (0, j))],
        out_specs=pl.BlockSpec((tm, tn), lambda i, j: (i, j)),
        out_shape=jax.ShapeDtypeStruct((m, n), out_dtype),
        compiler_params=_cparams("arbitrary", "arbitrary"),
        name="matmul",
    )(a, b)


def _mm_res_kernel(a_ref, b_ref, x_ref, g_ref, o_ref):
    o_ref[0] = x_ref[0] + g_ref[0] * _dot1(a_ref[0], b_ref[...])


def _matmul_residual(a, b, x, gate):
    bsz, t, k = a.shape
    n = b.shape[1]
    tm = _pick(t, 1024, SUBLANES)
    tn = _pick(n, 1024, LANES)
    return pl.pallas_call(
        _mm_res_kernel,
        grid=(bsz, t // tm, n // tn),
        in_specs=[pl.BlockSpec((1, tm, k), lambda bb, i, j: (bb, i, 0)),
                  pl.BlockSpec((k, tn), lambda bb, i, j: (0, j)),
                  pl.BlockSpec((1, tm, tn), lambda bb, i, j: (bb, i, j)),
                  pl.BlockSpec((1, 1, tn), lambda bb, i, j: (bb, 0, j))],
        out_specs=pl.BlockSpec((1, tm, tn), lambda bb, i, j: (bb, i, j)),
        out_shape=jax.ShapeDtypeStruct((bsz, t, n), F32),
        compiler_params=_cparams("arbitrary", "arbitrary", "arbitrary"),
        name="matmul_residual",
    )(a, b, x, gate)


def _shift_tile(z, prev8, next8, mu, i, n_t):
    tt = z.shape[0]
    pr = jnp.where(i > 0, prev8[SUBLANES - 1:SUBLANES, :], 0.0)
    nx = jnp.where(i < n_t - 1, next8[0:1, :], 0.0)
    ridx = lax.broadcasted_iota(I32, z.shape, 0)
    zp = jnp.where(ridx == 0, pr, pltpu.roll(z, 1, 0))
    zn = jnp.where(ridx == tt - 1, nx, pltpu.roll(z, tt - 1, 0))
    return z + mu * (0.5 * (zp + zn) - z)


def _seg_sum(x, e_ref, et_ref):
    s = _dot_exact_rhs(x, e_ref[...])
    return _dot_exact_rhs(s, et_ref[...])


def _rw_prep_kernel(zr_ref, zrp_ref, zrn_ref, zl_ref, zlp_ref, zln_ref, mur_ref, mul_ref,
                    w0_ref, w2_ref, a0_ref, a2_ref, g2_ref, kk_ref, ka_ref, rk_ref,
                    e_ref, et_ref, lf_ref, lb_ref,
                    r_o, v_o, g_o, bonus_o,
                    kdf_o, bf_o, anf_o, cumf_o, kdb_o, bb_o, anb_o, cumb_o, *, dl, il):
    i = pl.program_id(1)
    n_t = pl.num_programs(1)
    w = r_o.shape[-1]
    zs = _shift_tile(zr_ref[0], zrp_ref[0], zrn_ref[0], mur_ref[...], i, n_t)
    zl = _shift_tile(zl_ref[0], zlp_ref[0], zln_ref[0], mul_ref[...], i, n_t)
    r = zs[:, 0:w]
    k = zs[:, w:2 * w]
    v = zs[:, 2 * w:3 * w]
    gd = zl[:, 2 * dl + 2 * il:]
    kkr = k * kk_ref[...]
    kk = kkr * lax.rsqrt(jnp.maximum(_seg_sum(kkr * kkr, e_ref, et_ref), 1e-24))
    r_o[0] = r
    v_o[0] = v
    g_o[0] = _dot1(_sigmoid(gd), g2_ref[...])
    kd_sum = None
    outs = ((kdf_o, bf_o, anf_o, cumf_o, lf_ref), (kdb_o, bb_o, anb_o, cumb_o, lb_ref))
    for z, (kd_o, b_o, an_o, cum_o, l_ref) in enumerate(outs):
        wd = zl[:, z * dl:(z + 1) * dl]
        ad = zl[:, 2 * dl + z * il:2 * dl + (z + 1) * il]
        wl = w0_ref[z:z + 1, :] + _dot1(jnp.tanh(wd), w2_ref[z])
        lw = -math.exp(-0.5) * _sigmoid(wl)
        icl = _sigmoid(a0_ref[z:z + 1, :] + _dot1(ad, a2_ref[z]))
        kd = k * (1.0 + (icl - 1.0) * ka_ref[...])
        kd_o[0] = kd
        b_o[0] = kk * icl
        an_o[0] = -kk * jnp.exp(-lw)
        lh = lw.astype(BF16)
        rem = lw - lh.astype(F32)
        lm = rem.astype(BF16)
        ll = (rem - lm.astype(F32)).astype(BF16)
        lmat = l_ref[...]
        cum_o[0] = _dg(lmat, lh, _NN) + (_dg(lmat, lm, _NN) + _dg(lmat, ll, _NN))
        kd_sum = kd if kd_sum is None else kd_sum + kd
    bonus_o[0] = _seg_sum(r * kd_sum * rk_ref[...], e_ref, et_ref) * v


def _rw_prep(z_rkv, z_lo, mu_r, mu_l, w0, w2, a0, a2, g2, k_k, k_a, r_k):
    bsz, t, w3 = z_rkv.shape
    w = w3 // 3
    lw_cols = z_lo.shape[-1]
    dl, il = w2.shape[1], a2.shape[1]
    nh = w // RW_HEAD
    tt = _pick(t, 128, RW_CHUNK)
    n_t = t // tt
    hb = tt // SUBLANES
    head = jnp.arange(w, dtype=I32) // RW_HEAD
    e = (head[:, None] == jnp.arange(nh, dtype=I32)[None, :]).astype(BF16)
    et = e.T
    ti = jnp.arange(tt, dtype=I32)
    same = (ti[:, None] // RW_CHUNK) == (ti[None, :] // RW_CHUNK)
    lf = (same & (ti[None, :] <= ti[:, None])).astype(BF16)
    lb = (same & (ti[None, :] >= ti[:, None])).astype(BF16)

    def tile(c):
        return pl.BlockSpec((1, tt, c), lambda b, i: (b, i, 0))

    def prev(c):
        return pl.BlockSpec((1, SUBLANES, c), lambda b, i: (b, jnp.maximum(i * hb - 1, 0), 0))

    def nxt(c):
        return pl.BlockSpec((1, SUBLANES, c),
                            lambda b, i: (b, jnp.minimum((i + 1) * hb, t // SUBLANES - 1), 0))

    def full(a):
        nd = a.ndim
        return pl.BlockSpec(a.shape, lambda b, i: (0,) * nd)

    consts = [mu_r.reshape(1, w3), mu_l.reshape(1, lw_cols), w0, w2, a0, a2, g2,
              k_k.reshape(1, w), k_a.reshape(1, w), r_k.reshape(1, w), e, et, lf, lb]
    out = jax.ShapeDtypeStruct((bsz, t, w), F32)
    return pl.pallas_call(
        functools.partial(_rw_prep_kernel, dl=dl, il=il),
        grid=(bsz, n_t),
        in_specs=[tile(w3), prev(w3), nxt(w3), tile(lw_cols), prev(lw_cols), nxt(lw_cols)]
        + [full(a) for a in consts],
        out_specs=[tile(w)] * 12,
        out_shape=[out] * 12,
        compiler_params=_cparams("arbitrary", "arbitrary"),
        name="rw_prep",
    )(z_rkv, z_rkv, z_rkv, z_lo, z_lo, z_lo, *consts)


QUAD = 4


def _scan_quad_kernel(*refs):
    n_in = 6
    ins = (refs[:n_in], refs[n_in:2 * n_in])
    y_refs = refs[2 * n_in:2 * n_in + 2]
    h_refs = refs[2 * n_in + 2:]
    c = RW_CHUNK
    n = RW_HEAD
    qw = QUAD * n
    nq = h_refs[0].shape[1] // qw

    @pl.when(pl.program_id(1) == 0)
    def _():
        for h_ref in h_refs:
            h_ref[...] = jnp.zeros_like(h_ref)

    row = lax.broadcasted_iota(I32, (c, qw), 0)
    col = lax.broadcasted_iota(I32, (c, qw), 1) % n
    strict_d = (col < row, col > row)
    incl_d = (col <= row, col >= row)
    last_d = (c - 1, 0)
    eye = row == col
    eye_f = jnp.where(eye, 1.0, 0.0)
    bd_mask = (lax.broadcasted_iota(I32, (QUAD * c, qw), 0) // c
               == lax.broadcasted_iota(I32, (QUAD * c, qw), 1) // n)
    first_of_pair = lax.broadcasted_iota(I32, (c, 2 * n), 1) < n

    def bd(x):
        xb = x.astype(BF16)
        return jnp.where(bd_mask, jnp.concatenate([xb] * QUAD, axis=0), jnp.zeros((), BF16))

    def mm(a, b_bf16, dims=_NN):
        return _dg(a.astype(BF16), b_bf16, dims)

    chains = [(d, slice(qw * q, qw * (q + 1))) for d in range(2) for q in range(nq)]

    def load(i):
        return [ins[d][i][0, :, s] for d, s in chains]

    r_in, vs, kd_in, b_in, an_in, cum = (load(i) for i in range(n_in))
    e1 = [jnp.exp(x) for x in cum]
    e2 = [jnp.exp(-x) for x in cum]
    et = [jnp.exp(x[last_d[d]:last_d[d] + 1, :]) for x, (d, _) in zip(cum, chains)]
    rt = [x * e for x, e in zip(r_in, e1)]
    at = [x * e for x, e in zip(an_in, e1)]
    bt = [x * e for x, e in zip(b_in, e2)]
    kt = [x * e for x, e in zip(kd_in, e2)]
    bh = [x * e for x, e in zip(bt, et)]
    kh = [x * e for x, e in zip(kt, et)]
    hs = [h_refs[d][:, s] for d, s in chains]
    strict = [strict_d[d] for d, _ in chains]
    incl = [incl_d[d] for d, _ in chains]
    g = [mm(jnp.concatenate([a, r_], axis=0), jnp.concatenate([bd(b_), bd(k_)], axis=0), _NT)
         for a, r_, b_, k_ in zip(at, rt, bt, kt)]
    nab = [jnp.where(m, x[:c, :qw], 0.0) for m, x in zip(strict, g)]
    aak = [jnp.where(m, x[:c, qw:], 0.0) for m, x in zip(strict, g)]
    mb = [jnp.where(m, x[c:, :qw], 0.0) for m, x in zip(incl, g)]
    mk = [jnp.where(m, x[c:, qw:], 0.0) for m, x in zip(incl, g)]
    akv = [mm(jnp.concatenate([a, k_], axis=0), bd(x)) for a, k_, x in zip(aak, mk, vs)]
    av = [x[:c] for x in akv]
    mkv = [x[c:] for x in akv]
    tm = [eye_f + x for x in nab]
    npow = [mm(x, bd(x)) for x in nab]
    for _ in range(4):
        both = [mm(jnp.concatenate([x, t_], axis=0), bd(x)) for x, t_ in zip(npow, tm)]
        npow = [x[:c] for x in both]
        tm = [t_ + x[c:] for t_, x in zip(tm, both)]
    tm = [t_ + mm(t_, bd(x)) for t_, x in zip(tm, npow)]
    za = [mm(t_, bd(a)) for t_, a in zip(tm, at)]
    zw = [mm(t_, bd(a)) for t_, a in zip(tm, av)]
    rp = [r_ + mm(m, bd(z)) for r_, m, z in zip(rt, mb, za)]
    yi = [mm(m, bd(z)) + x for m, z, x in zip(mb, zw, mkv)]
    pm, qq = [], []
    for q in range(len(chains)):
        pms, qqs = [], []
        for pr in range(QUAD // 2):
            ps = slice(2 * n * pr, 2 * n * (pr + 1))
            lhs_t = jnp.concatenate([bh[q][:, ps], kh[q][:, ps]], axis=0)
            rhs = jnp.concatenate(
                [jnp.concatenate([za[q][:, ps], zw[q][:, ps]], axis=1),
                 jnp.concatenate([jnp.zeros((c, 2 * n), F32), vs[q][:, ps]], axis=1)], axis=0)
            m = _dot1(lhs_t, rhs, _TN)
            pms.append(jnp.where(first_of_pair, m[:n, :2 * n], m[n:, :2 * n]))
            qqs.append(jnp.where(first_of_pair, m[:n, 2 * n:], m[n:, 2 * n:]))
        pm.append(jnp.concatenate(pms, axis=1) + jnp.where(eye, et[q], 0.0))
        qq.append(jnp.concatenate(qqs, axis=1))
    o3 = [mm(jnp.concatenate([r_, p_], axis=0), bd(h_)) for r_, p_, h_ in zip(rp, pm, hs)]
    for q, (d, s) in enumerate(chains):
        y_refs[d][0, :, s] = o3[q][:c] + yi[q]
        h_refs[d][:, s] = o3[q][c:] + qq[q]


def _rw_scan(fwd, bwd):
    bsz, t, w = fwd[0].shape
    nc = t // RW_CHUNK
    assert w % (QUAD * RW_HEAD) == 0
    spec_f = pl.BlockSpec((1, RW_CHUNK, w), lambda bb, ci: (bb, ci, 0))
    spec_b = pl.BlockSpec((1, RW_CHUNK, w), lambda bb, ci: (bb, nc - 1 - ci, 0))
    out = jax.ShapeDtypeStruct((bsz, t, w), F32)
    return pl.pallas_call(
        _scan_quad_kernel,
        grid=(bsz, nc),
        in_specs=[spec_f] * 6 + [spec_b] * 6,
        out_specs=[spec_f, spec_b],
        out_shape=[out, out],
        scratch_shapes=[pltpu.VMEM((RW_HEAD, w), F32), pltpu.VMEM((RW_HEAD, w), F32)],
        compiler_params=_cparams("arbitrary", "arbitrary"),
        name="rw_scan",
    )(*fwd, *bwd)


def _merge_kernel(yf_ref, yb_ref, bonus_ref, g_ref, att_ref, ga_ref, gb_ref, lw_ref, lb_ref,
                  e_ref, et_ref, o_ref):
    y = yf_ref[0] + yb_ref[0]
    mean = _seg_sum(y, e_ref, et_ref) * (1.0 / RW_HEAD)
    d = y - mean
    var = _seg_sum(d * d, e_ref, et_ref) * (1.0 / RW_HEAD)
    yn = d * lax.rsqrt(var + GN_EPS) * lw_ref[...] + lb_ref[...]
    ya = (yn + bonus_ref[0]) * g_ref[0]
    o_ref[0] = (_sigmoid(ga_ref[0]) * ya + _sigmoid(gb_ref[0]) * att_ref[0]).astype(o_ref.dtype)


def _merge(yf, yb, bonus, g, att, z_gates, lnx_w, lnx_b):
    bsz, t, w = yf.shape
    nh = w // RW_HEAD
    tt = _pick(t, 256, SUBLANES)
    head = jnp.arange(w, dtype=I32) // RW_HEAD
    e = (head[:, None] == jnp.arange(nh, dtype=I32)[None, :]).astype(BF16)
    et = e.T
    tile = pl.BlockSpec((1, tt, w), lambda b, i: (b, i, 0))
    gate_b = pl.BlockSpec((1, tt, w), lambda b, i: (b, i, 1))
    vec = pl.BlockSpec((1, w), lambda b, i: (0, 0))
    return pl.pallas_call(
        _merge_kernel,
        grid=(bsz, t // tt),
        in_specs=[tile, tile, tile, tile, tile, tile, gate_b, vec, vec,
                  pl.BlockSpec(e.shape, lambda b, i: (0, 0)),
                  pl.BlockSpec(et.shape, lambda b, i: (0, 0))],
        out_specs=tile,
        out_shape=jax.ShapeDtypeStruct((bsz, t, w), BF16),
        compiler_params=_cparams("arbitrary", "arbitrary"),
        name="merge",
    )(yf, yb, bonus, g, att, z_gates, z_gates, lnx_w.reshape(1, w), lnx_b.reshape(1, w), e, et)


LOG2E = math.log2(math.e)


MIN_ROW_SUM = 2.0 ** -60


def _attn_kernel(slope_ref, lam_ref, q_ref, k_ref, v_ref, sg_ref, d_ref, o_ref, kb_ref, vb_ref,
                 kn_ref, *, out_scale):
    h = pl.program_id(1)
    qi = pl.program_id(2)
    tq = q_ref.shape[1]
    t = k_ref.shape[1]

    @pl.when(qi == 0)
    def _():
        k = k_ref[0]
        for c in range(2):
            kc = k[:, c * DA_QK:(c + 1) * DA_QK].astype(BF16)
            kb_ref[c] = kc
            kf = kc.astype(F32)
            kn2 = jnp.max(jnp.sum(kf * kf, axis=1, keepdims=True), axis=0, keepdims=True)
            kn_ref[c] = jnp.broadcast_to(kn2, kn_ref.shape[1:])
        vb_ref[...] = v_ref[0].astype(BF16)

    start = pl.multiple_of((pl.num_programs(2) - 1 - qi) * tq, tq)
    bias = (-LOG2E * slope_ref[h]) * d_ref[:, pl.ds(start, t)]
    q = (q_ref[0] * (DA_QK ** -0.5 * LOG2E)).astype(BF16)
    qf = q.astype(F32)

    def softmax_pv(row_shift):
        outs, sums = [], []
        for c in range(2):
            cs = slice(c * DA_QK, (c + 1) * DA_QK)
            s = _dg(q[:, cs], kb_ref[c], _NT) + bias
            p = jnp.exp2(s - row_shift(c, s))
            sums.append(jnp.sum(p, axis=1, keepdims=True))
            outs.append(_dg(p.astype(BF16), vb_ref[...], _NN))
        return outs, sums

    def finish(outs, sums):
        o = outs[0] / sums[0] - lam_ref[0] * (outs[1] / sums[1])
        o = o * lax.rsqrt(jnp.mean(o * o, axis=-1, keepdims=True) + EPS) * sg_ref[...]
        o_ref[0] = o * out_scale

    def norm_bound(c, s):
        qc = qf[:, c * DA_QK:(c + 1) * DA_QK]
        qn2 = jnp.sum(qc * qc, axis=1, keepdims=True)
        return jnp.sqrt(qn2 * kn_ref[c][0:1, 0:1]) * (1.0 + 2.0 ** -7)

    outs, sums = softmax_pv(norm_bound)
    safe = jnp.minimum(jnp.min(sums[0]), jnp.min(sums[1])) >= MIN_ROW_SUM

    @pl.when(safe)
    def _():
        finish(outs, sums)

    @pl.when(jnp.logical_not(safe))
    def _():
        finish(*softmax_pv(lambda c, s: jnp.max(s, axis=1, keepdims=True)))


def _diff_attention(z_qkv, lam, sub_g, lambda_init):
    bsz, t, w3 = z_qkv.shape
    nh = w3 // 3 // DA_V
    tq = _pick(t, 512, LANES)
    slopes = 2.0 ** (-8.0 * jnp.arange(1, nh + 1, dtype=F32) / nh)
    dist = jnp.abs(jnp.arange(tq, dtype=I32)[:, None] + (t - tq)
                   - jnp.arange(2 * t - tq, dtype=I32)[None, :]).astype(F32)
    smem = pl.BlockSpec(memory_space=pltpu.SMEM)
    return pl.pallas_call(
        functools.partial(_attn_kernel, out_scale=1.0 - lambda_init),
        grid=(bsz, nh, t // tq),
        in_specs=[smem, smem,
                  pl.BlockSpec((1, tq, DA_V), lambda b, h, i: (b, i, h)),
                  pl.BlockSpec((1, t, DA_V), lambda b, h, i: (b, 0, nh + h)),
                  pl.BlockSpec((1, t, DA_V), lambda b, h, i: (b, 0, 2 * nh + h)),
                  pl.BlockSpec((1, DA_V), lambda b, h, i: (0, 0)),
                  pl.BlockSpec(dist.shape, lambda b, h, i: (0, 0), pipeline_mode=pl.Buffered(1))],
        out_specs=pl.BlockSpec((1, tq, DA_V), lambda b, h, i: (b, i, h)),
        out_shape=jax.ShapeDtypeStruct((bsz, t, nh * DA_V), F32),
        scratch_shapes=[pltpu.VMEM((2, t, DA_QK), BF16), pltpu.VMEM((t, DA_V), BF16),
                        pltpu.VMEM((2, SUBLANES, LANES), F32)],
        compiler_params=_cparams("arbitrary", "arbitrary", "arbitrary"),
        name="diff_attention",
    )(slopes, lam, z_qkv, z_qkv, z_qkv, sub_g.reshape(1, DA_V), dist)


def _router_kernel(x_ref, g_ref, sc_ref, sh_ref, wr_ref, h_ref, aff_ref):
    x = x_ref[0]
    y = x * lax.rsqrt(jnp.mean(x * x, axis=-1, keepdims=True) + EPS) * g_ref[...]
    h = y * (1.0 + sc_ref[0]) + sh_ref[0]
    h_ref[0] = h.astype(h_ref.dtype)
    logits = _dot3(wr_ref[...], h, _NT)
    m = jnp.max(logits, axis=0, keepdims=True)
    ex = jnp.exp(logits - m)
    aff = ex / jnp.sum(ex, axis=0, keepdims=True)
    for j in range(aff_ref.shape[1]):
        aff_ref[:, j, :] = aff[:, j * LANES:(j + 1) * LANES]


def _router(x, g, scale, shift, w_router_t):
    bsz, t, d = x.shape
    ne = w_router_t.shape[0]
    tt = _pick(t, SUBLANES * LANES, SUBLANES * LANES)
    nb = tt // LANES
    n_t = t // tt
    row = pl.BlockSpec((1, 1, d), lambda b, i: (b, 0, 0))
    return pl.pallas_call(
        _router_kernel,
        grid=(bsz, n_t),
        in_specs=[pl.BlockSpec((1, tt, d), lambda b, i: (b, i, 0)),
                  pl.BlockSpec((1, d), lambda b, i: (0, 0)), row, row,
                  pl.BlockSpec((ne, d), lambda b, i: (0, 0))],
        out_specs=[pl.BlockSpec((1, tt, d), lambda b, i: (b, i, 0)),
                   pl.BlockSpec((ne, nb, LANES), lambda b, i: (0, b * n_t + i, 0))],
        out_shape=[jax.ShapeDtypeStruct((bsz, t, d), F32),
                   jax.ShapeDtypeStruct((ne, bsz * t // LANES, LANES), F32)],
        compiler_params=_cparams("arbitrary", "arbitrary"),
        name="router",
    )(x, g.reshape(1, d), scale, shift, w_router_t)


def _sum_all(x):
    return jnp.sum(jnp.sum(x, axis=1, keepdims=True), axis=0, keepdims=True)


SLOT_ROWS = 8


def _select_kernel(aff_ref, slots_ref, pos_ref, base_ref, rec_ref, thr_ref, base_smem, sem, *, cap):
    nb = aff_ref.shape[1]
    li = lax.broadcasted_iota(I32, (LANES, LANES), 0)
    lj = lax.broadcasted_iota(I32, (LANES, LANES), 1)
    upper = jnp.where(li <= lj, 1.0, 0.0).astype(BF16)
    ri = lax.broadcasted_iota(I32, (nb, nb), 0)
    rj = lax.broadcasted_iota(I32, (nb, nb), 1)
    lower = jnp.where(rj < ri, 1.0, 0.0).astype(BF16)

    def excl_prefix(mask):
        x = jnp.where(mask, 1.0, 0.0)
        inc = _dg(x.astype(BF16), upper, _NN)
        tot = jnp.broadcast_to(inc[:, LANES - 1:LANES], (nb, LANES))
        offs = _dg(lower, tot.astype(BF16), _NN)
        return offs + inc - x, offs

    bits_all = pltpu.bitcast(aff_ref[...], I32)

    def bit_body(i, thr):
        cand = thr | jnp.left_shift(jnp.int32(1), 30 - i)
        cnt = jnp.sum(jnp.where(bits_all >= cand, 1.0, 0.0), axis=(1, 2), keepdims=True)
        return jnp.where(cnt >= cap, cand, thr)

    thr_all = lax.fori_loop(0, 31, bit_body, jnp.zeros((aff_ref.shape[0], 1, 1), I32))
    thr_ref[...] = jnp.broadcast_to(thr_all, thr_ref.shape)

    def expert_body(e, carry):
        bits = pltpu.bitcast(aff_ref[e], I32)
        thr = thr_ref[e][0:1, 0:1]
        gt = bits > thr
        eq = bits == thr
        need = cap - _sum_all(jnp.where(gt, 1.0, 0.0))
        sel = gt | (eq & (excl_prefix(eq)[0] < need))
        pos, row_base = excl_prefix(sel)
        pos_ref[...] = jnp.where(sel, pos, -1.0).astype(I32)
        base_ref[...] = row_base.astype(I32)
        to_smem = pltpu.make_async_copy(base_ref, base_smem, sem)
        to_smem.start()
        rec_ref[...] = jnp.zeros_like(rec_ref)
        to_smem.wait()
        window_ids = lax.broadcasted_iota(I32, (2 * LANES, LANES), 0)
        lane_f = lax.broadcasted_iota(I32, (1, LANES), 1).astype(F32)

        def group_body(g, c2):
            w0 = pl.multiple_of((base_smem[g, 0] // LANES) * LANES, LANES)
            a = aff_ref[e, pl.ds(g, 1), :]
            a_hi = a.astype(BF16).astype(F32)
            a_mid = (a - a_hi).astype(BF16).astype(F32)
            a_lo = a - a_hi - a_mid
            base = jnp.full((1, LANES), LANES, F32) * g
            rec = jnp.concatenate([lane_f, base, a_hi, a_mid, a_lo,
                                   jnp.zeros((SLOT_ROWS - 5, LANES), F32)], axis=0)
            hot = jnp.where(window_ids == pos_ref[pl.ds(g, 1), :] - w0, 1.0, 0.0)
            rec_ref[:, pl.ds(w0, 2 * LANES)] += _dot1(rec, hot, _NT)
            return c2

        lax.fori_loop(0, nb, group_body, 0, unroll=4)
        rec = rec_ref[:, :cap]
        rid = lax.broadcasted_iota(I32, (SLOT_ROWS, cap), 0)
        token = rec[0:1] + rec[1:2]
        gate = rec[2:3] + rec[3:4] + rec[4:5]
        slots_ref[e] = jnp.where(rid == 0, token, jnp.where(rid == 1, gate, 0.0))
        return carry

    lax.fori_loop(0, aff_ref.shape[0], expert_body, 0)


def _select(aff, cap):
    ne, nb, _ = aff.shape
    return pl.pallas_call(
        functools.partial(_select_kernel, cap=cap),
        grid=(1,),
        in_specs=[pl.BlockSpec(aff.shape, lambda i: (0, 0, 0))],
        out_specs=pl.BlockSpec((ne, SLOT_ROWS, cap), lambda i: (0, 0, 0)),
        out_shape=jax.ShapeDtypeStruct((ne, SLOT_ROWS, cap), F32),
        scratch_shapes=[pltpu.VMEM((nb, LANES), I32), pltpu.VMEM((nb, LANES), I32),
                        pltpu.VMEM((SLOT_ROWS, -(-cap // LANES) * LANES + 2 * LANES), F32),
                        pltpu.VMEM((ne, SUBLANES, LANES), I32),
                        pltpu.SMEM((nb, LANES), I32), pltpu.SemaphoreType.DMA(())],
        compiler_params=_cparams("arbitrary"),
        name="select",
    )(aff)


def _ffn_kernel(idx_ref, h_hbm, wg_ref, wu_ref, wd_ref, gate_ref, moe_in_hbm, moe_hbm,
                xf_ref, xb_ref, acc_ref, rows_ref, sems):
    del moe_in_hbm
    e = pl.program_id(0)
    f = pl.program_id(1)
    n_e = pl.num_programs(0)
    n_f = pl.num_programs(1)
    cap = xf_ref.shape[0]

    def x_copy(ex, p):
        row = idx_ref[ex * cap + p]
        return pltpu.make_async_copy(h_hbm.at[pl.ds(row, 1)], xf_ref.at[pl.ds(p, 1)], sems.at[0])

    def acc_copy(p):
        row = idx_ref[e * cap + p]
        return pltpu.make_async_copy(moe_hbm.at[pl.ds(row, 1)], rows_ref.at[pl.ds(p, 1)], sems.at[1])

    def out_copy(ex, p):
        row = idx_ref[ex * cap + p]
        return pltpu.make_async_copy(rows_ref.at[pl.ds(p, 1)], moe_hbm.at[pl.ds(row, 1)], sems.at[2])

    def for_slots(fn):
        def body(p, c):
            fn(p)
            return c
        lax.fori_loop(0, cap, body, 0, unroll=8)

    @pl.when(f == 0)
    def _():
        @pl.when(e == 0)
        def _():
            for_slots(lambda p: x_copy(e, p).start())

        for_slots(lambda p: x_copy(e, p).wait())
        xb_ref[...] = xf_ref[...].astype(BF16)

        @pl.when(e + 1 < n_e)
        def _():
            for_slots(lambda p: x_copy(e + 1, p).start())

        acc_ref[...] = jnp.zeros_like(acc_ref)

    xe = xb_ref[...]
    hid = _silu(_dot1(xe, wg_ref[0])) * _dot1(xe, wu_ref[0])
    acc_ref[...] += _dot1(hid, wd_ref[0])

    @pl.when(f == n_f // 2)
    def _():
        @pl.when(e > 0)
        def _():
            for_slots(lambda p: out_copy(e - 1, p).wait())

        for_slots(lambda p: acc_copy(p).start())

    @pl.when(f == n_f - 1)
    def _():
        for_slots(lambda p: acc_copy(p).wait())
        rows_ref[...] += acc_ref[...] * gate_ref[0]
        for_slots(lambda p: out_copy(e, p).start())

        @pl.when(e == n_e - 1)
        def _():
            for_slots(lambda p: out_copy(e, p).wait())


def _expert_ffn(idx, gate, h2, w_gate, w_up, w_down):
    ne, cap, _ = gate.shape
    n_tok, d = h2.shape
    ff = w_gate.shape[-1]
    tf = _pick(ff, 256, LANES)
    n_f = ff // tf
    assert n_f >= 2, "accumulator rows are fetched mid-way through an expert's steps"
    any_spec = pl.BlockSpec(memory_space=pl.ANY)
    grid_spec = pltpu.PrefetchScalarGridSpec(
        num_scalar_prefetch=1,
        grid=(ne, n_f),
        in_specs=[any_spec,
                  pl.BlockSpec((1, d, tf), lambda e, f, idx_ref: (e, 0, f)),
                  pl.BlockSpec((1, d, tf), lambda e, f, idx_ref: (e, 0, f)),
                  pl.BlockSpec((1, tf, d), lambda e, f, idx_ref: (e, f, 0)),
                  pl.BlockSpec((1, cap, 1), lambda e, f, idx_ref: (e, 0, 0)),
                  any_spec],
        out_specs=any_spec,
        scratch_shapes=[pltpu.VMEM((cap, d), F32), pltpu.VMEM((cap, d), BF16),
                        pltpu.VMEM((cap, d), F32), pltpu.VMEM((cap, d), F32),
                        pltpu.SemaphoreType.DMA((3,))])
    return pl.pallas_call(
        _ffn_kernel,
        grid_spec=grid_spec,
        out_shape=jax.ShapeDtypeStruct((n_tok, d), F32),
        input_output_aliases={6: 0},
        compiler_params=_cparams("arbitrary", "arbitrary"),
        name="expert_ffn",
    )(idx, h2, w_gate, w_up, w_down, gate, jnp.zeros((n_tok, d), F32))


def _residual_out_kernel(x_ref, m_ref, g_ref, fg_ref, o_ref, *, final_norm):
    x = x_ref[0] + g_ref[0] * m_ref[0]
    if final_norm:
        x = x * lax.rsqrt(jnp.mean(x * x, axis=-1, keepdims=True) + EPS) * fg_ref[...]
    o_ref[0] = x


def _residual_out(x, moe, gate2, final_g, final_norm):
    bsz, t, d = x.shape
    tt = _pick(t, 512, SUBLANES)
    tile = pl.BlockSpec((1, tt, d), lambda b, i: (b, i, 0))
    return pl.pallas_call(
        functools.partial(_residual_out_kernel, final_norm=final_norm),
        grid=(bsz, t // tt),
        in_specs=[tile, tile, pl.BlockSpec((1, 1, d), lambda b, i: (b, 0, 0)),
                  pl.BlockSpec((1, d), lambda b, i: (0, 0))],
        out_specs=tile,
        out_shape=jax.ShapeDtypeStruct((bsz, t, d), F32),
        compiler_params=_cparams("arbitrary", "arbitrary"),
        name="residual_out",
    )(x, moe, gate2, final_g.reshape(1, d))


def _encoder_layer(x, ada, lw, lambda_init, final_g, final_norm):
    bsz, t, d = x.shape
    shift1, scale1, gate1, shift2, scale2, gate2 = [a[:, None, :] for a in jnp.split(ada, 6, axis=-1)]

    h = _norm_mod(x, lw["norm1_g"], scale1, shift1).reshape(bsz * t, d)
    z_rkv = _matmul(h, lw["w_rkv"]).reshape(bsz, t, -1)
    z_lo = _matmul(h, lw["w_lo"]).reshape(bsz, t, -1)
    z_qkv = _matmul(h, lw["w_qkv"]).reshape(bsz, t, -1)
    z_gates = _matmul(h, lw["w_gates"]).reshape(bsz, t, -1)

    (r, v, g, bonus, kd_f, b_f, an_f, cum_f, kd_b, b_b, an_b, cum_b) = _rw_prep(
        z_rkv, z_lo, lw["mu_r"], lw["mu_l"], lw["w0"], lw["w2"], lw["a0"], lw["a2"], lw["g2"],
        lw["k_k"], lw["k_a"], lw["r_k"])
    y_f, y_b = _rw_scan((r, v, kd_f, b_f, an_f, cum_f), (r, v, kd_b, b_b, an_b, cum_b))

    att = _diff_attention(z_qkv, lw["lam"], lw["sub_g"], lambda_init)
    merged = _merge(y_f, y_b, bonus, g, att, z_gates, lw["lnx_w"], lw["lnx_b"])
    x1 = _matmul_residual(merged, lw["w_out"], x, gate1)

    n_tok = bsz * t
    ne = lw["w_router_t"].shape[0]
    cap = max(1, CAPACITY_FACTOR * n_tok // ne)
    h2, aff = _router(x1, lw["norm2_g"], scale2, shift2, lw["w_router_t"])
    slots = _select(aff, cap)
    idx = slots[:, 0, :].astype(I32).reshape(ne * cap)
    gate = slots[:, 1, :].reshape(ne, cap, 1)
    moe = _expert_ffn(idx, gate, h2.reshape(n_tok, d), lw["w_gate"], lw["w_up"], lw["w_down"])
    return _residual_out(x1, moe.reshape(bsz, t, d), gate2, final_g, final_norm)


def kernel(x_prompt, x_sample, c_prompt, c_sample, w_ada, b_ada, norm1_g, w_in, mu_shift, w0, w2, a0, a2, g2, k_k, k_a, r_k, lnx_w, lnx_b, lam_q1, lam_k1, lam_q2, lam_k2, sub_g, w_out, norm2_g, w_router, w_gate, w_up, w_down, final_g):
    depth = w_ada.shape[0]
    d = x_prompt.shape[-1]
    w3 = 3 * d
    n_lo = 2 * w2.shape[2] + 2 * a2.shape[2] + g2.shape[1]
    xs = [x_prompt, x_sample]
    n_rows = [x_prompt.shape[0], x_sample.shape[0]]
    c_all = jnp.concatenate([c_prompt, c_sample], axis=0)
    pad = (-c_all.shape[0]) % SUBLANES
    c_all = jnp.pad(c_all, ((0, pad), (0, 0)))
    for l in range(depth):
        lambda_init = 0.8 - 0.6 * math.exp(-0.3 * l)
        w_in_l = w_in[l].astype(BF16)
        lam = (jnp.exp(jnp.sum(lam_q1[l] * lam_k1[l])) - jnp.exp(jnp.sum(lam_q2[l] * lam_k2[l]))
               + lambda_init).astype(F32).reshape(1)
        lw = dict(
            norm1_g=norm1_g[l], norm2_g=norm2_g[l],
            w_rkv=w_in_l[:, :w3], w_lo=w_in_l[:, w3:w3 + n_lo],
            w_qkv=w_in_l[:, w3 + n_lo:w3 + n_lo + 3 * d], w_gates=w_in_l[:, w3 + n_lo + 3 * d:],
            mu_r=mu_shift[l, :w3], mu_l=mu_shift[l, w3:w3 + n_lo],
            w0=w0[l], w2=w2[l], a0=a0[l], a2=a2[l], g2=g2[l], k_k=k_k[l], k_a=k_a[l], r_k=r_k[l],
            lnx_w=lnx_w[l], lnx_b=lnx_b[l], lam=lam, sub_g=sub_g[l],
            w_out=w_out[l].astype(BF16), w_router_t=w_router[l].T,
            w_gate=w_gate[l], w_up=w_up[l], w_down=w_down[l])
        ada = _ada(c_all, w_ada[l], b_ada[l])
        off = 0
        for gi in range(2):
            xs[gi] = _encoder_layer(xs[gi], ada[off:off + n_rows[gi]], lw, lambda_init, final_g,
                                    final_norm=(l == depth - 1))
            off += n_rows[gi]
    return tuple(xs)
```

```python
import functools
import math

import jax
import jax.numpy as jnp
from jax import lax
from jax.experimental import pallas as pl
from jax.experimental.pallas import tpu as pltpu

F32 = jnp.float32
BF16 = jnp.bfloat16
I32 = jnp.int32

RW_HEAD = 64
DA_QK = 64
DA_V = 2 * DA_QK
RW_CHUNK = 64
CAPACITY_FACTOR = 2
EPS = 1e-6
GN_EPS = RW_HEAD * 1e-5
LANES = 128
SUBLANES = 8
VMEM_LIMIT = 56 * 1024 * 1024


def _cparams(*sem):
    return pltpu.CompilerParams(dimension_semantics=sem, vmem_limit_bytes=VMEM_LIMIT)


def _pick(n, pref, align):
    if n <= pref:
        return n
    t = (pref // align) * align
    while t >= align:
        if n % t == 0:
            return t
        t -= align
    return n


_NN = (((1,), (0,)), ((), ()))
_NT = (((1,), (1,)), ((), ()))
_TN = (((0,), (0,)), ((), ()))


def _dg(a, b, dims):
    return lax.dot_general(a, b, dims, preferred_element_type=F32)


def _dot1(a, b, dims=_NN):
    return _dg(a.astype(BF16), b.astype(BF16), dims)


def _split(x):
    hi = x.astype(BF16)
    lo = (x - hi.astype(F32)).astype(BF16)
    return hi, lo


def _dot3(a, b, dims=_NN):
    ah, al = _split(a)
    bh, bl = _split(b)
    return _dg(ah, bh, dims) + (_dg(ah, bl, dims) + _dg(al, bh, dims))


def _dot_exact_rhs(a, b_bf16, dims=_NN):
    ah, al = _split(a)
    return _dg(ah, b_bf16, dims) + _dg(al, b_bf16, dims)


def _sigmoid(x):
    return 1.0 / (1.0 + jnp.exp(-x))


def _silu(x):
    return x * _sigmoid(x)


def _ada_kernel(c_ref, w_ref, b_ref, o_ref):
    o_ref[...] = _dot1(_silu(c_ref[...]), w_ref[...]) + b_ref[...]


def _ada(c, w, b):
    m, d = c.shape
    n = w.shape[1]
    tn = _pick(n, 1024, LANES)
    return pl.pallas_call(
        _ada_kernel,
        grid=(n // tn,),
        in_specs=[pl.BlockSpec((m, d), lambda j: (0, 0)),
                  pl.BlockSpec((d, tn), lambda j: (0, j)),
                  pl.BlockSpec((1, tn), lambda j: (0, j))],
        out_specs=pl.BlockSpec((m, tn), lambda j: (0, j)),
        out_shape=jax.ShapeDtypeStruct((m, n), F32),
        compiler_params=_cparams("arbitrary"),
        name="ada",
    )(c, w, b.reshape(1, n))


def _norm_mod_kernel(x_ref, g_ref, sc_ref, sh_ref, o_ref):
    x = x_ref[0]
    y = x * lax.rsqrt(jnp.mean(x * x, axis=-1, keepdims=True) + EPS) * g_ref[...]
    o_ref[0] = (y * (1.0 + sc_ref[0]) + sh_ref[0]).astype(o_ref.dtype)


def _norm_mod(x, g, scale, shift):
    bsz, t, d = x.shape
    tt = _pick(t, 512, SUBLANES)
    row = pl.BlockSpec((1, 1, d), lambda b, i: (b, 0, 0))
    return pl.pallas_call(
        _norm_mod_kernel,
        grid=(bsz, t // tt),
        in_specs=[pl.BlockSpec((1, tt, d), lambda b, i: (b, i, 0)),
                  pl.BlockSpec((1, d), lambda b, i: (0, 0)), row, row],
        out_specs=pl.BlockSpec((1, tt, d), lambda b, i: (b, i, 0)),
        out_shape=jax.ShapeDtypeStruct((bsz, t, d), BF16),
        compiler_params=_cparams("arbitrary", "arbitrary"),
        name="norm_mod",
    )(x, g.reshape(1, d), scale, shift)


def _mm_kernel(a_ref, b_ref, o_ref):
    o_ref[...] = _dot1(a_ref[...], b_ref[...]).astype(o_ref.dtype)


def _matmul(a, b, out_dtype=F32, tm_pref=1024, tn_pref=1024):
    m, k = a.shape
    n = b.shape[1]
    tm = _pick(m, tm_pref, SUBLANES)
    tn = _pick(n, tn_pref, LANES)
    return pl.pallas_call(
        _mm_kernel,
        grid=(m // tm, n // tn),
        in_specs=[pl.BlockSpec((tm, k), lambda i, j: (i, 0)),
                  pl.BlockSpec((k, tn), lambda i, j: (0, j))],
        out_specs=pl.BlockSpec((tm, tn), lambda i, j: (i, j)),
        out_shape=jax.ShapeDtypeStruct((m, n), out_dtype),
        compiler_params=_cparams("arbitrary", "arbitrary"),
        name="matmul",
    )(a, b)


def _mm_res_kernel(a_ref, b_ref, x_ref, g_ref, o_ref):
    o_ref[0] = x_ref[0] + g_ref[0] * _dot1(a_ref[0], b_ref[...])


def _matmul_residual(a, b, x, gate):
    bsz, t, k = a.shape
    n = b.shape[1]
    tm = _pick(t, 1024, SUBLANES)
    tn = _pick(n, 1024, LANES)
    return pl.pallas_call(
        _mm_res_kernel,
        grid=(bsz, t // tm, n // tn),
        in_specs=[pl.BlockSpec((1, tm, k), lambda bb, i, j: (bb, i, 0)),
                  pl.BlockSpec((k, tn), lambda bb, i, j: (0, j)),
                  pl.BlockSpec((1, tm, tn), lambda bb, i, j: (bb, i, j)),
                  pl.BlockSpec((1, 1, tn), lambda bb, i, j: (bb, 0, j))],
        out_specs=pl.BlockSpec((1, tm, tn), lambda bb, i, j: (bb, i, j)),
        out_shape=jax.ShapeDtypeStruct((bsz, t, n), F32),
        compiler_params=_cparams("arbitrary", "arbitrary", "arbitrary"),
        name="matmul_residual",
    )(a, b, x, gate)


def _shift_tile(z, prev8, next8, mu, i, n_t):
    tt = z.shape[0]
    pr = jnp.where(i > 0, prev8[SUBLANES - 1:SUBLANES, :], 0.0)
    nx = jnp.where(i < n_t - 1, next8[0:1, :], 0.0)
    ridx = lax.broadcasted_iota(I32, z.shape, 0)
    zp = jnp.where(ridx == 0, pr, pltpu.roll(z, 1, 0))
    zn = jnp.where(ridx == tt - 1, nx, pltpu.roll(z, tt - 1, 0))
    return z + mu * (0.5 * (zp + zn) - z)


def _seg_sum(x, e_ref, et_ref):
    s = _dot_exact_rhs(x, e_ref[...])
    return _dot_exact_rhs(s, et_ref[...])


def _rw_prep_kernel(zr_ref, zrp_ref, zrn_ref, zl_ref, zlp_ref, zln_ref, mur_ref, mul_ref,
                    w0_ref, w2_ref, a0_ref, a2_ref, g2_ref, kk_ref, ka_ref, rk_ref,
                    e_ref, et_ref, lf_ref, lb_ref,
                    r_o, v_o, g_o, bonus_o,
                    kdf_o, bf_o, anf_o, cumf_o, kdb_o, bb_o, anb_o, cumb_o, *, dl, il):
    i = pl.program_id(1)
    n_t = pl.num_programs(1)
    w = r_o.shape[-1]
    zs = _shift_tile(zr_ref[0], zrp_ref[0], zrn_ref[0], mur_ref[...], i, n_t)
    zl = _shift_tile(zl_ref[0], zlp_ref[0], zln_ref[0], mul_ref[...], i, n_t)
    r = zs[:, 0:w]
    k = zs[:, w:2 * w]
    v = zs[:, 2 * w:3 * w]
    gd = zl[:, 2 * dl + 2 * il:]
    kkr = k * kk_ref[...]
    kk = kkr * lax.rsqrt(jnp.maximum(_seg_sum(kkr * kkr, e_ref, et_ref), 1e-24))
    r_o[0] = r
    v_o[0] = v
    g_o[0] = _dot1(_sigmoid(gd), g2_ref[...])
    kd_sum = None
    outs = ((kdf_o, bf_o, anf_o, cumf_o, lf_ref), (kdb_o, bb_o, anb_o, cumb_o, lb_ref))
    for z, (kd_o, b_o, an_o, cum_o, l_ref) in enumerate(outs):
        wd = zl[:, z * dl:(z + 1) * dl]
        ad = zl[:, 2 * dl + z * il:2 * dl + (z + 1) * il]
        wl = w0_ref[z:z + 1, :] + _dot1(jnp.tanh(wd), w2_ref[z])
        lw = -math.exp(-0.5) * _sigmoid(wl)
        icl = _sigmoid(a0_ref[z:z + 1, :] + _dot1(ad, a2_ref[z]))
        kd = k * (1.0 + (icl - 1.0) * ka_ref[...])
        kd_o[0] = kd
        b_o[0] = kk * icl
        an_o[0] = -kk * jnp.exp(-lw)
        lh = lw.astype(BF16)
        rem = lw - lh.astype(F32)
        lm = rem.astype(BF16)
        ll = (rem - lm.astype(F32)).astype(BF16)
        lmat = l_ref[...]
        cum_o[0] = _dg(lmat, lh, _NN) + (_dg(lmat, lm, _NN) + _dg(lmat, ll, _NN))
        kd_sum = kd if kd_sum is None else kd_sum + kd
    bonus_o[0] = _seg_sum(r * kd_sum * rk_ref[...], e_ref, et_ref) * v


def _rw_prep(z_rkv, z_lo, mu_r, mu_l, w0, w2, a0, a2, g2, k_k, k_a, r_k):
    bsz, t, w3 = z_rkv.shape
    w = w3 // 3
    lw_cols = z_lo.shape[-1]
    dl, il = w2.shape[1], a2.shape[1]
    nh = w // RW_HEAD
    tt = _pick(t, 128, RW_CHUNK)
    n_t = t // tt
    hb = tt // SUBLANES
    head = jnp.arange(w, dtype=I32) // RW_HEAD
    e = (head[:, None] == jnp.arange(nh, dtype=I32)[None, :]).astype(BF16)
    et = e.T
    ti = jnp.arange(tt, dtype=I32)
    same = (ti[:, None] // RW_CHUNK) == (ti[None, :] // RW_CHUNK)
    lf = (same & (ti[None, :] <= ti[:, None])).astype(BF16)
    lb = (same & (ti[None, :] >= ti[:, None])).astype(BF16)

    def tile(c):
        return pl.BlockSpec((1, tt, c), lambda b, i: (b, i, 0))

    def prev(c):
        return pl.BlockSpec((1, SUBLANES, c), lambda b, i: (b, jnp.maximum(i * hb - 1, 0), 0))

    def nxt(c):
        return pl.BlockSpec((1, SUBLANES, c),
                            lambda b, i: (b, jnp.minimum((i + 1) * hb, t // SUBLANES - 1), 0))

    def full(a):
        nd = a.ndim
        return pl.BlockSpec(a.shape, lambda b, i: (0,) * nd)

    consts = [mu_r.reshape(1, w3), mu_l.reshape(1, lw_cols), w0, w2, a0, a2, g2,
              k_k.reshape(1, w), k_a.reshape(1, w), r_k.reshape(1, w), e, et, lf, lb]
    out = jax.ShapeDtypeStruct((bsz, t, w), F32)
    return pl.pallas_call(
        functools.partial(_rw_prep_kernel, dl=dl, il=il),
        grid=(bsz, n_t),
        in_specs=[tile(w3), prev(w3), nxt(w3), tile(lw_cols), prev(lw_cols), nxt(lw_cols)]
        + [full(a) for a in consts],
        out_specs=[tile(w)] * 12,
        out_shape=[out] * 12,
        compiler_params=_cparams("arbitrary", "arbitrary"),
        name="rw_prep",
    )(z_rkv, z_rkv, z_rkv, z_lo, z_lo, z_lo, *consts)


QUAD = 4


def _scan_quad_kernel(*refs):
    n_in = 6
    ins = (refs[:n_in], refs[n_in:2 * n_in])
    y_refs = refs[2 * n_in:2 * n_in + 2]
    h_refs = refs[2 * n_in + 2:]
    c = RW_CHUNK
    n = RW_HEAD
    qw = QUAD * n
    nq = h_refs[0].shape[1] // qw

    @pl.when(pl.program_id(1) == 0)
    def _():
        for h_ref in h_refs:
            h_ref[...] = jnp.zeros_like(h_ref)

    row = lax.broadcasted_iota(I32, (c, qw), 0)
    col = lax.broadcasted_iota(I32, (c, qw), 1) % n
    strict_d = (col < row, col > row)
    incl_d = (col <= row, col >= row)
    last_d = (c - 1, 0)
    eye = row == col
    eye_f = jnp.where(eye, 1.0, 0.0)
    bd_mask = (lax.broadcasted_iota(I32, (QUAD * c, qw), 0) // c
               == lax.broadcasted_iota(I32, (QUAD * c, qw), 1) // n)
    first_of_pair = lax.broadcasted_iota(I32, (c, 2 * n), 1) < n

    def bd(x):
        xb = x.astype(BF16)
        return jnp.where(bd_mask, jnp.concatenate([xb] * QUAD, axis=0), jnp.zeros((), BF16))

    def mm(a, b_bf16, dims=_NN):
        return _dg(a.astype(BF16), b_bf16, dims)

    chains = [(d, slice(qw * q, qw * (q + 1))) for d in range(2) for q in range(nq)]

    def load(i):
        return [ins[d][i][0, :, s] for d, s in chains]

    r_in, vs, kd_in, b_in, an_in, cum = (load(i) for i in range(n_in))
    e1 = [jnp.exp(x) for x in cum]
    e2 = [jnp.exp(-x) for x in cum]
    et = [jnp.exp(x[last_d[d]:last_d[d] + 1, :]) for x, (d, _) in zip(cum, chains)]
    rt = [x * e for x, e in zip(r_in, e1)]
    at = [x * e for x, e in zip(an_in, e1)]
    bt = [x * e for x, e in zip(b_in, e2)]
    kt = [x * e for x, e in zip(kd_in, e2)]
    bh = [x * e for x, e in zip(bt, et)]
    kh = [x * e for x, e in zip(kt, et)]
    hs = [h_refs[d][:, s] for d, s in chains]
    strict = [strict_d[d] for d, _ in chains]
    incl = [incl_d[d] for d, _ in chains]
    g = [mm(jnp.concatenate([a, r_], axis=0), jnp.concatenate([bd(b_), bd(k_)], axis=0), _NT)
         for a, r_, b_, k_ in zip(at, rt, bt, kt)]
    nab = [jnp.where(m, x[:c, :qw], 0.0) for m, x in zip(strict, g)]
    aak = [jnp.where(m, x[:c, qw:], 0.0) for m, x in zip(strict, g)]
    mb = [jnp.where(m, x[c:, :qw], 0.0) for m, x in zip(incl, g)]
    mk = [jnp.where(m, x[c:, qw:], 0.0) for m, x in zip(incl, g)]
    akv = [mm(jnp.concatenate([a, k_], axis=0), bd(x)) for a, k_, x in zip(aak, mk, vs)]
    av = [x[:c] for x in akv]
    mkv = [x[c:] for x in akv]
    tm = [eye_f + x for x in nab]
    npow = [mm(x, bd(x)) for x in nab]
    for _ in range(4):
        both = [mm(jnp.concatenate([x, t_], axis=0), bd(x)) for x, t_ in zip(npow, tm)]
        npow = [x[:c] for x in both]
        tm = [t_ + x[c:] for t_, x in zip(tm, both)]
    tm = [t_ + mm(t_, bd(x)) for t_, x in zip(tm, npow)]
    za = [mm(t_, bd(a)) for t_, a in zip(tm, at)]
    zw = [mm(t_, bd(a)) for t_, a in zip(tm, av)]
    rp = [r_ + mm(m, bd(z)) for r_, m, z in zip(rt, mb, za)]
    yi = [mm(m, bd(z)) + x for m, z, x in zip(mb, zw, mkv)]
    pm, qq = [], []
    for q in range(len(chains)):
        pms, qqs = [], []
        for pr in range(QUAD // 2):
            ps = slice(2 * n * pr, 2 * n * (pr + 1))
            lhs_t = jnp.concatenate([bh[q][:, ps], kh[q][:, ps]], axis=0)
            rhs = jnp.concatenate(
                [jnp.concatenate([za[q][:, ps], zw[q][:, ps]], axis=1),
                 jnp.concatenate([jnp.zeros((c, 2 * n), F32), vs[q][:, ps]], axis=1)], axis=0)
            m = _dot1(lhs_t, rhs, _TN)
            pms.append(jnp.where(first_of_pair, m[:n, :2 * n], m[n:, :2 * n]))
            qqs.append(jnp.where(first_of_pair, m[:n, 2 * n:], m[n:, 2 * n:]))
        pm.append(jnp.concatenate(pms, axis=1) + jnp.where(eye, et[q], 0.0))
        qq.append(jnp.concatenate(qqs, axis=1))
    o3 = [mm(jnp.concatenate([r_, p_], axis=0), bd(h_)) for r_, p_, h_ in zip(rp, pm, hs)]
    for q, (d, s) in enumerate(chains):
        y_refs[d][0, :, s] = o3[q][:c] + yi[q]
        h_refs[d][:, s] = o3[q][c:] + qq[q]


def _rw_scan(fwd, bwd):
    bsz, t, w = fwd[0].shape
    nc = t // RW_CHUNK
    assert w % (QUAD * RW_HEAD) == 0
    spec_f = pl.BlockSpec((1, RW_CHUNK, w), lambda bb, ci: (bb, ci, 0))
    spec_b = pl.BlockSpec((1, RW_CHUNK, w), lambda bb, ci: (bb, nc - 1 - ci, 0))
    out = jax.ShapeDtypeStruct((bsz, t, w), F32)
    return pl.pallas_call(
        _scan_quad_kernel,
        grid=(bsz, nc),
        in_specs=[spec_f] * 6 + [spec_b] * 6,
        out_specs=[spec_f, spec_b],
        out_shape=[out, out],
        scratch_shapes=[pltpu.VMEM((RW_HEAD, w), F32), pltpu.VMEM((RW_HEAD, w), F32)],
        compiler_params=_cparams("arbitrary", "arbitrary"),
        name="rw_scan",
    )(*fwd, *bwd)


def _merge_kernel(yf_ref, yb_ref, bonus_ref, g_ref, att_ref, ga_ref, gb_ref, lw_ref, lb_ref,
                  e_ref, et_ref, o_ref):
    y = yf_ref[0] + yb_ref[0]
    mean = _seg_sum(y, e_ref, et_ref) * (1.0 / RW_HEAD)
    d = y - mean
    var = _seg_sum(d * d, e_ref, et_ref) * (1.0 / RW_HEAD)
    yn = d * lax.rsqrt(var + GN_EPS) * lw_ref[...] + lb_ref[...]
    ya = (yn + bonus_ref[0]) * g_ref[0]
    o_ref[0] = (_sigmoid(ga_ref[0]) * ya + _sigmoid(gb_ref[0]) * att_ref[0]).astype(o_ref.dtype)


def _merge(yf, yb, bonus, g, att, z_gates, lnx_w, lnx_b):
    bsz, t, w = yf.shape
    nh = w // RW_HEAD
    tt = _pick(t, 256, SUBLANES)
    head = jnp.arange(w, dtype=I32) // RW_HEAD
    e = (head[:, None] == jnp.arange(nh, dtype=I32)[None, :]).astype(BF16)
    et = e.T
    tile = pl.BlockSpec((1, tt, w), lambda b, i: (b, i, 0))
    gate_b = pl.BlockSpec((1, tt, w), lambda b, i: (b, i, 1))
    vec = pl.BlockSpec((1, w), lambda b, i: (0, 0))
    return pl.pallas_call(
        _merge_kernel,
        grid=(bsz, t // tt),
        in_specs=[tile, tile, tile, tile, tile, tile, gate_b, vec, vec,
                  pl.BlockSpec(e.shape, lambda b, i: (0, 0)),
                  pl.BlockSpec(et.shape, lambda b, i: (0, 0))],
        out_specs=tile,
        out_shape=jax.ShapeDtypeStruct((bsz, t, w), BF16),
        compiler_params=_cparams("arbitrary", "arbitrary"),
        name="merge",
    )(yf, yb, bonus, g, att, z_gates, z_gates, lnx_w.reshape(1, w), lnx_b.reshape(1, w), e, et)


LOG2E = math.log2(math.e)


MIN_ROW_SUM = 2.0 ** -60


def _attn_kernel(slope_ref, lam_ref, q_ref, k_ref, v_ref, sg_ref, d_ref, o_ref, kb_ref, vb_ref,
                 kn_ref, *, out_scale):
    h = pl.program_id(1)
    qi = pl.program_id(2)
    tq = q_ref.shape[1]
    t = k_ref.shape[1]

    @pl.when(qi == 0)
    def _():
        k = k_ref[0]
        for c in range(2):
            kc = k[:, c * DA_QK:(c + 1) * DA_QK].astype(BF16)
            kb_ref[c] = kc
            kf = kc.astype(F32)
            kn2 = jnp.max(jnp.sum(kf * kf, axis=1, keepdims=True), axis=0, keepdims=True)
            kn_ref[c] = jnp.broadcast_to(kn2, kn_ref.shape[1:])
        vb_ref[...] = v_ref[0].astype(BF16)

    start = pl.multiple_of((pl.num_programs(2) - 1 - qi) * tq, tq)
    bias = (-LOG2E * slope_ref[h]) * d_ref[:, pl.ds(start, t)]
    q = (q_ref[0] * (DA_QK ** -0.5 * LOG2E)).astype(BF16)
    qf = q.astype(F32)

    def softmax_pv(row_shift):
        outs, sums = [], []
        for c in range(2):
            cs = slice(c * DA_QK, (c + 1) * DA_QK)
            s = _dg(q[:, cs], kb_ref[c], _NT) + bias
            p = jnp.exp2(s - row_shift(c, s))
            sums.append(jnp.sum(p, axis=1, keepdims=True))
            outs.append(_dg(p.astype(BF16), vb_ref[...], _NN))
        return outs, sums

    def finish(outs, sums):
        o = outs[0] / sums[0] - lam_ref[0] * (outs[1] / sums[1])
        o = o * lax.rsqrt(jnp.mean(o * o, axis=-1, keepdims=True) + EPS) * sg_ref[...]
        o_ref[0] = o * out_scale

    def norm_bound(c, s):
        qc = qf[:, c * DA_QK:(c + 1) * DA_QK]
        qn2 = jnp.sum(qc * qc, axis=1, keepdims=True)
        return jnp.sqrt(qn2 * kn_ref[c][0:1, 0:1]) * (1.0 + 2.0 ** -7)

    outs, sums = softmax_pv(norm_bound)
    safe = jnp.minimum(jnp.min(sums[0]), jnp.min(sums[1])) >= MIN_ROW_SUM

    @pl.when(safe)
    def _():
        finish(outs, sums)

    @pl.when(jnp.logical_not(safe))
    def _():
        finish(*softmax_pv(lambda c, s: jnp.max(s, axis=1, keepdims=True)))


def _diff_attention(z_qkv, lam, sub_g, lambda_init):
    bsz, t, w3 = z_qkv.shape
    nh = w3 // 3 // DA_V
    tq = _pick(t, 512, LANES)
    slopes = 2.0 ** (-8.0 * jnp.arange(1, nh + 1, dtype=F32) / nh)
    dist = jnp.abs(jnp.arange(tq, dtype=I32)[:, None] + (t - tq)
                   - jnp.arange(2 * t - tq, dtype=I32)[None, :]).astype(F32)
    smem = pl.BlockSpec(memory_space=pltpu.SMEM)
    return pl.pallas_call(
        functools.partial(_attn_kernel, out_scale=1.0 - lambda_init),
        grid=(bsz, nh, t // tq),
        in_specs=[smem, smem,
                  pl.BlockSpec((1, tq, DA_V), lambda b, h, i: (b, i, h)),
                  pl.BlockSpec((1, t, DA_V), lambda b, h, i: (b, 0, nh + h)),
                  pl.BlockSpec((1, t, DA_V), lambda b, h, i: (b, 0, 2 * nh + h)),
                  pl.BlockSpec((1, DA_V), lambda b, h, i: (0, 0)),
                  pl.BlockSpec(dist.shape, lambda b, h, i: (0, 0), pipeline_mode=pl.Buffered(1))],
        out_specs=pl.BlockSpec((1, tq, DA_V), lambda b, h, i: (b, i, h)),
        out_shape=jax.ShapeDtypeStruct((bsz, t, nh * DA_V), F32),
        scratch_shapes=[pltpu.VMEM((2, t, DA_QK), BF16), pltpu.VMEM((t, DA_V), BF16),
                        pltpu.VMEM((2, SUBLANES, LANES), F32)],
        compiler_params=_cparams("arbitrary", "arbitrary", "arbitrary"),
        name="diff_attention",
    )(slopes, lam, z_qkv, z_qkv, z_qkv, sub_g.reshape(1, DA_V), dist)


def _router_kernel(x_ref, g_ref, sc_ref, sh_ref, wr_ref, h_ref, aff_ref):
    x = x_ref[0]
    y = x * lax.rsqrt(jnp.mean(x * x, axis=-1, keepdims=True) + EPS) * g_ref[...]
    h = y * (1.0 + sc_ref[0]) + sh_ref[0]
    h_ref[0] = h.astype(h_ref.dtype)
    logits = _dot3(wr_ref[...], h, _NT)
    m = jnp.max(logits, axis=0, keepdims=True)
    ex = jnp.exp(logits - m)
    aff = ex / jnp.sum(ex, axis=0, keepdims=True)
    for j in range(aff_ref.shape[1]):
        aff_ref[:, j, :] = aff[:, j * LANES:(j + 1) * LANES]


def _router(x, g, scale, shift, w_router_t):
    bsz, t, d = x.shape
    ne = w_router_t.shape[0]
    tt = _pick(t, SUBLANES * LANES, SUBLANES * LANES)
    nb = tt // LANES
    n_t = t // tt
    row = pl.BlockSpec((1, 1, d), lambda b, i: (b, 0, 0))
    return pl.pallas_call(
        _router_kernel,
        grid=(bsz, n_t),
        in_specs=[pl.BlockSpec((1, tt, d), lambda b, i: (b, i, 0)),
                  pl.BlockSpec((1, d), lambda b, i: (0, 0)), row, row,
                  pl.BlockSpec((ne, d), lambda b, i: (0, 0))],
        out_specs=[pl.BlockSpec((1, tt, d), lambda b, i: (b, i, 0)),
                   pl.BlockSpec((ne, nb, LANES), lambda b, i: (0, b * n_t + i, 0))],
        out_shape=[jax.ShapeDtypeStruct((bsz, t, d), F32),
                   jax.ShapeDtypeStruct((ne, bsz * t // LANES, LANES), F32)],
        compiler_params=_cparams("arbitrary", "arbitrary"),
        name="router",
    )(x, g.reshape(1, d), scale, shift, w_router_t)


def _sum_all(x):
    return jnp.sum(jnp.sum(x, axis=1, keepdims=True), axis=0, keepdims=True)


SLOT_ROWS = 8


def _select_kernel(aff_ref, slots_ref, pos_ref, base_ref, rec_ref, thr_ref, base_smem, sem, *, cap):
    nb = aff_ref.shape[1]
    li = lax.broadcasted_iota(I32, (LANES, LANES), 0)
    lj = lax.broadcasted_iota(I32, (LANES, LANES), 1)
    upper = jnp.where(li <= lj, 1.0, 0.0).astype(BF16)
    ri = lax.broadcasted_iota(I32, (nb, nb), 0)
    rj = lax.broadcasted_iota(I32, (nb, nb), 1)
    lower = jnp.where(rj < ri, 1.0, 0.0).astype(BF16)

    def excl_prefix(mask):
        x = jnp.where(mask, 1.0, 0.0)
        inc = _dg(x.astype(BF16), upper, _NN)
        tot = jnp.broadcast_to(inc[:, LANES - 1:LANES], (nb, LANES))
        offs = _dg(lower, tot.astype(BF16), _NN)
        return offs + inc - x, offs

    bits_all = pltpu.bitcast(aff_ref[...], I32)

    def bit_body(i, thr):
        cand = thr | jnp.left_shift(jnp.int32(1), 30 - i)
        cnt = jnp.sum(jnp.where(bits_all >= cand, 1.0, 0.0), axis=(1, 2), keepdims=True)
        return jnp.where(cnt >= cap, cand, thr)

    thr_all = lax.fori_loop(0, 31, bit_body, jnp.zeros((aff_ref.shape[0], 1, 1), I32))
    thr_ref[...] = jnp.broadcast_to(thr_all, thr_ref.shape)

    def expert_body(e, carry):
        bits = pltpu.bitcast(aff_ref[e], I32)
        thr = thr_ref[e][0:1, 0:1]
        gt = bits > thr
        eq = bits == thr
        need = cap - _sum_all(jnp.where(gt, 1.0, 0.0))
        sel = gt | (eq & (excl_prefix(eq)[0] < need))
        pos, row_base = excl_prefix(sel)
        pos_ref[...] = jnp.where(sel, pos, -1.0).astype(I32)
        base_ref[...] = row_base.astype(I32)
        to_smem = pltpu.make_async_copy(base_ref, base_smem, sem)
        to_smem.start()
        rec_ref[...] = jnp.zeros_like(rec_ref)
        to_smem.wait()
        window_ids = lax.broadcasted_iota(I32, (2 * LANES, LANES), 0)
        lane_f = lax.broadcasted_iota(I32, (1, LANES), 1).astype(F32)

        def group_body(g, c2):
            w0 = pl.multiple_of((base_smem[g, 0] // LANES) * LANES, LANES)
            a = aff_ref[e, pl.ds(g, 1), :]
            a_hi = a.astype(BF16).astype(F32)
            a_mid = (a - a_hi).astype(BF16).astype(F32)
            a_lo = a - a_hi - a_mid
            base = jnp.full((1, LANES), LANES, F32) * g
            rec = jnp.concatenate([lane_f, base, a_hi, a_mid, a_lo,
                                   jnp.zeros((SLOT_ROWS - 5, LANES), F32)], axis=0)
            hot = jnp.where(window_ids == pos_ref[pl.ds(g, 1), :] - w0, 1.0, 0.0)
            rec_ref[:, pl.ds(w0, 2 * LANES)] += _dot1(rec, hot, _NT)
            return c2

        lax.fori_loop(0, nb, group_body, 0, unroll=4)
        rec = rec_ref[:, :cap]
        rid = lax.broadcasted_iota(I32, (SLOT_ROWS, cap), 0)
        token = rec[0:1] + rec[1:2]
        gate = rec[2:3] + rec[3:4] + rec[4:5]
        slots_ref[e] = jnp.where(rid == 0, token, jnp.where(rid == 1, gate, 0.0))
        return carry

    lax.fori_loop(0, aff_ref.shape[0], expert_body, 0)


def _select(aff, cap):
    ne, nb, _ = aff.shape
    return pl.pallas_call(
        functools.partial(_select_kernel, cap=cap),
        grid=(1,),
        in_specs=[pl.BlockSpec(aff.shape, lambda i: (0, 0, 0))],
        out_specs=pl.BlockSpec((ne, SLOT_ROWS, cap), lambda i: (0, 0, 0)),
        out_shape=jax.ShapeDtypeStruct((ne, SLOT_ROWS, cap), F32),
        scratch_shapes=[pltpu.VMEM((nb, LANES), I32), pltpu.VMEM((nb, LANES), I32),
                        pltpu.VMEM((SLOT_ROWS, -(-cap // LANES) * LANES + 2 * LANES), F32),
                        pltpu.VMEM((ne, SUBLANES, LANES), I32),
                        pltpu.SMEM((nb, LANES), I32), pltpu.SemaphoreType.DMA(())],
        compiler_params=_cparams("arbitrary"),
        name="select",
    )(aff)


def _ffn_kernel(idx_ref, h_hbm, wg_ref, wu_ref, wd_ref, gate_ref, moe_in_hbm, moe_hbm,
                xf_ref, xb_ref, acc_ref, rows_ref, sems, *, x_per_step):
    del moe_in_hbm
    e = pl.program_id(0)
    f = pl.program_id(1)
    n_e = pl.num_programs(0)
    n_f = pl.num_programs(1)
    cap = xb_ref.shape[0]

    def x_copy(ex, p):
        row = idx_ref[ex * cap + jnp.minimum(p, cap - 1)]
        return pltpu.make_async_copy(h_hbm.at[pl.ds(row, 1)], xf_ref.at[pl.ds(p, 1)], sems.at[0])

    def for_staging_rows(fn):
        def body(p, c):
            fn(p)
            return c
        lax.fori_loop(0, x_per_step * n_f, body, 0, unroll=8)

    def acc_copy(p):
        row = idx_ref[e * cap + p]
        return pltpu.make_async_copy(moe_hbm.at[pl.ds(row, 1)], rows_ref.at[pl.ds(p, 1)], sems.at[1])

    def out_copy(ex, p):
        row = idx_ref[ex * cap + p]
        return pltpu.make_async_copy(rows_ref.at[pl.ds(p, 1)], moe_hbm.at[pl.ds(row, 1)], sems.at[2])

    def for_slots(fn):
        def body(p, c):
            fn(p)
            return c
        lax.fori_loop(0, cap, body, 0, unroll=8)

    @pl.when(f == 0)
    def _():
        @pl.when(e == 0)
        def _():
            for_staging_rows(lambda p: x_copy(e, p).start())

        for_staging_rows(lambda p: x_copy(e, p).wait())
        xb_ref[...] = xf_ref[0:cap, :].astype(BF16)
        acc_ref[...] = jnp.zeros_like(acc_ref)

    nxt = jnp.where(e + 1 < n_e, e + 1, 0)
    for j in range(x_per_step):
        x_copy(nxt, f * x_per_step + j).start()

    xe = xb_ref[...]
    hid = _silu(_dot1(xe, wg_ref[0])) * _dot1(xe, wu_ref[0])
    acc_ref[...] += _dot1(hid, wd_ref[0])

    @pl.when(f == n_f // 2)
    def _():
        @pl.when(e > 0)
        def _():
            for_slots(lambda p: out_copy(e - 1, p).wait())

        for_slots(lambda p: acc_copy(p).start())

    @pl.when(f == n_f - 1)
    def _():
        for_slots(lambda p: acc_copy(p).wait())
        rows_ref[...] += acc_ref[...] * gate_ref[0]
        for_slots(lambda p: out_copy(e, p).start())

        @pl.when(e == n_e - 1)
        def _():
            for_slots(lambda p: out_copy(e, p).wait())
            for_staging_rows(lambda p: x_copy(nxt, p).wait())


def _expert_ffn(idx, gate, h2, w_gate, w_up, w_down):
    ne, cap, _ = gate.shape
    n_tok, d = h2.shape
    ff = w_gate.shape[-1]
    tf = _pick(ff, 256, LANES)
    n_f = ff // tf
    assert n_f >= 2, "accumulator rows are fetched mid-way through an expert's steps"
    x_per_step = -(-cap // n_f)
    x_rows = -(-x_per_step * n_f // SUBLANES) * SUBLANES
    any_spec = pl.BlockSpec(memory_space=pl.ANY)
    grid_spec = pltpu.PrefetchScalarGridSpec(
        num_scalar_prefetch=1,
        grid=(ne, n_f),
        in_specs=[any_spec,
                  pl.BlockSpec((1, d, tf), lambda e, f, idx_ref: (e, 0, f)),
                  pl.BlockSpec((1, d, tf), lambda e, f, idx_ref: (e, 0, f)),
                  pl.BlockSpec((1, tf, d), lambda e, f, idx_ref: (e, f, 0)),
                  pl.BlockSpec((1, cap, 1), lambda e, f, idx_ref: (e, 0, 0)),
                  any_spec],
        out_specs=any_spec,
        scratch_shapes=[pltpu.VMEM((x_rows, d), F32), pltpu.VMEM((cap, d), BF16),
                        pltpu.VMEM((cap, d), F32), pltpu.VMEM((cap, d), F32),
                        pltpu.SemaphoreType.DMA((3,))])
    return pl.pallas_call(
        functools.partial(_ffn_kernel, x_per_step=x_per_step),
        grid_spec=grid_spec,
        out_shape=jax.ShapeDtypeStruct((n_tok, d), F32),
        input_output_aliases={6: 0},
        compiler_params=_cparams("arbitrary", "arbitrary"),
        name="expert_ffn",
    )(idx, h2, w_gate, w_up, w_down, gate, jnp.zeros((n_tok, d), F32))


def _residual_out_kernel(x_ref, m_ref, g_ref, fg_ref, o_ref, *, final_norm):
    x = x_ref[0] + g_ref[0] * m_ref[0]
    if final_norm:
        x = x * lax.rsqrt(jnp.mean(x * x, axis=-1, keepdims=True) + EPS) * fg_ref[...]
    o_ref[0] = x


def _residual_out(x, moe, gate2, final_g, final_norm):
    bsz, t, d = x.shape
    tt = _pick(t, 512, SUBLANES)
    tile = pl.BlockSpec((1, tt, d), lambda b, i: (b, i, 0))
    return pl.pallas_call(
        functools.partial(_residual_out_kernel, final_norm=final_norm),
        grid=(bsz, t // tt),
        in_specs=[tile, tile, pl.BlockSpec((1, 1, d), lambda b, i: (b, 0, 0)),
                  pl.BlockSpec((1, d), lambda b, i: (0, 0))],
        out_specs=tile,
        out_shape=jax.ShapeDtypeStruct((bsz, t, d), F32),
        compiler_params=_cparams("arbitrary", "arbitrary"),
        name="residual_out",
    )(x, moe, gate2, final_g.reshape(1, d))


def _encoder_layer(x, ada, lw, lambda_init, final_g, final_norm):
    bsz, t, d = x.shape
    shift1, scale1, gate1, shift2, scale2, gate2 = [a[:, None, :] for a in jnp.split(ada, 6, axis=-1)]

    h = _norm_mod(x, lw["norm1_g"], scale1, shift1).reshape(bsz * t, d)
    z_rkv = _matmul(h, lw["w_rkv"]).reshape(bsz, t, -1)
    z_lo = _matmul(h, lw["w_lo"]).reshape(bsz, t, -1)
    z_qkv = _matmul(h, lw["w_qkv"]).reshape(bsz, t, -1)
    z_gates = _matmul(h, lw["w_gates"]).reshape(bsz, t, -1)

    (r, v, g, bonus, kd_f, b_f, an_f, cum_f, kd_b, b_b, an_b, cum_b) = _rw_prep(
        z_rkv, z_lo, lw["mu_r"], lw["mu_l"], lw["w0"], lw["w2"], lw["a0"], lw["a2"], lw["g2"],
        lw["k_k"], lw["k_a"], lw["r_k"])
    y_f, y_b = _rw_scan((r, v, kd_f, b_f, an_f, cum_f), (r, v, kd_b, b_b, an_b, cum_b))

    att = _diff_attention(z_qkv, lw["lam"], lw["sub_g"], lambda_init)
    merged = _merge(y_f, y_b, bonus, g, att, z_gates, lw["lnx_w"], lw["lnx_b"])
    x1 = _matmul_residual(merged, lw["w_out"], x, gate1)

    n_tok = bsz * t
    ne = lw["w_router_t"].shape[0]
    cap = max(1, CAPACITY_FACTOR * n_tok // ne)
    h2, aff = _router(x1, lw["norm2_g"], scale2, shift2, lw["w_router_t"])
    slots = _select(aff, cap)
    idx = slots[:, 0, :].astype(I32).reshape(ne * cap)
    gate = slots[:, 1, :].reshape(ne, cap, 1)
    moe = _expert_ffn(idx, gate, h2.reshape(n_tok, d), lw["w_gate"], lw["w_up"], lw["w_down"])
    return _residual_out(x1, moe.reshape(bsz, t, d), gate2, final_g, final_norm)


def kernel(x_prompt, x_sample, c_prompt, c_sample, w_ada, b_ada, norm1_g, w_in, mu_shift, w0, w2, a0, a2, g2, k_k, k_a, r_k, lnx_w, lnx_b, lam_q1, lam_k1, lam_q2, lam_k2, sub_g, w_out, norm2_g, w_router, w_gate, w_up, w_down, final_g):
    depth = w_ada.shape[0]
    d = x_prompt.shape[-1]
    w3 = 3 * d
    n_lo = 2 * w2.shape[2] + 2 * a2.shape[2] + g2.shape[1]
    xs = [x_prompt, x_sample]
    n_rows = [x_prompt.shape[0], x_sample.shape[0]]
    c_all = jnp.concatenate([c_prompt, c_sample], axis=0)
    pad = (-c_all.shape[0]) % SUBLANES
    c_all = jnp.pad(c_all, ((0, pad), (0, 0)))
    for l in range(depth):
        lambda_init = 0.8 - 0.6 * math.exp(-0.3 * l)
        w_in_l = w_in[l].astype(BF16)
        lam = (jnp.exp(jnp.sum(lam_q1[l] * lam_k1[l])) - jnp.exp(jnp.sum(lam_q2[l] * lam_k2[l]))
               + lambda_init).astype(F32).reshape(1)
        lw = dict(
            norm1_g=norm1_g[l], norm2_g=norm2_g[l],
            w_rkv=w_in_l[:, :w3], w_lo=w_in_l[:, w3:w3 + n_lo],
            w_qkv=w_in_l[:, w3 + n_lo:w3 + n_lo + 3 * d], w_gates=w_in_l[:, w3 + n_lo + 3 * d:],
            mu_r=mu_shift[l, :w3], mu_l=mu_shift[l, w3:w3 + n_lo],
            w0=w0[l], w2=w2[l], a0=a0[l], a2=a2[l], g2=g2[l], k_k=k_k[l], k_a=k_a[l], r_k=r_k[l],
            lnx_w=lnx_w[l], lnx_b=lnx_b[l], lam=lam, sub_g=sub_g[l],
            w_out=w_out[l].astype(BF16), w_router_t=w_router[l].T,
            w_gate=w_gate[l], w_up=w_up[l], w_down=w_down[l])
        ada = _ada(c_all, w_ada[l], b_ada[l])
        off = 0
        for gi in range(2):
            xs[gi] = _encoder_layer(xs[gi], ada[off:off + n_rows[gi]], lw, lambda_init, final_g,
                                    final_norm=(l == depth - 1))
            off += n_rows[gi]
    return tuple(xs)
```

```python
import functools
import math

import jax
import jax.numpy as jnp
from jax import lax
from jax.experimental import pallas as pl
from jax.experimental.pallas import tpu as pltpu

F32 = jnp.float32
BF16 = jnp.bfloat16
I32 = jnp.int32

RW_HEAD = 64
DA_QK = 64
DA_V = 2 * DA_QK
RW_CHUNK = 64
CAPACITY_FACTOR = 2
EPS = 1e-6
GN_EPS = RW_HEAD * 1e-5
LANES = 128
SUBLANES = 8
VMEM_LIMIT = 56 * 1024 * 1024


def _cparams(*sem):
    return pltpu.CompilerParams(dimension_semantics=sem, vmem_limit_bytes=VMEM_LIMIT)


def _pick(n, pref, align):
    if n <= pref:
        return n
    t = (pref // align) * align
    while t >= align:
        if n % t == 0:
            return t
        t -= align
    return n


_NN = (((1,), (0,)), ((), ()))
_NT = (((1,), (1,)), ((), ()))
_TN = (((0,), (0,)), ((), ()))


def _dg(a, b, dims):
    return lax.dot_general(a, b, dims, preferred_element_type=F32)


def _dot1(a, b, dims=_NN):
    return _dg(a.astype(BF16), b.astype(BF16), dims)


def _split(x):
    hi = x.astype(BF16)
    lo = (x - hi.astype(F32)).astype(BF16)
    return hi, lo


def _dot3(a, b, dims=_NN):
    ah, al = _split(a)
    bh, bl = _split(b)
    return _dg(ah, bh, dims) + (_dg(ah, bl, dims) + _dg(al, bh, dims))


def _dot_exact_rhs(a, b_bf16, dims=_NN):
    ah, al = _split(a)
    return _dg(ah, b_bf16, dims) + _dg(al, b_bf16, dims)


def _sigmoid(x):
    return 1.0 / (1.0 + jnp.exp(-x))


def _silu(x):
    return x * _sigmoid(x)


def _ada_kernel(c_ref, w_ref, b_ref, o_ref):
    o_ref[...] = _dot1(_silu(c_ref[...]), w_ref[...]) + b_ref[...]


def _ada(c, w, b):
    m, d = c.shape
    n = w.shape[1]
    tn = _pick(n, 1024, LANES)
    return pl.pallas_call(
        _ada_kernel,
        grid=(n // tn,),
        in_specs=[pl.BlockSpec((m, d), lambda j: (0, 0)),
                  pl.BlockSpec((d, tn), lambda j: (0, j)),
                  pl.BlockSpec((1, tn), lambda j: (0, j))],
        out_specs=pl.BlockSpec((m, tn), lambda j: (0, j)),
        out_shape=jax.ShapeDtypeStruct((m, n), F32),
        compiler_params=_cparams("arbitrary"),
        name="ada",
    )(c, w, b.reshape(1, n))


def _norm_mod_kernel(x_ref, g_ref, sc_ref, sh_ref, o_ref):
    x = x_ref[0]
    y = x * lax.rsqrt(jnp.mean(x * x, axis=-1, keepdims=True) + EPS) * g_ref[...]
    o_ref[0] = (y * (1.0 + sc_ref[0]) + sh_ref[0]).astype(o_ref.dtype)


def _norm_mod(x, g, scale, shift):
    bsz, t, d = x.shape
    tt = _pick(t, 512, SUBLANES)
    row = pl.BlockSpec((1, 1, d), lambda b, i: (b, 0, 0))
    return pl.pallas_call(
        _norm_mod_kernel,
        grid=(bsz, t // tt),
        in_specs=[pl.BlockSpec((1, tt, d), lambda b, i: (b, i, 0)),
                  pl.BlockSpec((1, d), lambda b, i: (0, 0)), row, row],
        out_specs=pl.BlockSpec((1, tt, d), lambda b, i: (b, i, 0)),
        out_shape=jax.ShapeDtypeStruct((bsz, t, d), BF16),
        compiler_params=_cparams("arbitrary", "arbitrary"),
        name="norm_mod",
    )(x, g.reshape(1, d), scale, shift)


def _mm_kernel(a_ref, b_ref, o_ref):
    o_ref[...] = _dot1(a_ref[...], b_ref[...]).astype(o_ref.dtype)


def _matmul(a, b, out_dtype=F32, tm_pref=1024, tn_pref=1024):
    m, k = a.shape
    n = b.shape[1]
    tm = _pick(m, tm_pref, SUBLANES)
    tn = _pick(n, tn_pref, LANES)
    return pl.pallas_call(
        _mm_kernel,
        grid=(m // tm, n // tn),
        in_specs=[pl.BlockSpec((tm, k), lambda i, j: (i, 0)),
                  pl.BlockSpec((k, tn), lambda i, j: (0, j))],
        out_specs=pl.BlockSpec((tm, tn), lambda i, j: (i, j)),
        out_shape=jax.ShapeDtypeStruct((m, n), out_dtype),
        compiler_params=_cparams("arbitrary", "arbitrary"),
        name="matmul",
    )(a, b)


def _mm_res_kernel(a_ref, b_ref, x_ref, g_ref, o_ref):
    o_ref[0] = x_ref[0] + g_ref[0] * _dot1(a_ref[0], b_ref[...])


def _matmul_residual(a, b, x, gate):
    bsz, t, k = a.shape
    n = b.shape[1]
    tm = _pick(t, 1024, SUBLANES)
    tn = _pick(n, 1024, LANES)
    return pl.pallas_call(
        _mm_res_kernel,
        grid=(bsz, t // tm, n // tn),
        in_specs=[pl.BlockSpec((1, tm, k), lambda bb, i, j: (bb, i, 0)),
                  pl.BlockSpec((k, tn), lambda bb, i, j: (0, j)),
                  pl.BlockSpec((1, tm, tn), lambda bb, i, j: (bb, i, j)),
                  pl.BlockSpec((1, 1, tn), lambda bb, i, j: (bb, 0, j))],
        out_specs=pl.BlockSpec((1, tm, tn), lambda bb, i, j: (bb, i, j)),
        out_shape=jax.ShapeDtypeStruct((bsz, t, n), F32),
        compiler_params=_cparams("arbitrary", "arbitrary", "arbitrary"),
        name="matmul_residual",
    )(a, b, x, gate)


def _shift_tile(z, prev8, next8, mu, i, n_t):
    tt = z.shape[0]
    pr = jnp.where(i > 0, prev8[SUBLANES - 1:SUBLANES, :], 0.0)
    nx = jnp.where(i < n_t - 1, next8[0:1, :], 0.0)
    ridx = lax.broadcasted_iota(I32, z.shape, 0)
    zp = jnp.where(ridx == 0, pr, pltpu.roll(z, 1, 0))
    zn = jnp.where(ridx == tt - 1, nx, pltpu.roll(z, tt - 1, 0))
    return z + mu * (0.5 * (zp + zn) - z)


def _seg_sum(x, e_ref, et_ref):
    s = _dot_exact_rhs(x, e_ref[...])
    return _dot_exact_rhs(s, et_ref[...])


def _rw_prep_kernel(zr_ref, zrp_ref, zrn_ref, zl_ref, zlp_ref, zln_ref, mur_ref, mul_ref,
                    w0_ref, w2_ref, a0_ref, a2_ref, g2_ref, kk_ref, ka_ref, rk_ref,
                    e_ref, et_ref, lf_ref, lb_ref,
                    r_o, v_o, g_o, bonus_o,
                    kdf_o, bf_o, anf_o, cumf_o, kdb_o, bb_o, anb_o, cumb_o, *, dl, il):
    i = pl.program_id(1)
    n_t = pl.num_programs(1)
    w = r_o.shape[-1]
    zs = _shift_tile(zr_ref[0], zrp_ref[0], zrn_ref[0], mur_ref[...], i, n_t)
    zl = _shift_tile(zl_ref[0], zlp_ref[0], zln_ref[0], mul_ref[...], i, n_t)
    r = zs[:, 0:w]
    k = zs[:, w:2 * w]
    v = zs[:, 2 * w:3 * w]
    gd = zl[:, 2 * dl + 2 * il:]
    kkr = k * kk_ref[...]
    kk = kkr * lax.rsqrt(jnp.maximum(_seg_sum(kkr * kkr, e_ref, et_ref), 1e-24))
    r_o[0] = r
    v_o[0] = v
    g_o[0] = _dot1(_sigmoid(gd), g2_ref[...])
    kd_sum = None
    outs = ((kdf_o, bf_o, anf_o, cumf_o, lf_ref), (kdb_o, bb_o, anb_o, cumb_o, lb_ref))
    for z, (kd_o, b_o, an_o, cum_o, l_ref) in enumerate(outs):
        wd = zl[:, z * dl:(z + 1) * dl]
        ad = zl[:, 2 * dl + z * il:2 * dl + (z + 1) * il]
        wl = w0_ref[z:z + 1, :] + _dot1(jnp.tanh(wd), w2_ref[z])
        lw = -math.exp(-0.5) * _sigmoid(wl)
        icl = _sigmoid(a0_ref[z:z + 1, :] + _dot1(ad, a2_ref[z]))
        kd = k * (1.0 + (icl - 1.0) * ka_ref[...])
        kd_o[0] = kd
        b_o[0] = kk * icl
        an_o[0] = -kk * jnp.exp(-lw)
        lh = lw.astype(BF16)
        rem = lw - lh.astype(F32)
        lm = rem.astype(BF16)
        ll = (rem - lm.astype(F32)).astype(BF16)
        lmat = l_ref[...]
        cum_o[0] = _dg(lmat, lh, _NN) + (_dg(lmat, lm, _NN) + _dg(lmat, ll, _NN))
        kd_sum = kd if kd_sum is None else kd_sum + kd
    bonus_o[0] = _seg_sum(r * kd_sum * rk_ref[...], e_ref, et_ref) * v


def _rw_prep(z_rkv, z_lo, mu_r, mu_l, w0, w2, a0, a2, g2, k_k, k_a, r_k):
    bsz, t, w3 = z_rkv.shape
    w = w3 // 3
    lw_cols = z_lo.shape[-1]
    dl, il = w2.shape[1], a2.shape[1]
    nh = w // RW_HEAD
    tt = _pick(t, 128, RW_CHUNK)
    n_t = t // tt
    hb = tt // SUBLANES
    head = jnp.arange(w, dtype=I32) // RW_HEAD
    e = (head[:, None] == jnp.arange(nh, dtype=I32)[None, :]).astype(BF16)
    et = e.T
    ti = jnp.arange(tt, dtype=I32)
    same = (ti[:, None] // RW_CHUNK) == (ti[None, :] // RW_CHUNK)
    lf = (same & (ti[None, :] <= ti[:, None])).astype(BF16)
    lb = (same & (ti[None, :] >= ti[:, None])).astype(BF16)

    def tile(c):
        return pl.BlockSpec((1, tt, c), lambda b, i: (b, i, 0))

    def prev(c):
        return pl.BlockSpec((1, SUBLANES, c), lambda b, i: (b, jnp.maximum(i * hb - 1, 0), 0))

    def nxt(c):
        return pl.BlockSpec((1, SUBLANES, c),
                            lambda b, i: (b, jnp.minimum((i + 1) * hb, t // SUBLANES - 1), 0))

    def full(a):
        nd = a.ndim
        return pl.BlockSpec(a.shape, lambda b, i: (0,) * nd)

    consts = [mu_r.reshape(1, w3), mu_l.reshape(1, lw_cols), w0, w2, a0, a2, g2,
              k_k.reshape(1, w), k_a.reshape(1, w), r_k.reshape(1, w), e, et, lf, lb]
    out = jax.ShapeDtypeStruct((bsz, t, w), F32)
    return pl.pallas_call(
        functools.partial(_rw_prep_kernel, dl=dl, il=il),
        grid=(bsz, n_t),
        in_specs=[tile(w3), prev(w3), nxt(w3), tile(lw_cols), prev(lw_cols), nxt(lw_cols)]
        + [full(a) for a in consts],
        out_specs=[tile(w)] * 12,
        out_shape=[out] * 12,
        compiler_params=_cparams("arbitrary", "arbitrary"),
        name="rw_prep",
    )(z_rkv, z_rkv, z_rkv, z_lo, z_lo, z_lo, *consts)


QUAD = 4


def _scan_quad_kernel(*refs):
    n_in = 6
    ins = (refs[:n_in], refs[n_in:2 * n_in])
    y_refs = refs[2 * n_in:2 * n_in + 2]
    h_refs = refs[2 * n_in + 2:]
    c = RW_CHUNK
    n = RW_HEAD
    qw = QUAD * n
    nq = h_refs[0].shape[1] // qw

    @pl.when(pl.program_id(1) == 0)
    def _():
        for h_ref in h_refs:
            h_ref[...] = jnp.zeros_like(h_ref)

    row = lax.broadcasted_iota(I32, (c, qw), 0)
    col = lax.broadcasted_iota(I32, (c, qw), 1) % n
    strict_d = (col < row, col > row)
    incl_d = (col <= row, col >= row)
    last_d = (c - 1, 0)
    eye = row == col
    eye_f = jnp.where(eye, 1.0, 0.0)
    bd_mask = (lax.broadcasted_iota(I32, (QUAD * c, qw), 0) // c
               == lax.broadcasted_iota(I32, (QUAD * c, qw), 1) // n)
    first_of_pair = lax.broadcasted_iota(I32, (c, 2 * n), 1) < n

    def bd(x):
        xb = x.astype(BF16)
        return jnp.where(bd_mask, jnp.concatenate([xb] * QUAD, axis=0), jnp.zeros((), BF16))

    def mm(a, b_bf16, dims=_NN):
        return _dg(a.astype(BF16), b_bf16, dims)

    chains = [(d, slice(qw * q, qw * (q + 1))) for d in range(2) for q in range(nq)]

    def load(i):
        return [ins[d][i][0, :, s] for d, s in chains]

    r_in, vs, kd_in, b_in, an_in, cum = (load(i) for i in range(n_in))
    e1 = [jnp.exp(x) for x in cum]
    e2 = [jnp.exp(-x) for x in cum]
    et = [jnp.exp(x[last_d[d]:last_d[d] + 1, :]) for x, (d, _) in zip(cum, chains)]
    rt = [x * e for x, e in zip(r_in, e1)]
    at = [x * e for x, e in zip(an_in, e1)]
    bt = [x * e for x, e in zip(b_in, e2)]
    kt = [x * e for x, e in zip(kd_in, e2)]
    bh = [x * e for x, e in zip(bt, et)]
    kh = [x * e for x, e in zip(kt, et)]
    hs = [h_refs[d][:, s] for d, s in chains]
    strict = [strict_d[d] for d, _ in chains]
    incl = [incl_d[d] for d, _ in chains]
    g = [mm(jnp.concatenate([a, r_], axis=0), jnp.concatenate([bd(b_), bd(k_)], axis=0), _NT)
         for a, r_, b_, k_ in zip(at, rt, bt, kt)]
    nab = [jnp.where(m, x[:c, :qw], 0.0) for m, x in zip(strict, g)]
    aak = [jnp.where(m, x[:c, qw:], 0.0) for m, x in zip(strict, g)]
    mb = [jnp.where(m, x[c:, :qw], 0.0) for m, x in zip(incl, g)]
    mk = [jnp.where(m, x[c:, qw:], 0.0) for m, x in zip(incl, g)]
    akv = [mm(jnp.concatenate([a, k_], axis=0), bd(x)) for a, k_, x in zip(aak, mk, vs)]
    av = [x[:c] for x in akv]
    mkv = [x[c:] for x in akv]
    tm = [eye_f + x for x in nab]
    npow = [mm(x, bd(x)) for x in nab]
    for _ in range(4):
        both = [mm(jnp.concatenate([x, t_], axis=0), bd(x)) for x, t_ in zip(npow, tm)]
        npow = [x[:c] for x in both]
        tm = [t_ + x[c:] for t_, x in zip(tm, both)]
    tm = [t_ + mm(t_, bd(x)) for t_, x in zip(tm, npow)]
    za = [mm(t_, bd(a)) for t_, a in zip(tm, at)]
    zw = [mm(t_, bd(a)) for t_, a in zip(tm, av)]
    rp = [r_ + mm(m, bd(z)) for r_, m, z in zip(rt, mb, za)]
    yi = [mm(m, bd(z)) + x for m, z, x in zip(mb, zw, mkv)]
    pm, qq = [], []
    for q in range(len(chains)):
        pms, qqs = [], []
        for pr in range(QUAD // 2):
            ps = slice(2 * n * pr, 2 * n * (pr + 1))
            lhs_t = jnp.concatenate([bh[q][:, ps], kh[q][:, ps]], axis=0)
            rhs = jnp.concatenate(
                [jnp.concatenate([za[q][:, ps], zw[q][:, ps]], axis=1),
                 jnp.concatenate([jnp.zeros((c, 2 * n), F32), vs[q][:, ps]], axis=1)], axis=0)
            m = _dot1(lhs_t, rhs, _TN)
            pms.append(jnp.where(first_of_pair, m[:n, :2 * n], m[n:, :2 * n]))
            qqs.append(jnp.where(first_of_pair, m[:n, 2 * n:], m[n:, 2 * n:]))
        pm.append(jnp.concatenate(pms, axis=1) + jnp.where(eye, et[q], 0.0))
        qq.append(jnp.concatenate(qqs, axis=1))
    o3 = [mm(jnp.concatenate([r_, p_], axis=0), bd(h_)) for r_, p_, h_ in zip(rp, pm, hs)]
    for q, (d, s) in enumerate(chains):
        y_refs[d][0, :, s] = o3[q][:c] + yi[q]
        h_refs[d][:, s] = o3[q][c:] + qq[q]


def _rw_scan(fwd, bwd):
    bsz, t, w = fwd[0].shape
    nc = t // RW_CHUNK
    assert w % (QUAD * RW_HEAD) == 0
    spec_f = pl.BlockSpec((1, RW_CHUNK, w), lambda bb, ci: (bb, ci, 0))
    spec_b = pl.BlockSpec((1, RW_CHUNK, w), lambda bb, ci: (bb, nc - 1 - ci, 0))
    out = jax.ShapeDtypeStruct((bsz, t, w), F32)
    return pl.pallas_call(
        _scan_quad_kernel,
        grid=(bsz, nc),
        in_specs=[spec_f] * 6 + [spec_b] * 6,
        out_specs=[spec_f, spec_b],
        out_shape=[out, out],
        scratch_shapes=[pltpu.VMEM((RW_HEAD, w), F32), pltpu.VMEM((RW_HEAD, w), F32)],
        compiler_params=_cparams("arbitrary", "arbitrary"),
        name="rw_scan",
    )(*fwd, *bwd)


def _merge_kernel(yf_ref, yb_ref, bonus_ref, g_ref, att_ref, ga_ref, gb_ref, lw_ref, lb_ref,
                  e_ref, et_ref, o_ref):
    y = yf_ref[0] + yb_ref[0]
    mean = _seg_sum(y, e_ref, et_ref) * (1.0 / RW_HEAD)
    d = y - mean
    var = _seg_sum(d * d, e_ref, et_ref) * (1.0 / RW_HEAD)
    yn = d * lax.rsqrt(var + GN_EPS) * lw_ref[...] + lb_ref[...]
    ya = (yn + bonus_ref[0]) * g_ref[0]
    o_ref[0] = (_sigmoid(ga_ref[0]) * ya + _sigmoid(gb_ref[0]) * att_ref[0]).astype(o_ref.dtype)


def _merge(yf, yb, bonus, g, att, z_gates, lnx_w, lnx_b):
    bsz, t, w = yf.shape
    nh = w // RW_HEAD
    tt = _pick(t, 256, SUBLANES)
    head = jnp.arange(w, dtype=I32) // RW_HEAD
    e = (head[:, None] == jnp.arange(nh, dtype=I32)[None, :]).astype(BF16)
    et = e.T
    tile = pl.BlockSpec((1, tt, w), lambda b, i: (b, i, 0))
    gate_b = pl.BlockSpec((1, tt, w), lambda b, i: (b, i, 1))
    vec = pl.BlockSpec((1, w), lambda b, i: (0, 0))
    return pl.pallas_call(
        _merge_kernel,
        grid=(bsz, t // tt),
        in_specs=[tile, tile, tile, tile, tile, tile, gate_b, vec, vec,
                  pl.BlockSpec(e.shape, lambda b, i: (0, 0)),
                  pl.BlockSpec(et.shape, lambda b, i: (0, 0))],
        out_specs=tile,
        out_shape=jax.ShapeDtypeStruct((bsz, t, w), BF16),
        compiler_params=_cparams("arbitrary", "arbitrary"),
        name="merge",
    )(yf, yb, bonus, g, att, z_gates, z_gates, lnx_w.reshape(1, w), lnx_b.reshape(1, w), e, et)


LOG2E = math.log2(math.e)


MIN_ROW_SUM = 2.0 ** -60
NORM_GROUP = 16


def _attn_kernel(slope_ref, lam_ref, q_ref, k_ref, v_ref, sg_ref, d_ref, o_ref, kb_ref, vb_ref,
                 kn_ref, *, out_scale):
    h = pl.program_id(1)
    qi = pl.program_id(2)
    tq = q_ref.shape[1]
    t = k_ref.shape[1]

    @pl.when(qi == 0)
    def _():
        k = k_ref[0]
        for c in range(2):
            kc = k[:, c * DA_QK:(c + 1) * DA_QK].astype(BF16)
            kb_ref[c] = kc
            kf = kc.astype(F32)
            k2 = kf * kf
            rows = t // NORM_GROUP
            gmax = k2[0:rows]
            for i in range(1, NORM_GROUP):
                gmax = jnp.maximum(gmax, k2[i * rows:(i + 1) * rows])
            kn2 = jnp.max(jnp.sum(gmax, axis=1, keepdims=True), axis=0, keepdims=True)
            kn_ref[c] = jnp.broadcast_to(kn2, kn_ref.shape[1:])
        vb_ref[...] = v_ref[0].astype(BF16)

    start = pl.multiple_of((pl.num_programs(2) - 1 - qi) * tq, tq)
    bias = (-LOG2E * slope_ref[h]) * d_ref[:, pl.ds(start, t)]
    q = (q_ref[0] * (DA_QK ** -0.5 * LOG2E)).astype(BF16)
    qf = q.astype(F32)

    def softmax_pv(row_shift):
        outs, sums = [], []
        for c in range(2):
            cs = slice(c * DA_QK, (c + 1) * DA_QK)
            s = _dg(q[:, cs], kb_ref[c], _NT) + bias
            p = jnp.exp2(s - row_shift(c, s))
            sums.append(jnp.sum(p, axis=1, keepdims=True))
            outs.append(_dg(p.astype(BF16), vb_ref[...], _NN))
        return outs, sums

    def finish(outs, sums):
        o = outs[0] / sums[0] - lam_ref[0] * (outs[1] / sums[1])
        o = o * lax.rsqrt(jnp.mean(o * o, axis=-1, keepdims=True) + EPS) * sg_ref[...]
        o_ref[0] = o * out_scale

    def norm_bound(c, s):
        qc = qf[:, c * DA_QK:(c + 1) * DA_QK]
        qn2 = jnp.sum(qc * qc, axis=1, keepdims=True)
        return jnp.sqrt(qn2 * kn_ref[c][0:1, 0:1]) * (1.0 + 2.0 ** -7)

    outs, sums = softmax_pv(norm_bound)
    safe = jnp.minimum(jnp.min(sums[0]), jnp.min(sums[1])) >= MIN_ROW_SUM

    @pl.when(safe)
    def _():
        finish(outs, sums)

    @pl.when(jnp.logical_not(safe))
    def _():
        finish(*softmax_pv(lambda c, s: jnp.max(s, axis=1, keepdims=True)))


def _diff_attention(z_qkv, lam, sub_g, lambda_init):
    bsz, t, w3 = z_qkv.shape
    nh = w3 // 3 // DA_V
    tq = _pick(t, 512, LANES)
    slopes = 2.0 ** (-8.0 * jnp.arange(1, nh + 1, dtype=F32) / nh)
    dist = jnp.abs(jnp.arange(tq, dtype=I32)[:, None] + (t - tq)
                   - jnp.arange(2 * t - tq, dtype=I32)[None, :]).astype(F32)
    smem = pl.BlockSpec(memory_space=pltpu.SMEM)
    return pl.pallas_call(
        functools.partial(_attn_kernel, out_scale=1.0 - lambda_init),
        grid=(bsz, nh, t // tq),
        in_specs=[smem, smem,
                  pl.BlockSpec((1, tq, DA_V), lambda b, h, i: (b, i, h)),
                  pl.BlockSpec((1, t, DA_V), lambda b, h, i: (b, 0, nh + h)),
                  pl.BlockSpec((1, t, DA_V), lambda b, h, i: (b, 0, 2 * nh + h)),
                  pl.BlockSpec((1, DA_V), lambda b, h, i: (0, 0)),
                  pl.BlockSpec(dist.shape, lambda b, h, i: (0, 0), pipeline_mode=pl.Buffered(1))],
        out_specs=pl.BlockSpec((1, tq, DA_V), lambda b, h, i: (b, i, h)),
        out_shape=jax.ShapeDtypeStruct((bsz, t, nh * DA_V), F32),
        scratch_shapes=[pltpu.VMEM((2, t, DA_QK), BF16), pltpu.VMEM((t, DA_V), BF16),
                        pltpu.VMEM((2, SUBLANES, LANES), F32)],
        compiler_params=_cparams("arbitrary", "arbitrary", "arbitrary"),
        name="diff_attention",
    )(slopes, lam, z_qkv, z_qkv, z_qkv, sub_g.reshape(1, DA_V), dist)


def _router_kernel(x_ref, g_ref, sc_ref, sh_ref, wr_ref, h_ref, aff_ref):
    x = x_ref[0]
    y = x * lax.rsqrt(jnp.mean(x * x, axis=-1, keepdims=True) + EPS) * g_ref[...]
    h = y * (1.0 + sc_ref[0]) + sh_ref[0]
    h_ref[0] = h.astype(h_ref.dtype)
    logits = _dot3(wr_ref[...], h, _NT)
    m = jnp.max(logits, axis=0, keepdims=True)
    ex = jnp.exp(logits - m)
    aff = ex / jnp.sum(ex, axis=0, keepdims=True)
    for j in range(aff_ref.shape[1]):
        aff_ref[:, j, :] = aff[:, j * LANES:(j + 1) * LANES]


def _router(x, g, scale, shift, w_router_t):
    bsz, t, d = x.shape
    ne = w_router_t.shape[0]
    tt = _pick(t, SUBLANES * LANES, SUBLANES * LANES)
    nb = tt // LANES
    n_t = t // tt
    row = pl.BlockSpec((1, 1, d), lambda b, i: (b, 0, 0))
    return pl.pallas_call(
        _router_kernel,
        grid=(bsz, n_t),
        in_specs=[pl.BlockSpec((1, tt, d), lambda b, i: (b, i, 0)),
                  pl.BlockSpec((1, d), lambda b, i: (0, 0)), row, row,
                  pl.BlockSpec((ne, d), lambda b, i: (0, 0))],
        out_specs=[pl.BlockSpec((1, tt, d), lambda b, i: (b, i, 0)),
                   pl.BlockSpec((ne, nb, LANES), lambda b, i: (0, b * n_t + i, 0))],
        out_shape=[jax.ShapeDtypeStruct((bsz, t, d), F32),
                   jax.ShapeDtypeStruct((ne, bsz * t // LANES, LANES), F32)],
        compiler_params=_cparams("arbitrary", "arbitrary"),
        name="router",
    )(x, g.reshape(1, d), scale, shift, w_router_t)


def _sum_all(x):
    return jnp.sum(jnp.sum(x, axis=1, keepdims=True), axis=0, keepdims=True)


SLOT_ROWS = 8


def _select_kernel(aff_ref, slots_ref, pos_ref, base_ref, rec_ref, thr_ref, base_smem, sem, *, cap):
    nb = aff_ref.shape[1]
    li = lax.broadcasted_iota(I32, (LANES, LANES), 0)
    lj = lax.broadcasted_iota(I32, (LANES, LANES), 1)
    upper = jnp.where(li <= lj, 1.0, 0.0).astype(BF16)
    ri = lax.broadcasted_iota(I32, (nb, nb), 0)
    rj = lax.broadcasted_iota(I32, (nb, nb), 1)
    lower = jnp.where(rj < ri, 1.0, 0.0).astype(BF16)

    def excl_prefix(mask):
        x = jnp.where(mask, 1.0, 0.0)
        inc = _dg(x.astype(BF16), upper, _NN)
        tot = jnp.broadcast_to(inc[:, LANES - 1:LANES], (nb, LANES))
        offs = _dg(lower, tot.astype(BF16), _NN)
        return offs + inc - x, offs

    bits_all = pltpu.bitcast(aff_ref[...], I32)

    def bit_body(i, thr):
        cand = thr | jnp.left_shift(jnp.int32(1), 30 - i)
        cnt = jnp.sum(jnp.where(bits_all >= cand, 1.0, 0.0), axis=(1, 2), keepdims=True)
        return jnp.where(cnt >= cap, cand, thr)

    thr_all = lax.fori_loop(0, 31, bit_body, jnp.zeros((aff_ref.shape[0], 1, 1), I32))
    thr_ref[...] = jnp.broadcast_to(thr_all, thr_ref.shape)

    def expert_body(e, carry):
        bits = pltpu.bitcast(aff_ref[e], I32)
        thr = thr_ref[e][0:1, 0:1]
        gt = bits > thr
        eq = bits == thr
        need = cap - _sum_all(jnp.where(gt, 1.0, 0.0))
        sel = gt | (eq & (excl_prefix(eq)[0] < need))
        pos, row_base = excl_prefix(sel)
        pos_ref[...] = jnp.where(sel, pos, -1.0).astype(I32)
        base_ref[...] = row_base.astype(I32)
        to_smem = pltpu.make_async_copy(base_ref, base_smem, sem)
        to_smem.start()
        rec_ref[...] = jnp.zeros_like(rec_ref)
        to_smem.wait()
        window_ids = lax.broadcasted_iota(I32, (2 * LANES, LANES), 0)
        lane_f = lax.broadcasted_iota(I32, (1, LANES), 1).astype(F32)

        def group_body(g, c2):
            w0 = pl.multiple_of((base_smem[g, 0] // LANES) * LANES, LANES)
            a = aff_ref[e, pl.ds(g, 1), :]
            a_hi = a.astype(BF16).astype(F32)
            a_mid = (a - a_hi).astype(BF16).astype(F32)
            a_lo = a - a_hi - a_mid
            base = jnp.full((1, LANES), LANES, F32) * g
            rec = jnp.concatenate([lane_f, base, a_hi, a_mid, a_lo,
                                   jnp.zeros((SLOT_ROWS - 5, LANES), F32)], axis=0)
            hot = jnp.where(window_ids == pos_ref[pl.ds(g, 1), :] - w0, 1.0, 0.0)
            rec_ref[:, pl.ds(w0, 2 * LANES)] += _dot1(rec, hot, _NT)
            return c2

        lax.fori_loop(0, nb, group_body, 0, unroll=4)
        rec = rec_ref[:, :cap]
        rid = lax.broadcasted_iota(I32, (SLOT_ROWS, cap), 0)
        token = rec[0:1] + rec[1:2]
        gate = rec[2:3] + rec[3:4] + rec[4:5]
        slots_ref[e] = jnp.where(rid == 0, token, jnp.where(rid == 1, gate, 0.0))
        return carry

    lax.fori_loop(0, aff_ref.shape[0], expert_body, 0)


def _select(aff, cap):
    ne, nb, _ = aff.shape
    return pl.pallas_call(
        functools.partial(_select_kernel, cap=cap),
        grid=(1,),
        in_specs=[pl.BlockSpec(aff.shape, lambda i: (0, 0, 0))],
        out_specs=pl.BlockSpec((ne, SLOT_ROWS, cap), lambda i: (0, 0, 0)),
        out_shape=jax.ShapeDtypeStruct((ne, SLOT_ROWS, cap), F32),
        scratch_shapes=[pltpu.VMEM((nb, LANES), I32), pltpu.VMEM((nb, LANES), I32),
                        pltpu.VMEM((SLOT_ROWS, -(-cap // LANES) * LANES + 2 * LANES), F32),
                        pltpu.VMEM((ne, SUBLANES, LANES), I32),
                        pltpu.SMEM((nb, LANES), I32), pltpu.SemaphoreType.DMA(())],
        compiler_params=_cparams("arbitrary"),
        name="select",
    )(aff)


def _ffn_kernel(idx_ref, h_hbm, wg_ref, wu_ref, wd_ref, gate_ref, moe_in_hbm, moe_hbm,
                xf_ref, xb_ref, acc_ref, rows_ref, sems, *, x_per_step):
    del moe_in_hbm
    e = pl.program_id(0)
    f = pl.program_id(1)
    n_e = pl.num_programs(0)
    n_f = pl.num_programs(1)
    cap = xb_ref.shape[0]

    def x_copy(ex, p):
        row = idx_ref[ex * cap + jnp.minimum(p, cap - 1)]
        return pltpu.make_async_copy(h_hbm.at[pl.ds(row, 1)], xf_ref.at[pl.ds(p, 1)], sems.at[0])

    def for_staging_rows(fn):
        def body(p, c):
            fn(p)
            return c
        lax.fori_loop(0, x_per_step * n_f, body, 0, unroll=8)

    def acc_copy(p):
        row = idx_ref[e * cap + p]
        return pltpu.make_async_copy(moe_hbm.at[pl.ds(row, 1)], rows_ref.at[pl.ds(p, 1)], sems.at[1])

    def out_copy(ex, p):
        row = idx_ref[ex * cap + p]
        return pltpu.make_async_copy(rows_ref.at[pl.ds(p, 1)], moe_hbm.at[pl.ds(row, 1)], sems.at[2])

    def for_slots(fn):
        def body(p, c):
            fn(p)
            return c
        lax.fori_loop(0, cap, body, 0, unroll=8)

    @pl.when(f == 0)
    def _():
        @pl.when(e == 0)
        def _():
            for_staging_rows(lambda p: x_copy(e, p).start())

        for_staging_rows(lambda p: x_copy(e, p).wait())
        xb_ref[...] = xf_ref[0:cap, :].astype(BF16)
        acc_ref[...] = jnp.zeros_like(acc_ref)

    nxt = jnp.where(e + 1 < n_e, e + 1, 0)
    for j in range(x_per_step):
        x_copy(nxt, f * x_per_step + j).start()

    xe = xb_ref[...]
    hid = _silu(_dot1(xe, wg_ref[0])) * _dot1(xe, wu_ref[0])
    acc_ref[...] += _dot1(hid, wd_ref[0])

    @pl.when(f == n_f // 2)
    def _():
        @pl.when(e > 0)
        def _():
            for_slots(lambda p: out_copy(e - 1, p).wait())

        for_slots(lambda p: acc_copy(p).start())

    @pl.when(f == n_f - 1)
    def _():
        for_slots(lambda p: acc_copy(p).wait())
        rows_ref[...] += acc_ref[...] * gate_ref[0]
        for_slots(lambda p: out_copy(e, p).start())

        @pl.when(e == n_e - 1)
        def _():
            for_slots(lambda p: out_copy(e, p).wait())
            for_staging_rows(lambda p: x_copy(nxt, p).wait())


def _expert_ffn(idx, gate, h2, w_gate, w_up, w_down):
    ne, cap, _ = gate.shape
    n_tok, d = h2.shape
    ff = w_gate.shape[-1]
    tf = _pick(ff, 256, LANES)
    n_f = ff // tf
    assert n_f >= 2, "accumulator rows are fetched mid-way through an expert's steps"
    x_per_step = -(-cap // n_f)
    x_rows = -(-x_per_step * n_f // SUBLANES) * SUBLANES
    any_spec = pl.BlockSpec(memory_space=pl.ANY)
    grid_spec = pltpu.PrefetchScalarGridSpec(
        num_scalar_prefetch=1,
        grid=(ne, n_f),
        in_specs=[any_spec,
                  pl.BlockSpec((1, d, tf), lambda e, f, idx_ref: (e, 0, f)),
                  pl.BlockSpec((1, d, tf), lambda e, f, idx_ref: (e, 0, f)),
                  pl.BlockSpec((1, tf, d), lambda e, f, idx_ref: (e, f, 0)),
                  pl.BlockSpec((1, cap, 1), lambda e, f, idx_ref: (e, 0, 0)),
                  any_spec],
        out_specs=any_spec,
        scratch_shapes=[pltpu.VMEM((x_rows, d), F32), pltpu.VMEM((cap, d), BF16),
                        pltpu.VMEM((cap, d), F32), pltpu.VMEM((cap, d), F32),
                        pltpu.SemaphoreType.DMA((3,))])
    return pl.pallas_call(
        functools.partial(_ffn_kernel, x_per_step=x_per_step),
        grid_spec=grid_spec,
        out_shape=jax.ShapeDtypeStruct((n_tok, d), F32),
        input_output_aliases={6: 0},
        compiler_params=_cparams("arbitrary", "arbitrary"),
        name="expert_ffn",
    )(idx, h2, w_gate, w_up, w_down, gate, jnp.zeros((n_tok, d), F32))


def _residual_out_kernel(x_ref, m_ref, g_ref, fg_ref, o_ref, *, final_norm):
    x = x_ref[0] + g_ref[0] * m_ref[0]
    if final_norm:
        x = x * lax.rsqrt(jnp.mean(x * x, axis=-1, keepdims=True) + EPS) * fg_ref[...]
    o_ref[0] = x


def _residual_out(x, moe, gate2, final_g, final_norm):
    bsz, t, d = x.shape
    tt = _pick(t, 512, SUBLANES)
    tile = pl.BlockSpec((1, tt, d), lambda b, i: (b, i, 0))
    return pl.pallas_call(
        functools.partial(_residual_out_kernel, final_norm=final_norm),
        grid=(bsz, t // tt),
        in_specs=[tile, tile, pl.BlockSpec((1, 1, d), lambda b, i: (b, 0, 0)),
                  pl.BlockSpec((1, d), lambda b, i: (0, 0))],
        out_specs=tile,
        out_shape=jax.ShapeDtypeStruct((bsz, t, d), F32),
        compiler_params=_cparams("arbitrary", "arbitrary"),
        name="residual_out",
    )(x, moe, gate2, final_g.reshape(1, d))


def _encoder_layer(x, ada, lw, lambda_init, final_g, final_norm):
    bsz, t, d = x.shape
    shift1, scale1, gate1, shift2, scale2, gate2 = [a[:, None, :] for a in jnp.split(ada, 6, axis=-1)]

    h = _norm_mod(x, lw["norm1_g"], scale1, shift1).reshape(bsz * t, d)
    z_rkv = _matmul(h, lw["w_rkv"]).reshape(bsz, t, -1)
    z_lo = _matmul(h, lw["w_lo"]).reshape(bsz, t, -1)
    z_qkv = _matmul(h, lw["w_qkv"]).reshape(bsz, t, -1)
    z_gates = _matmul(h, lw["w_gates"]).reshape(bsz, t, -1)

    (r, v, g, bonus, kd_f, b_f, an_f, cum_f, kd_b, b_b, an_b, cum_b) = _rw_prep(
        z_rkv, z_lo, lw["mu_r"], lw["mu_l"], lw["w0"], lw["w2"], lw["a0"], lw["a2"], lw["g2"],
        lw["k_k"], lw["k_a"], lw["r_k"])
    y_f, y_b = _rw_scan((r, v, kd_f, b_f, an_f, cum_f), (r, v, kd_b, b_b, an_b, cum_b))

    att = _diff_attention(z_qkv, lw["lam"], lw["sub_g"], lambda_init)
    merged = _merge(y_f, y_b, bonus, g, att, z_gates, lw["lnx_w"], lw["lnx_b"])
    x1 = _matmul_residual(merged, lw["w_out"], x, gate1)

    n_tok = bsz * t
    ne = lw["w_router_t"].shape[0]
    cap = max(1, CAPACITY_FACTOR * n_tok // ne)
    h2, aff = _router(x1, lw["norm2_g"], scale2, shift2, lw["w_router_t"])
    slots = _select(aff, cap)
    idx = slots[:, 0, :].astype(I32).reshape(ne * cap)
    gate = slots[:, 1, :].reshape(ne, cap, 1)
    moe = _expert_ffn(idx, gate, h2.reshape(n_tok, d), lw["w_gate"], lw["w_up"], lw["w_down"])
    return _residual_out(x1, moe.reshape(bsz, t, d), gate2, final_g, final_norm)


def kernel(x_prompt, x_sample, c_prompt, c_sample, w_ada, b_ada, norm1_g, w_in, mu_shift, w0, w2, a0, a2, g2, k_k, k_a, r_k, lnx_w, lnx_b, lam_q1, lam_k1, lam_q2, lam_k2, sub_g, w_out, norm2_g, w_router, w_gate, w_up, w_down, final_g):
    depth = w_ada.shape[0]
    d = x_prompt.shape[-1]
    w3 = 3 * d
    n_lo = 2 * w2.shape[2] + 2 * a2.shape[2] + g2.shape[1]
    xs = [x_prompt, x_sample]
    n_rows = [x_prompt.shape[0], x_sample.shape[0]]
    c_all = jnp.concatenate([c_prompt, c_sample], axis=0)
    pad = (-c_all.shape[0]) % SUBLANES
    c_all = jnp.pad(c_all, ((0, pad), (0, 0)))
    for l in range(depth):
        lambda_init = 0.8 - 0.6 * math.exp(-0.3 * l)
        w_in_l = w_in[l].astype(BF16)
        lam = (jnp.exp(jnp.sum(lam_q1[l] * lam_k1[l])) - jnp.exp(jnp.sum(lam_q2[l] * lam_k2[l]))
               + lambda_init).astype(F32).reshape(1)
        lw = dict(
            norm1_g=norm1_g[l], norm2_g=norm2_g[l],
            w_rkv=w_in_l[:, :w3], w_lo=w_in_l[:, w3:w3 + n_lo],
            w_qkv=w_in_l[:, w3 + n_lo:w3 + n_lo + 3 * d], w_gates=w_in_l[:, w3 + n_lo + 3 * d:],
            mu_r=mu_shift[l, :w3], mu_l=mu_shift[l, w3:w3 + n_lo],
            w0=w0[l], w2=w2[l], a0=a0[l], a2=a2[l], g2=g2[l], k_k=k_k[l], k_a=k_a[l], r_k=r_k[l],
            lnx_w=lnx_w[l], lnx_b=lnx_b[l], lam=lam, sub_g=sub_g[l],
            w_out=w_out[l].astype(BF16), w_router_t=w_router[l].T,
            w_gate=w_gate[l], w_up=w_up[l], w_down=w_down[l])
        ada = _ada(c_all, w_ada[l], b_ada[l])
        off = 0
        for gi in range(2):
            xs[gi] = _encoder_layer(xs[gi], ada[off:off + n_rows[gi]], lw, lambda_init, final_g,
                                    final_norm=(l == depth - 1))
            off += n_rows[gi]
    return tuple(xs)
```
